```python
import jax
import jax.numpy as jnp
from jax import lax
import numpy as np

D_MODEL = 1024
BATCH = 8
SEQ = 4096
DEPTH = 4

CTX_LEN = 256
GRID_W = 64
N_DIR = 2
MIX_W = 512
N_BRANCH = 3
HG_HEADS = 4
HG_DK = 128
HG_DV = MIX_W // HG_HEADS
ML_HEADS = 4
ML_DV = MIX_W // ML_HEADS
ML_DQK = ML_DV // 2
LRU_W = MIX_W
LRU_BLOCKS = 8
LRU_BD = LRU_W // LRU_BLOCKS
LRU_C = 8.0
CONV_W = 4
CONV_LEFT = 2
FFN_HIDDEN = ((8 * D_MODEL // 3 + 255) // 256) * 256
CHUNK = 64
EPS = 1e-6
NEG_BIG = -1e30
LB_TINY = 1e-30
IN_SIZES = (
    HG_HEADS * HG_DK,
    HG_HEADS * HG_DV,
    HG_HEADS * HG_DV,
    N_DIR * HG_HEADS * HG_DK,
    ML_HEADS * ML_DQK,
    ML_HEADS * ML_DQK,
    ML_HEADS * ML_DV,
    ML_HEADS * ML_DV,
    N_DIR * ML_HEADS,
    N_DIR * ML_HEADS,
    LRU_W,
    LRU_W,
    N_BRANCH * D_MODEL,
)
N_IN = sum(IN_SIZES)
ML_FGATE_IDX = 9
N_MIX_PARTS = 12

kernel_name = 'hybrid_hgrn2_mlstm_rglru_dit_trunk'


def rms_norm(x, w):
    xf = x.astype(jnp.float32)
    y = xf * lax.rsqrt(jnp.mean(xf * xf, axis=-1, keepdims=True) + EPS)
    return (y * w).astype(x.dtype)


def modulate(h, shift, scale):
    return h * (1 + scale) + shift


def grid_transpose(h, rows, cols):
    b, _, d = h.shape
    return h.reshape(b, rows, cols, d).transpose(0, 2, 1, 3).reshape(b, rows * cols, d)


def split_in(u):
    return jnp.split(u, np.cumsum(IN_SIZES)[:-1].tolist(), axis=-1)


def to_heads(a, n_heads):
    b, t, _ = a.shape
    return a.reshape(b, t, n_heads, -1).transpose(0, 2, 1, 3)


def to_chunks(a):
    b, h, t = a.shape[:3]
    return jnp.moveaxis(a.reshape(b, h, t // CHUNK, CHUNK, *a.shape[3:]), 2, 0)


def from_chunks(a):
    a = jnp.moveaxis(a, 0, 2)
    return a.reshape(a.shape[0], a.shape[1], -1, *a.shape[4:])


def head_rms_norm(h, w):
    h = h * lax.rsqrt(jnp.mean(h * h, axis=-1, keepdims=True) + EPS)
    b, nh, t, dh = h.shape
    return h.transpose(0, 2, 1, 3).reshape(b, t, nh * dh) * w


def _orient(a, axis, rev):
    return jnp.flip(a, axis) if rev else a


def hgrn2_chunkwise(q, k, v, log_f, s0):
    mask = jnp.tril(jnp.ones((CHUNK, CHUNK), dtype=bool))[:, :, None]

    def step(s_prev, inp):
        qc, kc, vc, gc = inp
        b = jnp.cumsum(gc, axis=2)
        log_decay = jnp.where(mask, b[:, :, :, None, :] - b[:, :, None, :, :], NEG_BIG)
        scores = jnp.einsum('bhtk,bhsk,bhtsk->bhts', qc, kc, jnp.exp(log_decay))
        o = scores @ vc + (qc * jnp.exp(b)) @ s_prev
        b_last = b[:, :, -1:, :]
        s_new = (jnp.exp(b_last[:, :, 0, :, None]) * s_prev
                 + jnp.einsum('bhsk,bhsv->bhkv', kc * jnp.exp(b_last - b), vc))
        return s_new, o

    s_last, o = lax.scan(step, s0, tuple(to_chunks(a) for a in (q, k, v, log_f)))
    return from_chunks(o), s_last


def mlstm_chunkwise(q, k, v, log_i, log_f, state):
    mask = jnp.tril(jnp.ones((CHUNK, CHUNK), dtype=bool))

    def step(carry, inp):
        c_mem, n_mem, m_prev = carry
        qc, kc, vc, ic, fc = inp
        b = jnp.cumsum(fc, axis=-1)
        log_d = jnp.where(mask, b[..., :, None] - b[..., None, :] + ic[..., None, :], NEG_BIG)
        log_inter = b + m_prev[..., None]
        m_row = jnp.maximum(jnp.max(log_d, axis=-1), log_inter)
        s = jnp.einsum('bhtk,bhsk->bhts', qc, kc) * jnp.exp(log_d - m_row[..., None])
        w_inter = jnp.exp(log_inter - m_row)
        num = s @ vc + w_inter[..., None] * (qc @ c_mem)
        den = jnp.sum(s, axis=-1) + w_inter * jnp.einsum('bhtk,bhk->bht', qc, n_mem)
        h = num / jnp.maximum(jnp.abs(den), jnp.exp(-m_row))[..., None]
        b_last = b[..., -1]
        log_w = b_last[..., None] - b + ic
        m_new = jnp.maximum(b_last + m_prev, jnp.max(log_w, axis=-1))
        w_s = jnp.exp(log_w - m_new[..., None])
        carry_decay = jnp.exp(b_last + m_prev - m_new)
        c_new = carry_decay[..., None, None] * c_mem + jnp.einsum('bhs,bhsk,bhsv->bhkv', w_s, kc, vc)
        n_new = carry_decay[..., None] * n_mem + jnp.einsum('bhs,bhsk->bhk', w_s, kc)
        return (c_new, n_new, m_new), h

    carry, h = lax.scan(step, state, tuple(to_chunks(a) for a in (q, k, v, log_i, log_f)))
    return from_chunks(h), carry


def _linear_combine(left, right):
    a_l, b_l = left
    a_r, b_r = right
    return a_l * a_r, a_r * b_l + b_r


def rglru(xc, r_pre, i_pre, lam, h0):
    log_a = -LRU_C * jax.nn.softplus(-lam) * jax.nn.sigmoid(r_pre)
    a = jnp.exp(log_a)
    u = jnp.sqrt(jnp.maximum(-jnp.expm1(2.0 * log_a), 0.0)) * (jax.nn.sigmoid(i_pre) * xc)
    a_cum, h = lax.associative_scan(_linear_combine, (a, u), axis=1)
    h = h + a_cum * h0[:, None, :]
    return h, h[:, -1]


def depthwise_conv(x, w, b):
    t = x.shape[1]
    xp = jnp.pad(x, ((0, 0), (CONV_LEFT, CONV_W - 1 - CONV_LEFT), (0, 0)))
    return sum(xp[:, j:j + t] * w[j] for j in range(CONV_W)) + b


def zero_states(batch):
    f32 = jnp.float32
    one = (jnp.zeros((batch, HG_HEADS, HG_DK, HG_DV), f32),
           jnp.zeros((batch, ML_HEADS, ML_DQK, ML_DV), f32),
           jnp.zeros((batch, ML_HEADS, ML_DQK), f32),
           jnp.zeros((batch, ML_HEADS), f32),
           jnp.zeros((batch, LRU_W), f32))
    return (one, one)


def token_mixers(parts, lb, hg_norm_w, ml_norm_w, conv_w, conv_b, lru_gate_w, lru_gate_b,
                 lru_lambda, init, with_outputs):
    hg_q, hg_i, hg_g, hg_f, ml_q, ml_k, ml_v, ml_o, ml_ig, ml_fg, lru_x, lru_y = parts
    f32 = jnp.float32
    dt = hg_q.dtype
    bsz, t, _ = hg_q.shape
    q_hg = to_heads(jax.nn.silu(hg_q.astype(f32)), HG_HEADS)
    v_hg = to_heads(hg_i.astype(f32), HG_HEADS)
    f_pre = hg_f.astype(f32).reshape(bsz, t, N_DIR, HG_HEADS * HG_DK)
    log_lb = jnp.log(jnp.maximum(lb, LB_TINY))
    logf_hg = jnp.logaddexp(log_lb, jnp.log1p(-lb) + jax.nn.log_sigmoid(f_pre))
    k_hg = (1.0 - lb) * jax.nn.sigmoid(-f_pre)
    q_ml = to_heads(ml_q.astype(f32), ML_HEADS) * ML_DQK ** -0.5
    k_ml = to_heads(ml_k.astype(f32), ML_HEADS)
    v_ml = to_heads(ml_v.astype(f32), ML_HEADS)
    logi_ml = ml_ig.astype(f32).reshape(bsz, t, N_DIR, ML_HEADS).transpose(2, 0, 3, 1)
    logf_ml = jax.nn.log_sigmoid(ml_fg.astype(f32).reshape(bsz, t, N_DIR, ML_HEADS)).transpose(2, 0, 3, 1)
    xc = depthwise_conv(lru_x, conv_w, conv_b).astype(f32)
    gates = jnp.einsum('btnd,zgnde->zgbtne', xc.reshape(bsz, t, LRU_BLOCKS, LRU_BD),
                       lru_gate_w.astype(f32)).reshape(N_DIR, 2, bsz, t, LRU_W)
    gates = gates + lru_gate_b.astype(f32)[:, :, None, None, :]

    o_hg = 0.0
    h_ml = 0.0
    h_lru = 0.0
    finals = []
    for d in range(N_DIR):
        rev = d == 1
        s_hg, c_ml, n_ml, m_ml, s_lru = init[d]
        o, s_hg = hgrn2_chunkwise(_orient(q_hg, 2, rev),
                                  _orient(to_heads(k_hg[:, :, d], HG_HEADS), 2, rev),
                                  _orient(v_hg, 2, rev),
                                  _orient(to_heads(logf_hg[:, :, d], HG_HEADS), 2, rev), s_hg)
        o_hg = o_hg + _orient(o, 2, rev)
        h, (c_ml, n_ml, m_ml) = mlstm_chunkwise(_orient(q_ml, 2, rev), _orient(k_ml, 2, rev),
                                                _orient(v_ml, 2, rev), _orient(logi_ml[d], 2, rev),
                                                _orient(logf_ml[d], 2, rev), (c_ml, n_ml, m_ml))
        h_ml = h_ml + _orient(h, 2, rev)
        hl, s_lru = rglru(_orient(xc, 1, rev), _orient(gates[d, 0], 1, rev),
                          _orient(gates[d, 1], 1, rev), lru_lambda[d].astype(f32), s_lru)
        h_lru = h_lru + _orient(hl, 1, rev)
        finals.append((s_hg, c_ml, n_ml, m_ml, s_lru))
    if not with_outputs:
        return None, finals
    a_out = head_rms_norm(o_hg, hg_norm_w) * jax.nn.silu(hg_g.astype(f32))
    b_out = head_rms_norm(h_ml, ml_norm_w) * jax.nn.sigmoid(ml_o.astype(f32))
    c_out = h_lru * jax.nn.gelu(lru_y.astype(f32))
    return (a_out.astype(dt), b_out.astype(dt), c_out.astype(dt)), finals


def merge_branches(branches, gate_pre, w_branch, w_out):
    bsz, t, _ = gate_pre.shape
    g = jax.nn.sigmoid(gate_pre.reshape(bsz, t, N_BRANCH, D_MODEL))
    merged = sum(g[:, :, n] * (branches[n] @ w_branch[n]) for n in range(N_BRANCH))
    return merged @ w_out


def swiglu(h, w_in, w_out):
    gate, up = jnp.split(h @ w_in, 2, axis=-1)
    return (jax.nn.silu(gate) * up) @ w_out


def setup_inputs(seed: int = 0) -> dict:
    key = jax.random.key(seed)
    ks = jax.random.split(key, 24)
    f32 = jnp.float32

    def nrm(k, shape, fan_in):
        return jax.random.normal(k, shape, f32) * fan_in ** -0.5

    def small(k, shape):
        return 0.02 * jax.random.normal(k, shape, f32)

    b_in = small(ks[8], (DEPTH, N_IN))
    f_off = int(sum(IN_SIZES[:ML_FGATE_IDX]))
    f_bias = jnp.tile(jnp.linspace(3.0, 6.0, ML_HEADS, dtype=f32), N_DIR)
    b_in = b_in.at[:, f_off:f_off + N_DIR * ML_HEADS].add(f_bias)
    a_base = jax.random.uniform(ks[16], (DEPTH, N_DIR, LRU_W), f32, 0.9, 0.999)
    s_base = a_base ** (1.0 / LRU_C)
    return {
        'x': jax.random.normal(ks[0], (BATCH, SEQ, D_MODEL), f32),
        'c': jax.random.normal(ks[1], (BATCH, D_MODEL), f32),
        'ctx': jax.random.normal(ks[2], (BATCH, CTX_LEN, D_MODEL), f32),
        'c_ctx': jax.random.normal(ks[3], (D_MODEL,), f32),
        'w_ada': 0.5 * nrm(ks[4], (DEPTH, D_MODEL, 6 * D_MODEL), D_MODEL),
        'b_ada': small(ks[5], (DEPTH, 6 * D_MODEL)),
        'ln1': 1.0 + small(ks[6], (DEPTH, D_MODEL)),
        'w_in': nrm(ks[7], (DEPTH, D_MODEL, N_IN), D_MODEL),
        'b_in': b_in,
        'hg_lb_raw': 0.5 * jax.random.normal(ks[9], (DEPTH, N_DIR, HG_HEADS * HG_DK), f32),
        'hg_norm': 1.0 + small(ks[10], (DEPTH, MIX_W)),
        'ml_norm': 1.0 + small(ks[11], (DEPTH, MIX_W)),
        'conv_w': nrm(ks[12], (DEPTH, CONV_W, LRU_W), CONV_W),
        'conv_b': small(ks[13], (DEPTH, LRU_W)),
        'lru_gate_w': nrm(ks[14], (DEPTH, N_DIR, 2, LRU_BLOCKS, LRU_BD, LRU_BD), LRU_BD),
        'lru_gate_b': small(ks[15], (DEPTH, N_DIR, 2, LRU_W)),
        'lru_lambda': jnp.log(s_base) - jnp.log1p(-s_base),
        'w_branch': nrm(ks[17], (DEPTH, N_BRANCH, MIX_W, D_MODEL), MIX_W),
        'w_out': nrm(ks[18], (DEPTH, D_MODEL, D_MODEL), D_MODEL),
        'ln2': 1.0 + small(ks[19], (DEPTH, D_MODEL)),
        'w_ffn_in': nrm(ks[20], (DEPTH, D_MODEL, 2 * FFN_HIDDEN), D_MODEL),
        'w_ffn_out': nrm(ks[21], (DEPTH, FFN_HIDDEN, D_MODEL), FFN_HIDDEN),
        'final_norm': 1.0 + small(ks[22], (D_MODEL,)),
    }


def reference(x, c, ctx, c_ctx, w_ada, b_ada, ln1, w_in, b_in, hg_lb_raw, hg_norm, ml_norm,
              conv_w, conv_b, lru_gate_w, lru_gate_b, lru_lambda, w_branch, w_out, ln2,
              w_ffn_in, w_ffn_out, final_norm):
    bsz, seq, _ = x.shape
    rows = seq // GRID_W
    lb_p = jax.nn.softmax(hg_lb_raw.astype(jnp.float32), axis=0)
    lb_all = jnp.cumsum(lb_p, axis=0) - lb_p[0]
    sc = jax.nn.silu(c)
    sc_ctx = jax.nn.silu(c_ctx)
    for l in range(DEPTH):
        last = l == DEPTH - 1
        col_major = l % 2 == 1
        mod = jnp.split(sc @ w_ada[l] + b_ada[l], 6, axis=-1)
        mod_c = jnp.split(sc_ctx @ w_ada[l] + b_ada[l], 6, axis=-1)
        mix_params = (lb_all[l], hg_norm[l], ml_norm[l], conv_w[l], conv_b[l],
                      lru_gate_w[l], lru_gate_b[l], lru_lambda[l])
        hx = modulate(rms_norm(x, ln1[l]), mod[0][:, None], mod[1][:, None])
        hc = modulate(rms_norm(ctx, ln1[l]), mod_c[0], mod_c[1])
        if col_major:
            hx = grid_transpose(hx, rows, GRID_W)
        px = split_in(hx @ w_in[l] + b_in[l])
        pc = split_in(hc @ w_in[l] + b_in[l])
        br_c, st_c = token_mixers(pc[:N_MIX_PARTS], *mix_params, zero_states(ctx.shape[0]), not last)
        br_x, _ = token_mixers(px[:N_MIX_PARTS], *mix_params, st_c, True)
        yx = merge_branches(br_x, px[N_MIX_PARTS], w_branch[l], w_out[l])
        if col_major:
            yx = grid_transpose(yx, GRID_W, rows)
        x = x + mod[2][:, None] * yx
        x = x + mod[5][:, None] * swiglu(
            modulate(rms_norm(x, ln2[l]), mod[3][:, None], mod[4][:, None]), w_ffn_in[l], w_ffn_out[l])
        if not last:
            ctx = ctx + mod_c[2] * merge_branches(br_c, pc[N_MIX_PARTS], w_branch[l], w_out[l])
            ctx = ctx + mod_c[5] * swiglu(
                modulate(rms_norm(ctx, ln2[l]), mod_c[3], mod_c[4]), w_ffn_in[l], w_ffn_out[l])
    return rms_norm(x, final_norm)
```

```python
import functools

import numpy as np
import jax
import jax.numpy as jnp
from jax import lax
from jax.experimental import pallas as pl
from jax.experimental.pallas import tpu as pltpu

F32 = jnp.float32
BF16 = jnp.bfloat16

GRID_W = 64
MIX_W = 512
N_HEADS = 4
HEAD_DV = 128
HG_DK = 128
ML_DQK = 64
LRU_BLOCKS = 8
LRU_BD = 64
LRU_C = 8.0
EPS = 1e-6
NEG_BIG = -1e30
LB_TINY = 1e-30
GATE_LANES = 128
CHUNK = 128
FFN_TM = 512
VMEM_LIMIT = 56 * 1024 * 1024


def _cparams(sem):
    return pltpu.CompilerParams(dimension_semantics=sem, vmem_limit_bytes=VMEM_LIMIT)


def _sigmoid(x):
    return 1.0 / (1.0 + jnp.exp(-x))


def _log_sigmoid(x):
    return jnp.minimum(x, 0.0) - jnp.log(1.0 + jnp.exp(-jnp.abs(x)))


def _dot(a, b):
    return jnp.dot(a, b, preferred_element_type=F32)


def _dot_nt(a, b):
    return lax.dot_general(a, b, (((1,), (1,)), ((), ())), preferred_element_type=F32)


def _dot_tn(a, b):
    return lax.dot_general(a, b, (((0,), (0,)), ((), ())), preferred_element_type=F32)


def _split_hi_lo(x):
    hi = x.astype(BF16)
    lo = (x - hi.astype(F32)).astype(BF16)
    return hi, lo


def _norm_mod(x, ln, shift, scale):
    y = x * lax.rsqrt(jnp.mean(x * x, axis=-1, keepdims=True) + EPS) * ln
    return y * (1.0 + scale) + shift


def _mirror(m):
    return m[..., ::-1, ::-1].copy()


@functools.lru_cache(maxsize=None)
def _hgrn2_consts(C):
    n_lv = int(np.log2(C))
    mats, masks = [], []
    r = np.arange(C)
    for lv in range(n_lv):
        s = C >> (lv + 1)
        base = (r // (2 * s)) * (2 * s)
        mid = base + s - 1
        odd = r >= base + s
        m = np.zeros((C, C), np.float32)
        for t in range(C):
            if odd[t]:
                m[t, mid[t] + 1:t + 1] = 1.0
            else:
                m[t, t + 1:mid[t] + 1] = 1.0
        mats.append(m)
        same = base[:, None] == base[None, :]
        masks.append((same & odd[:, None] & (~odd)[None, :]).astype(np.float32))
    mats.append(np.tril(np.ones((C, C), np.float32)))
    mats.append(np.triu(np.ones((C, C), np.float32), 1))
    masks.append(np.eye(C, dtype=np.float32))
    mats = np.stack(mats)
    masks = np.stack(masks)
    mats = np.stack([mats, _mirror(mats)])
    masks = np.stack([masks, _mirror(masks)])
    mst = mats.reshape(2, (n_lv + 2) * C, C)
    mst = np.concatenate([mst, mst], axis=-1)
    return mst, masks, n_lv


@functools.lru_cache(maxsize=None)
def _mlstm_consts(C):
    tri = np.tril(np.ones((C, C), np.float32))
    tri = np.stack([tri, _mirror(tri)])
    tri_cat = np.concatenate([tri, tri], axis=-1)
    tri_t = np.transpose(tri, (0, 2, 1))
    tri_t_cat = np.concatenate([tri_t, tri_t], axis=1)
    return tri_cat, tri_t_cat, tri


def _chunk_index(d, j, nctx, nch):
    bw = jnp.where(j < nctx, nctx - 1 - j, nch - 1 + nctx - j)
    return jnp.where(d == 0, j, bw)


def _ada_body(c_ref, w_ref, b_ref, o_ref):
    cc = c_ref[...]
    s = cc * _sigmoid(cc)
    o_ref[...] = jnp.dot(s, w_ref[...], preferred_element_type=F32,
                         precision=lax.Precision.HIGHEST) + b_ref[...]


def _ada_call(c_all, w_ada, b_ada):
    depth, d_model, n6 = w_ada.shape
    rows = c_all.shape[0]
    tn = 1536
    return pl.pallas_call(
        _ada_body,
        grid=(depth, n6 // tn),
        in_specs=[
            pl.BlockSpec((rows, d_model), lambda l, n: (0, 0)),
            pl.BlockSpec((None, d_model, tn), lambda l, n: (l, 0, n)),
            pl.BlockSpec((None, 1, tn), lambda l, n: (l, 0, n)),
        ],
        out_specs=pl.BlockSpec((None, rows, tn), lambda l, n: (l, 0, n)),
        out_shape=jax.ShapeDtypeStruct((depth, rows, n6), F32),
        compiler_params=_cparams(("arbitrary", "arbitrary")),
        name="ada_mod",
    )(c_all, w_ada, b_ada.reshape(depth, 1, n6))


def _lb_body(raw_ref, o_ref):
    raw = raw_ref[...]
    depth = raw.shape[0]
    e = jnp.exp(raw - jnp.max(raw, axis=0, keepdims=True))
    p = e / jnp.sum(e, axis=0, keepdims=True)
    acc = jnp.zeros_like(p[0:1])
    for l in range(depth):
        acc = acc + p[l:l + 1]
        o_ref[l:l + 1, :] = acc - p[0:1]


def _lb_call(hg_lb_raw):
    depth = hg_lb_raw.shape[0]
    raw = hg_lb_raw.reshape(depth, -1)
    return pl.pallas_call(
        _lb_body,
        out_shape=jax.ShapeDtypeStruct(raw.shape, F32),
        name="hgrn2_lower_bounds",
    )(raw)


def _in_proj_body(ctx_ref, x_ref, sh_ref, sc_ref, ln_ref, *rest, n_out):
    w_refs = rest[:n_out]
    b_refs = rest[n_out:2 * n_out]
    o_refs = rest[2 * n_out:]
    t = pl.program_id(1)
    xin = jnp.where(t == 0, ctx_ref[...], x_ref[...])
    h = _norm_mod(xin, ln_ref[...], sh_ref[...], sc_ref[...]).astype(BF16)
    for w_ref, b_ref, o_ref in zip(w_refs, b_refs, o_refs):
        o_ref[...] = (_dot(h, w_ref[...]) + b_ref[...]).astype(o_ref.dtype)


def _in_proj_call(l, ctx, x, mods4, ln1, ws, bs, out_dtypes):
    bsz, lc, d_model = ctx.shape
    seq = x.shape[1]
    tm = lc
    nt = 1 + seq // tm
    n_out = len(ws)

    def mod_spec(chunk):
        return pl.BlockSpec((None, None, 1, d_model),
                            lambda b, t: (l, jnp.where(t == 0, bsz, b), 0, chunk))

    in_specs = [
        pl.BlockSpec((None, tm, d_model), lambda b, t: (b, 0, 0)),
        pl.BlockSpec((None, tm, d_model), lambda b, t: (b, jnp.maximum(t - 1, 0), 0)),
        mod_spec(0), mod_spec(1),
        pl.BlockSpec((None, 1, d_model), lambda b, t: (l, 0, 0)),
    ]
    in_specs += [pl.BlockSpec((None, d_model, w.shape[-1]), lambda b, t: (l, 0, 0)) for w in ws]
    in_specs += [pl.BlockSpec((None, 1, w.shape[-1]), lambda b, t: (l, 0, 0)) for w in ws]
    out_specs = [pl.BlockSpec((None, tm, w.shape[-1]), lambda b, t: (b, t, 0)) for w in ws]
    out_shape = [jax.ShapeDtypeStruct((bsz, lc + seq, w.shape[-1]), dt) for w, dt in zip(ws, out_dtypes)]
    return pl.pallas_call(
        functools.partial(_in_proj_body, n_out=n_out),
        grid=(bsz, nt),
        in_specs=in_specs,
        out_specs=out_specs,
        out_shape=out_shape,
        compiler_params=_cparams(("arbitrary", "arbitrary")),
        name="in_proj",
    )(ctx, x, mods4, mods4, ln1, *ws, *bs)


def _hgrn2_body(q_ref, v_ref, f_ref, lb_ref, mst_ref, msk_ref, o_ref, st_ref, *, C, n_lv):
    @pl.when(pl.program_id(2) == 0)
    def _():
        st_ref[...] = jnp.zeros_like(st_ref)

    q = q_ref[...].astype(F32)
    q = q * _sigmoid(q)
    v = v_ref[...]
    fp = f_ref[...]
    lb = lb_ref[...]
    e = jnp.exp(-jnp.abs(fp))
    inv = 1.0 / (1.0 + e)
    pos = fp >= 0.0
    sig = jnp.where(pos, inv, e * inv)
    nsig = jnp.where(pos, e * inv, inv)
    g = jnp.log(jnp.maximum(lb, LB_TINY) + (1.0 - lb) * sig)
    kk = (1.0 - lb) * nsig
    g_hi, g_lo = _split_hi_lo(g)
    ex = _dot(mst_ref[...], jnp.concatenate([g_hi, g_lo], axis=0))
    tot = jnp.sum(g, axis=0, keepdims=True)

    for h in range(N_HEADS):
        sl = slice(h * HG_DK, (h + 1) * HG_DK)
        qh = q[:, sl]
        kh = kk[:, sl]
        p = msk_ref[n_lv] * _dot_nt(qh.astype(BF16), kh.astype(BF16))
        for lv in range(n_lv):
            w = jnp.exp(ex[lv * C:(lv + 1) * C, sl])
            p = p + msk_ref[lv] * _dot_nt((qh * w).astype(BF16), (kh * w).astype(BF16))
        st = st_ref[h]
        qb = (qh * jnp.exp(ex[n_lv * C:(n_lv + 1) * C, sl])).astype(BF16)
        vh = v[:, sl]
        o_ref[:, sl] = _dot(p.astype(BF16), vh) + _dot_nt(qb, st.astype(BF16))
        kb = (kh * jnp.exp(ex[(n_lv + 1) * C:(n_lv + 2) * C, sl])).astype(BF16)
        st_ref[h] = st * jnp.exp(tot[:, sl]) + _dot_tn(vh, kb)


def _hgrn2_call(l, hg_a, hg_f, lb4, lc):
    bsz, tt, _ = hg_a.shape
    C = CHUNK
    nch, nctx = tt // C, lc // C
    mst, msk, n_lv = _hgrn2_consts(C)
    mst = jnp.asarray(mst, BF16)
    msk = jnp.asarray(msk, F32)
    cidx = functools.partial(_chunk_index, nctx=nctx, nch=nch)
    return pl.pallas_call(
        functools.partial(_hgrn2_body, C=C, n_lv=n_lv),
        grid=(bsz, 2, nch),
        in_specs=[
            pl.BlockSpec((None, C, MIX_W), lambda b, d, j: (b, cidx(d, j), 0)),
            pl.BlockSpec((None, C, MIX_W), lambda b, d, j: (b, cidx(d, j), 1)),
            pl.BlockSpec((None, C, MIX_W), lambda b, d, j: (b, cidx(d, j), d)),
            pl.BlockSpec((None, None, 1, MIX_W), lambda b, d, j: (l, d, 0, 0)),
            pl.BlockSpec((None,) + mst.shape[1:], lambda b, d, j: (d, 0, 0)),
            pl.BlockSpec((None,) + msk.shape[1:], lambda b, d, j: (d, 0, 0, 0)),
        ],
        out_specs=pl.BlockSpec((None, None, C, MIX_W), lambda b, d, j: (b, d, cidx(d, j), 0)),
        out_shape=jax.ShapeDtypeStruct((bsz, 2, tt, MIX_W), F32),
        scratch_shapes=[pltpu.VMEM((N_HEADS, HEAD_DV, HG_DK), F32)],
        compiler_params=_cparams(("arbitrary", "arbitrary", "arbitrary")),
        name="hgrn2",
    )(hg_a, hg_a, hg_f, lb4, mst, msk)


def _mlstm_body(q_ref, k_ref, v_ref, g_ref, tri_ref, trit_ref, msk_ref, o_ref, c_ref, m_ref, *, C):
    @pl.when(pl.program_id(2) == 0)
    def _():
        c_ref[...] = jnp.zeros_like(c_ref)
        m_ref[...] = jnp.zeros_like(m_ref)

    gates = g_ref[...]
    gates_t = gates.T
    lf = _log_sigmoid(gates)
    lf_t = _log_sigmoid(gates_t)
    lf_hi, lf_lo = _split_hi_lo(lf)
    b_cols = _dot(tri_ref[...], jnp.concatenate([lf_hi, lf_lo], axis=0))
    lft_hi, lft_lo = _split_hi_lo(lf_t)
    b_rows = _dot(jnp.concatenate([lft_hi, lft_lo], axis=1), trit_ref[...])
    tot = jnp.sum(lf, axis=0, keepdims=True)
    allowed = msk_ref[...] > 0.0
    ones = jnp.ones((C, HEAD_DV), BF16)
    scale = ML_DQK ** -0.5

    for h in range(N_HEADS):
        b_col = b_cols[:, N_HEADS + h:N_HEADS + h + 1]
        b_row = b_rows[N_HEADS + h:N_HEADS + h + 1, :]
        i_row = gates_t[h:h + 1, :]
        i_col = gates[:, h:h + 1]
        m_prev = m_ref[h][0:1, 0:1]
        log_d = jnp.where(allowed, b_col - b_row + i_row, NEG_BIG)
        log_inter = b_col + m_prev
        m_row = jnp.maximum(jnp.max(log_d, axis=-1, keepdims=True), log_inter)
        qh = q_ref[:, h * ML_DQK:(h + 1) * ML_DQK]
        kh = k_ref[:, h * ML_DQK:(h + 1) * ML_DQK]
        vaug = jnp.concatenate([v_ref[:, h * HEAD_DV:(h + 1) * HEAD_DV], ones], axis=1)
        s = (_dot_nt(qh, kh) * scale) * jnp.exp(log_d - m_row)
        w_inter = jnp.exp(log_inter - m_row)
        c_aug = c_ref[h]
        r = _dot(s.astype(BF16), vaug) + (w_inter * scale) * _dot(qh, c_aug.astype(BF16))
        num = r[:, :HEAD_DV]
        den = r[:, HEAD_DV:]
        o_ref[:, h * HEAD_DV:(h + 1) * HEAD_DV] = num / jnp.maximum(jnp.abs(den), jnp.exp(-m_row))
        b_last = tot[:, N_HEADS + h:N_HEADS + h + 1]
        log_w = b_last - b_col + i_col
        m_new = jnp.maximum(b_last + m_prev, jnp.max(log_w, axis=0, keepdims=True))
        w_s = jnp.exp(log_w - m_new)
        decay = jnp.exp(b_last + m_prev - m_new)
        kw = (kh.astype(F32) * w_s).astype(BF16)
        c_ref[h] = decay * c_aug + _dot_tn(kw, vaug)
        m_ref[h] = jnp.broadcast_to(m_new, m_ref.shape[1:])


def _mlstm_call(ml_a, ml_g, lc):
    bsz, tt, _ = ml_a.shape
    C = CHUNK
    nch, nctx = tt // C, lc // C
    tri_cat, tri_t_cat, tri = _mlstm_consts(C)
    tri_cat = jnp.asarray(tri_cat, BF16)
    tri_t_cat = jnp.asarray(tri_t_cat, BF16)
    msk = jnp.asarray(tri, F32)
    cidx = functools.partial(_chunk_index, nctx=nctx, nch=nch)
    qk_w = N_HEADS * ML_DQK
    return pl.pallas_call(
        functools.partial(_mlstm_body, C=C),
        grid=(bsz, 2, nch),
        in_specs=[
            pl.BlockSpec((None, C, qk_w), lambda b, d, j: (b, cidx(d, j), 0)),
            pl.BlockSpec((None, C, qk_w), lambda b, d, j: (b, cidx(d, j), 1)),
            pl.BlockSpec((None, C, MIX_W), lambda b, d, j: (b, cidx(d, j), 1)),
            pl.BlockSpec((None, C, GATE_LANES), lambda b, d, j: (b, cidx(d, j), d)),
            pl.BlockSpec((None, C, 2 * C), lambda b, d, j: (d, 0, 0)),
            pl.BlockSpec((None, 2 * C, C), lambda b, d, j: (d, 0, 0)),
            pl.BlockSpec((None, C, C), lambda b, d, j: (d, 0, 0)),
        ],
        out_specs=pl.BlockSpec((None, None, C, MIX_W), lambda b, d, j: (b, d, cidx(d, j), 0)),
        out_shape=jax.ShapeDtypeStruct((bsz, 2, tt, MIX_W), F32),
        scratch_shapes=[pltpu.VMEM((N_HEADS, ML_DQK, 2 * HEAD_DV), F32),
                        pltpu.VMEM((N_HEADS, 8, 128), F32)],
        compiler_params=_cparams(("arbitrary", "arbitrary", "arbitrary")),
        name="mlstm",
    )(ml_a, ml_a, ml_a, ml_g, tri_cat, tri_t_cat, msk)


def _lru_gates_body(x_ref, xp_ref, xn_ref, cw_ref, cb_ref, wg_ref, bg_ref, lam_ref,
                    a0_ref, u0_ref, a1_ref, u1_ref, ext_ref, *, tc, nctx_t, nt):
    t = pl.program_id(1)
    prev_ok = jnp.logical_and(t != 0, t != nctx_t)
    next_ok = jnp.logical_and(t != nctx_t - 1, t != nt - 1)
    x = x_ref[...]
    ext_ref[0:8, :] = jnp.where(prev_ok, xp_ref[...], 0.0)
    ext_ref[8:8 + tc, :] = x
    ext_ref[8 + tc:16 + tc, :] = jnp.where(next_ok, xn_ref[...], 0.0)
    cw = cw_ref[...]
    xc = (cw[0:1] * ext_ref[6:6 + tc, :] + cw[1:2] * ext_ref[7:7 + tc, :] + cw[2:3] * x
          + cw[3:4] * ext_ref[9:9 + tc, :]) + cb_ref[...]
    gates = _dot(xc.astype(BF16), wg_ref[...]) + bg_ref[...]
    outs = ((a0_ref, u0_ref), (a1_ref, u1_ref))
    for d in range(2):
        r_pre = gates[:, (2 * d) * MIX_W:(2 * d + 1) * MIX_W]
        i_pre = gates[:, (2 * d + 1) * MIX_W:(2 * d + 2) * MIX_W]
        z = -lam_ref[d:d + 1, :]
        softplus = jnp.maximum(z, 0.0) + jnp.log(1.0 + jnp.exp(-jnp.abs(z)))
        log_a = (-LRU_C * softplus) * _sigmoid(r_pre)
        a_ref, u_ref = outs[d]
        a = jnp.exp(log_a)
        a_ref[...] = a
        one_m_a2 = -jnp.tanh(log_a) * (a * a + 1.0)
        u_ref[...] = jnp.sqrt(jnp.maximum(one_m_a2, 0.0)) * (_sigmoid(i_pre) * xc)


def _lru_gates_call(l, lru_x, conv_w, conv_b, wg, bg, lam, lc):
    bsz, tt, w = lru_x.shape
    tc = CHUNK
    nt, nctx_t = tt // tc, lc // tc
    h8 = tc // 8
    out = jax.ShapeDtypeStruct((tt, bsz * w), F32)
    o_spec = pl.BlockSpec((tc, w), lambda b, t: (t, b))
    return pl.pallas_call(
        functools.partial(_lru_gates_body, tc=tc, nctx_t=nctx_t, nt=nt),
        grid=(bsz, nt),
        in_specs=[
            pl.BlockSpec((None, tc, w), lambda b, t: (b, t, 0)),
            pl.BlockSpec((None, 8, w), lambda b, t: (b, jnp.maximum(t * h8 - 1, 0), 0)),
            pl.BlockSpec((None, 8, w), lambda b, t: (b, jnp.minimum((t + 1) * h8, tt // 8 - 1), 0)),
            pl.BlockSpec((None, 4, w), lambda b, t: (l, 0, 0)),
            pl.BlockSpec((None, 1, w), lambda b, t: (l, 0, 0)),
            pl.BlockSpec((None, w, 4 * w), lambda b, t: (l, 0, 0)),
            pl.BlockSpec((None, 1, 4 * w), lambda b, t: (l, 0, 0)),
            pl.BlockSpec((None, 2, w), lambda b, t: (l, 0, 0)),
        ],
        out_specs=[o_spec] * 4,
        out_shape=[out] * 4,
        scratch_shapes=[pltpu.VMEM((tc + 16, w), F32)],
        compiler_params=_cparams(("arbitrary", "arbitrary")),
        name="rglru_gates",
    )(lru_x, lru_x, lru_x, conv_w, conv_b, wg, bg, lam)


def _lru_scan_body(a0_ref, u0_ref, a1_ref, u1_ref, hf_ref, hb_ref, sf_ref, sb_ref, *, tc):
    @pl.when(pl.program_id(0) == 0)
    def _():
        sf_ref[...] = jnp.zeros_like(sf_ref)
        sb_ref[...] = jnp.zeros_like(sb_ref)

    def step(i, carry):
        hf, hb = carry
        hf = a0_ref[i] * hf + u0_ref[i]
        hf_ref[i] = hf
        ib = tc - 1 - i
        hb = a1_ref[ib] * hb + u1_ref[ib]
        hb_ref[ib] = hb
        return hf, hb

    hf, hb = lax.fori_loop(0, tc, step, (sf_ref[...], sb_ref[...]), unroll=8)
    sf_ref[...] = hf
    sb_ref[...] = hb


def _lru_scan_call(a0, u0, a1, u1, bsz, lc):
    tt = a0.shape[0]
    w = a0.shape[1] // bsz
    tc = CHUNK
    nch, nctx = tt // tc, lc // tc
    args = [a.reshape(tt, bsz, w) for a in (a0, u0, a1, u1)]
    fw = pl.BlockSpec((tc, bsz, w), lambda j: (j, 0, 0))
    bw = pl.BlockSpec((tc, bsz, w), lambda j: (_chunk_index(1, j, nctx, nch), 0, 0))
    out = jax.ShapeDtypeStruct((tt, bsz, w), F32)
    hf, hb = pl.pallas_call(
        functools.partial(_lru_scan_body, tc=tc),
        grid=(nch,),
        in_specs=[fw, fw, bw, bw],
        out_specs=[fw, bw],
        out_shape=[out, out],
        scratch_shapes=[pltpu.VMEM((bsz, w), F32), pltpu.VMEM((bsz, w), F32)],
        compiler_params=_cparams(("arbitrary",)),
        name="rglru_scan",
    )(*args)
    return hf.reshape(tt, bsz * w), hb.reshape(tt, bsz * w)


def _head_rms(o, w):
    parts = []
    for h in range(N_HEADS):
        oh = o[:, h * HEAD_DV:(h + 1) * HEAD_DV]
        parts.append(oh * lax.rsqrt(jnp.mean(oh * oh, axis=-1, keepdims=True) + EPS))
    return jnp.concatenate(parts, axis=1) * w


def _gelu_tanh(x):
    return 0.5 * x * (1.0 + jnp.tanh(0.7978845608028654 * (x + 0.044715 * (x * x * x))))


def _merge_body(hg0_ref, hg1_ref, ml0_ref, ml1_ref, lf_ref, lb_ref, hgg_ref, mlo_ref, ly_ref, mg_ref,
                ctx_ref, x_ref, gate_ref, hgn_ref, mln_ref, wb_ref, wo_ref, ctx_o_ref, x_o_ref,
                *, t0, d_model):
    t = pl.program_id(1) + t0
    hgg = hgg_ref[...].astype(F32)
    a_out = _head_rms(hg0_ref[...] + hg1_ref[...], hgn_ref[...]) * (hgg * _sigmoid(hgg))
    b_out = _head_rms(ml0_ref[...] + ml1_ref[...], mln_ref[...]) * _sigmoid(mlo_ref[...].astype(F32))
    c_out = (lf_ref[...] + lb_ref[...]) * _gelu_tanh(ly_ref[...].astype(F32))
    merged = None
    for n, br in enumerate((a_out, b_out, c_out)):
        gate = _sigmoid(mg_ref[:, n * d_model:(n + 1) * d_model].astype(F32))
        term = gate * _dot(br.astype(BF16), wb_ref[n])
        merged = term if merged is None else merged + term
    y = gate_ref[...] * _dot(merged.astype(BF16), wo_ref[...])

    if t0 == 0:
        @pl.when(t == 0)
        def _():
            ctx_o_ref[...] = ctx_ref[...] + y
    else:
        @pl.when(t == t0)
        def _():
            ctx_o_ref[...] = ctx_ref[...]

    @pl.when(t > 0)
    def _():
        x_o_ref[...] = x_ref[...] + y


def _merge_call(l, last, o_hg, o_ml, h_f, h_b, hg_a, ml_a, lru_y, mg, ctx, x, mods4,
                hg_norm, ml_norm, w_branch, w_out):
    bsz, lc, d_model = ctx.shape
    seq = x.shape[1]
    tm = lc
    t0 = 1 if last else 0
    nt = 1 + seq // tm - t0

    def tok(b, t):
        return t + t0

    def xi(t):
        return jnp.maximum(t + t0 - 1, 0)

    mix = lambda d: pl.BlockSpec((None, None, tm, MIX_W), lambda b, t: (b, d, tok(b, t), 0))
    lru = pl.BlockSpec((tm, MIX_W), lambda b, t: (tok(b, t), b))
    in_specs = [
        mix(0), mix(1), mix(0), mix(1), lru, lru,
        pl.BlockSpec((None, tm, MIX_W), lambda b, t: (b, tok(b, t), 2)),
        pl.BlockSpec((None, tm, MIX_W), lambda b, t: (b, tok(b, t), 2)),
        pl.BlockSpec((None, tm, MIX_W), lambda b, t: (b, tok(b, t), 0)),
        pl.BlockSpec((None, tm, 3 * d_model), lambda b, t: (b, tok(b, t), 0)),
        pl.BlockSpec((None, tm, d_model), lambda b, t: (b, 0, 0)),
        pl.BlockSpec((None, tm, d_model), lambda b, t: (b, xi(t), 0)),
        pl.BlockSpec((None, None, 1, d_model), lambda b, t: (l, jnp.where(t + t0 == 0, bsz, b), 0, 2)),
        pl.BlockSpec((None, 1, MIX_W), lambda b, t: (l, 0, 0)),
        pl.BlockSpec((None, 1, MIX_W), lambda b, t: (l, 0, 0)),
        pl.BlockSpec((None, 3, MIX_W, d_model), lambda b, t: (l, 0, 0, 0)),
        pl.BlockSpec((None, d_model, d_model), lambda b, t: (l, 0, 0)),
    ]
    out_specs = [
        pl.BlockSpec((None, tm, d_model), lambda b, t: (b, 0, 0)),
        pl.BlockSpec((None, tm, d_model), lambda b, t: (b, xi(t), 0)),
    ]
    out_shape = [jax.ShapeDtypeStruct(ctx.shape, F32), jax.ShapeDtypeStruct(x.shape, F32)]
    return pl.pallas_call(
        functools.partial(_merge_body, t0=t0, d_model=d_model),
        grid=(bsz, nt),
        in_specs=in_specs,
        out_specs=out_specs,
        out_shape=out_shape,
        compiler_params=_cparams(("arbitrary", "arbitrary")),
        name="merge",
    )(o_hg, o_hg, o_ml, o_ml, h_f, h_b, hg_a, ml_a, lru_y, mg, ctx, x, mods4,
      hg_norm, ml_norm, w_branch, w_out)


def _ffn_body(x_ref, sh_ref, sc_ref, gt_ref, ln_ref, wg_ref, wu_ref, wo_ref, fn_ref, o_ref,
              h_ref, acc_ref, *, n_k, n_inner, final_norm):
    k = pl.program_id(2)

    @pl.when(k == 0)
    def _():
        h_ref[...] = _norm_mod(x_ref[...], ln_ref[...], sh_ref[...], sc_ref[...]).astype(BF16)
        acc_ref[...] = jnp.zeros_like(acc_ref)

    h = h_ref[...]
    gate = _dot(h, wg_ref[...])
    up = _dot(h, wu_ref[...])
    act = (gate * _sigmoid(gate) * up).astype(BF16)
    acc_ref[...] += _dot(act, wo_ref[...])

    @pl.when(k == n_k - 1)
    def _():
        y = x_ref[...] + gt_ref[...] * acc_ref[...]
        if final_norm:
            y = y * lax.rsqrt(jnp.mean(y * y, axis=-1, keepdims=True) + EPS) * fn_ref[...]
        if n_inner is None:
            o_ref[...] = y
        else:
            d_model = y.shape[-1]
            for a in range(y.shape[0] // n_inner):
                o_ref[:, a * d_model:(a + 1) * d_model] = y[a * n_inner:(a + 1) * n_inner, :]


def _ffn_call(l, x, mods4, ctx_row, ln2, wg, wu, wo, fin, *, tm, n_inner, final_norm):
    bsz, n, d_model = x.shape
    n_k, _, tk = wg.shape[1:]
    nt = n // tm

    def mod_spec(chunk):
        return pl.BlockSpec((None, None, 1, d_model),
                            lambda b, t, k: (l, bsz if ctx_row else b, 0, chunk))

    if n_inner is None:
        o_spec = pl.BlockSpec((None, tm, d_model), lambda b, t, k: (b, t, 0))
        o_shape = jax.ShapeDtypeStruct(x.shape, F32)
    else:
        n_a = tm // n_inner
        o_spec = pl.BlockSpec((None, n_inner, n_a * d_model), lambda b, t, k: (b, 0, t))
        o_shape = jax.ShapeDtypeStruct((bsz, n_inner, (n // n_inner) * d_model), F32)
    out = pl.pallas_call(
        functools.partial(_ffn_body, n_k=n_k, n_inner=n_inner, final_norm=final_norm),
        grid=(bsz, nt, n_k),
        in_specs=[
            pl.BlockSpec((None, tm, d_model), lambda b, t, k: (b, t, 0)),
            mod_spec(3), mod_spec(4), mod_spec(5),
            pl.BlockSpec((None, 1, d_model), lambda b, t, k: (l, 0, 0)),
            pl.BlockSpec((None, None, d_model, tk), lambda b, t, k: (l, k, 0, 0)),
            pl.BlockSpec((None, None, d_model, tk), lambda b, t, k: (l, k, 0, 0)),
            pl.BlockSpec((None, None, tk, d_model), lambda b, t, k: (l, k, 0, 0)),
            pl.BlockSpec((1, d_model), lambda b, t, k: (0, 0)),
        ],
        out_specs=o_spec,
        out_shape=o_shape,
        scratch_shapes=[pltpu.VMEM((tm, d_model), BF16), pltpu.VMEM((tm, d_model), F32)],
        compiler_params=_cparams(("arbitrary", "arbitrary", "arbitrary")),
        name="ffn",
    )(x, mods4, mods4, mods4, ln2, wg, wu, wo, fin)
    return out.reshape(bsz, n, d_model)


def _prep_in_proj(w_in, b_in):
    depth, d_model, _ = w_in.shape
    hq = N_HEADS * HG_DK
    sizes = (hq, MIX_W, MIX_W, 2 * hq, N_HEADS * ML_DQK, N_HEADS * ML_DQK, MIX_W, MIX_W,
             2 * N_HEADS, 2 * N_HEADS, MIX_W, MIX_W, 3 * d_model)
    offs = np.concatenate([[0], np.cumsum(sizes)])
    assert offs[-1] == w_in.shape[-1]

    def cols(a, lo, hi):
        return a[..., int(offs[lo]):int(offs[hi])]

    def gate_block(a):
        ig = cols(a, 8, 9).reshape(a.shape[:-1] + (2, N_HEADS))
        fg = cols(a, 9, 10).reshape(a.shape[:-1] + (2, N_HEADS))
        pad = jnp.zeros(a.shape[:-1] + (2, GATE_LANES - 2 * N_HEADS), a.dtype)
        return jnp.concatenate([ig, fg, pad], axis=-1).reshape(a.shape[:-1] + (2 * GATE_LANES,))

    groups = [
        (lambda a: cols(a, 0, 3), BF16),
        (lambda a: cols(a, 3, 4), F32),
        (lambda a: cols(a, 4, 8), BF16),
        (gate_block, F32),
        (lambda a: cols(a, 10, 11), F32),
        (lambda a: cols(a, 11, 12), BF16),
        (lambda a: cols(a, 12, 13), BF16),
    ]
    b3 = b_in.reshape(depth, 1, -1)
    ws = [f(w_in).astype(BF16) for f, _ in groups]
    bs = [f(b3) for f, _ in groups]
    return ws, bs, [dt for _, dt in groups]


def _prep_lru_gates(lru_gate_w, lru_gate_b):
    depth = lru_gate_w.shape[0]
    eye = jnp.eye(LRU_BLOCKS, dtype=lru_gate_w.dtype)
    dense = jnp.einsum("lzgnde,nm->lndzgme", lru_gate_w, eye)
    dense = dense.reshape(depth, MIX_W, 4 * MIX_W).astype(BF16)
    return dense, lru_gate_b.reshape(depth, 1, 4 * MIX_W)


def _prep_ffn(w_ffn_in, w_ffn_out):
    depth, d_model, two_h = w_ffn_in.shape
    hidden = two_h // 2
    tk = hidden // 2 if (hidden // 2) % 128 == 0 else hidden
    n_k = hidden // tk
    wi = w_ffn_in.astype(BF16).reshape(depth, d_model, 2, n_k, tk)
    wg = jnp.transpose(wi[:, :, 0], (0, 2, 1, 3))
    wu = jnp.transpose(wi[:, :, 1], (0, 2, 1, 3))
    wo = w_ffn_out.astype(BF16).reshape(depth, n_k, tk, d_model)
    return wg, wu, wo


def kernel(x, c, ctx, c_ctx, w_ada, b_ada, ln1, w_in, b_in, hg_lb_raw, hg_norm, ml_norm, conv_w, conv_b,
           lru_gate_w, lru_gate_b, lru_lambda, w_branch, w_out, ln2, w_ffn_in, w_ffn_out, final_norm):
    bsz, seq, d_model = x.shape
    lc = ctx.shape[1]
    depth = w_ada.shape[0]
    rows = seq // GRID_W
    assert lc % CHUNK == 0 and seq % lc == 0 and seq % FFN_TM == 0 and bsz < 16

    c_all = jnp.zeros((16, d_model), F32).at[:bsz].set(c).at[bsz].set(c_ctx)
    mods4 = _ada_call(c_all, w_ada, b_ada).reshape(depth, 16, 1, 6 * d_model)
    lb4 = _lb_call(hg_lb_raw).reshape(depth, 2, 1, MIX_W)

    ws, bs, out_dtypes = _prep_in_proj(w_in, b_in)
    wgate, bgate = _prep_lru_gates(lru_gate_w, lru_gate_b)
    wg, wu, wo = _prep_ffn(w_ffn_in, w_ffn_out)
    w_branch_b = w_branch.astype(BF16)
    w_out_b = w_out.astype(BF16)
    ln1_3 = ln1.reshape(depth, 1, d_model)
    ln2_3 = ln2.reshape(depth, 1, d_model)
    hgn = hg_norm.reshape(depth, 1, MIX_W)
    mln = ml_norm.reshape(depth, 1, MIX_W)
    conv_b3 = conv_b.reshape(depth, 1, MIX_W)
    fin = final_norm.reshape(1, d_model)

    for l in range(depth):
        last = l == depth - 1
        n_inner = GRID_W if l % 2 == 0 else rows
        hg_a, hg_f, ml_a, ml_g, lru_x, lru_y, mg = _in_proj_call(l, ctx, x, mods4, ln1_3, ws, bs, out_dtypes)
        o_hg = _hgrn2_call(l, hg_a, hg_f, lb4, lc)
        o_ml = _mlstm_call(ml_a, ml_g, lc)
        a0, u0, a1, u1 = _lru_gates_call(l, lru_x, conv_w, conv_b3, wgate, bgate, lru_lambda, lc)
        h_f, h_b = _lru_scan_call(a0, u0, a1, u1, bsz, lc)
        ctx_m, x_m = _merge_call(l, last, o_hg, o_ml, h_f, h_b, hg_a, ml_a, lru_y, mg, ctx, x, mods4,
                                 hgn, mln, w_branch_b, w_out_b)
        x = _ffn_call(l, x_m, mods4, False, ln2_3, wg, wu, wo, fin,
                      tm=FFN_TM, n_inner=n_inner, final_norm=last)
        if not last:
            ctx = _ffn_call(l, ctx_m, mods4, True, ln2_3, wg, wu, wo, fin,
                            tm=lc, n_inner=None, final_norm=False)
    return x
```

```python
import functools

import numpy as np
import jax
import jax.numpy as jnp
from jax import lax
from jax.experimental import pallas as pl
from jax.experimental.pallas import tpu as pltpu

F32 = jnp.float32
BF16 = jnp.bfloat16

GRID_W = 64
MIX_W = 512
N_HEADS = 4
HEAD_DV = 128
HG_DK = 128
ML_DQK = 64
LRU_BLOCKS = 8
LRU_BD = 64
LRU_C = 8.0
EPS = 1e-6
NEG_BIG = -1e30
LB_TINY = 1e-30
GATE_LANES = 128
CHUNK = 128
FFN_TM = 512
VMEM_LIMIT = 56 * 1024 * 1024


def _cparams(sem):
    return pltpu.CompilerParams(dimension_semantics=sem, vmem_limit_bytes=VMEM_LIMIT)


def _sigmoid(x):
    return 1.0 / (1.0 + jnp.exp(-x))


def _log_sigmoid(x):
    return jnp.minimum(x, 0.0) - jnp.log(1.0 + jnp.exp(-jnp.abs(x)))


def _dot(a, b):
    return jnp.dot(a, b, preferred_element_type=F32)


def _dot_nt(a, b):
    return lax.dot_general(a, b, (((1,), (1,)), ((), ())), preferred_element_type=F32)


def _dot_tn(a, b):
    return lax.dot_general(a, b, (((0,), (0,)), ((), ())), preferred_element_type=F32)


def _split_hi_lo(x):
    hi = x.astype(BF16)
    lo = (x - hi.astype(F32)).astype(BF16)
    return hi, lo


def _norm_mod(x, ln, shift, scale):
    y = x * lax.rsqrt(jnp.mean(x * x, axis=-1, keepdims=True) + EPS) * ln
    return y * (1.0 + scale) + shift


def _mirror(m):
    return m[..., ::-1, ::-1].copy()


@functools.lru_cache(maxsize=None)
def _hgrn2_consts(C):
    n_lv = int(np.log2(C))
    mats, masks = [], []
    r = np.arange(C)
    for lv in range(n_lv):
        s = C >> (lv + 1)
        base = (r // (2 * s)) * (2 * s)
        mid = base + s - 1
        odd = r >= base + s
        m = np.zeros((C, C), np.float32)
        for t in range(C):
            if odd[t]:
                m[t, mid[t] + 1:t + 1] = 1.0
            else:
                m[t, t + 1:mid[t] + 1] = 1.0
        mats.append(m)
        same = base[:, None] == base[None, :]
        masks.append((same & odd[:, None] & (~odd)[None, :]).astype(np.float32))
    mats.append(np.tril(np.ones((C, C), np.float32)))
    mats.append(np.triu(np.ones((C, C), np.float32), 1))
    masks.append(np.eye(C, dtype=np.float32))
    mats = np.stack(mats)
    masks = np.stack(masks)
    mats = np.stack([mats, _mirror(mats)])
    masks = np.stack([masks, _mirror(masks)])
    mst = mats.reshape(2, (n_lv + 2) * C, C)
    mst = np.concatenate([mst, mst], axis=-1)
    return mst, masks, n_lv


@functools.lru_cache(maxsize=None)
def _mlstm_consts(C):
    tri = np.tril(np.ones((C, C), np.float32))
    tri = np.stack([tri, _mirror(tri)])
    tri_cat = np.concatenate([tri, tri], axis=-1)
    tri_t = np.transpose(tri, (0, 2, 1))
    tri_t_cat = np.concatenate([tri_t, tri_t], axis=1)
    return tri_cat, tri_t_cat, tri


def _chunk_index(d, j, nctx, nch):
    bw = jnp.where(j < nctx, nctx - 1 - j, nch - 1 + nctx - j)
    return jnp.where(d == 0, j, bw)


def _ada_body(c_ref, w_ref, b_ref, o_ref):
    cc = c_ref[...]
    s = cc * _sigmoid(cc)
    o_ref[...] = jnp.dot(s, w_ref[...], preferred_element_type=F32,
                         precision=lax.Precision.HIGHEST) + b_ref[...]


def _ada_call(c_all, w_ada, b_ada):
    depth, d_model, n6 = w_ada.shape
    rows = c_all.shape[0]
    tn = 1536
    return pl.pallas_call(
        _ada_body,
        grid=(depth, n6 // tn),
        in_specs=[
            pl.BlockSpec((rows, d_model), lambda l, n: (0, 0)),
            pl.BlockSpec((None, d_model, tn), lambda l, n: (l, 0, n)),
            pl.BlockSpec((None, 1, tn), lambda l, n: (l, 0, n)),
        ],
        out_specs=pl.BlockSpec((None, rows, tn), lambda l, n: (l, 0, n)),
        out_shape=jax.ShapeDtypeStruct((depth, rows, n6), F32),
        compiler_params=_cparams(("arbitrary", "arbitrary")),
        name="ada_mod",
    )(c_all, w_ada, b_ada.reshape(depth, 1, n6))


def _lb_body(raw_ref, o_ref):
    raw = raw_ref[...]
    depth = raw.shape[0]
    e = jnp.exp(raw - jnp.max(raw, axis=0, keepdims=True))
    p = e / jnp.sum(e, axis=0, keepdims=True)
    acc = jnp.zeros_like(p[0:1])
    for l in range(depth):
        acc = acc + p[l:l + 1]
        o_ref[l:l + 1, :] = acc - p[0:1]


def _lb_call(hg_lb_raw):
    depth = hg_lb_raw.shape[0]
    raw = hg_lb_raw.reshape(depth, -1)
    return pl.pallas_call(
        _lb_body,
        out_shape=jax.ShapeDtypeStruct(raw.shape, F32),
        name="hgrn2_lower_bounds",
    )(raw)


def _in_proj_body(ctx_ref, x_ref, sh_ref, sc_ref, ln_ref, *rest, n_out):
    w_refs = rest[:n_out]
    b_refs = rest[n_out:2 * n_out]
    o_refs = rest[2 * n_out:]
    t = pl.program_id(1)
    xin = jnp.where(t == 0, ctx_ref[...], x_ref[...])
    h = _norm_mod(xin, ln_ref[...], sh_ref[...], sc_ref[...]).astype(BF16)
    for w_ref, b_ref, o_ref in zip(w_refs, b_refs, o_refs):
        o_ref[...] = (_dot(h, w_ref[...]) + b_ref[...]).astype(o_ref.dtype)


def _in_proj_call(l, ctx, x, mods4, ln1, ws, bs, out_dtypes):
    bsz, lc, d_model = ctx.shape
    seq = x.shape[1]
    tm = lc
    nt = 1 + seq // tm
    n_out = len(ws)

    def mod_spec(chunk):
        return pl.BlockSpec((None, None, 1, d_model),
                            lambda b, t: (l, jnp.where(t == 0, bsz, b), 0, chunk))

    in_specs = [
        pl.BlockSpec((None, tm, d_model), lambda b, t: (b, 0, 0)),
        pl.BlockSpec((None, tm, d_model), lambda b, t: (b, jnp.maximum(t - 1, 0), 0)),
        mod_spec(0), mod_spec(1),
        pl.BlockSpec((None, 1, d_model), lambda b, t: (l, 0, 0)),
    ]
    in_specs += [pl.BlockSpec((None, d_model, w.shape[-1]), lambda b, t: (l, 0, 0)) for w in ws]
    in_specs += [pl.BlockSpec((None, 1, w.shape[-1]), lambda b, t: (l, 0, 0)) for w in ws]
    out_specs = [pl.BlockSpec((None, tm, w.shape[-1]), lambda b, t: (b, t, 0)) for w in ws]
    out_shape = [jax.ShapeDtypeStruct((bsz, lc + seq, w.shape[-1]), dt) for w, dt in zip(ws, out_dtypes)]
    return pl.pallas_call(
        functools.partial(_in_proj_body, n_out=n_out),
        grid=(bsz, nt),
        in_specs=in_specs,
        out_specs=out_specs,
        out_shape=out_shape,
        compiler_params=_cparams(("arbitrary", "arbitrary")),
        name="in_proj",
    )(ctx, x, mods4, mods4, ln1, *ws, *bs)


def _hgrn2_body(q_ref, v_ref, f_ref, lb_ref, mst_ref, msk_ref, o_ref, st_ref, *, C, n_lv):
    @pl.when(pl.program_id(2) == 0)
    def _():
        st_ref[...] = jnp.zeros_like(st_ref)

    q = q_ref[...].astype(F32)
    q = q * _sigmoid(q)
    v = v_ref[...]
    fp = f_ref[...]
    lb = lb_ref[...]
    e = jnp.exp(-jnp.abs(fp))
    inv = 1.0 / (1.0 + e)
    pos = fp >= 0.0
    sig = jnp.where(pos, inv, e * inv)
    nsig = jnp.where(pos, e * inv, inv)
    g = jnp.log(jnp.maximum(lb, LB_TINY) + (1.0 - lb) * sig)
    kk = (1.0 - lb) * nsig
    g_hi, g_lo = _split_hi_lo(g)
    ex = _dot(mst_ref[...], jnp.concatenate([g_hi, g_lo], axis=0))
    tot = jnp.sum(g, axis=0, keepdims=True)

    for h in range(N_HEADS):
        sl = slice(h * HG_DK, (h + 1) * HG_DK)
        qh = q[:, sl]
        kh = kk[:, sl]
        p = msk_ref[n_lv] * _dot_nt(qh.astype(BF16), kh.astype(BF16))
        for lv in range(n_lv):
            w = jnp.exp(ex[lv * C:(lv + 1) * C, sl])
            p = p + msk_ref[lv] * _dot_nt((qh * w).astype(BF16), (kh * w).astype(BF16))
        st = st_ref[h]
        qb = (qh * jnp.exp(ex[n_lv * C:(n_lv + 1) * C, sl])).astype(BF16)
        vh = v[:, sl]
        o_ref[:, sl] = (_dot(p.astype(BF16), vh) + _dot_nt(qb, st.astype(BF16))).astype(o_ref.dtype)
        kb = (kh * jnp.exp(ex[(n_lv + 1) * C:(n_lv + 2) * C, sl])).astype(BF16)
        st_ref[h] = st * jnp.exp(tot[:, sl]) + _dot_tn(vh, kb)


def _mlstm_body(q_ref, k_ref, v_ref, g_ref, tri_ref, trit_ref, msk_ref, o_ref, c_ref, m_ref, *, C):
    @pl.when(pl.program_id(2) == 0)
    def _():
        c_ref[...] = jnp.zeros_like(c_ref)
        m_ref[...] = jnp.zeros_like(m_ref)

    gates = g_ref[...]
    gates_t = gates.T
    lf = _log_sigmoid(gates)
    lf_t = _log_sigmoid(gates_t)
    lf_hi, lf_lo = _split_hi_lo(lf)
    b_cols = _dot(tri_ref[...], jnp.concatenate([lf_hi, lf_lo], axis=0))
    lft_hi, lft_lo = _split_hi_lo(lf_t)
    b_rows = _dot(jnp.concatenate([lft_hi, lft_lo], axis=1), trit_ref[...])
    tot = jnp.sum(lf, axis=0, keepdims=True)
    allowed = msk_ref[...] > 0.0
    ones = jnp.ones((C, HEAD_DV), BF16)
    scale = ML_DQK ** -0.5

    for h in range(N_HEADS):
        b_col = b_cols[:, N_HEADS + h:N_HEADS + h + 1]
        b_row = b_rows[N_HEADS + h:N_HEADS + h + 1, :]
        i_row = gates_t[h:h + 1, :]
        i_col = gates[:, h:h + 1]
        m_prev = m_ref[h][0:1, 0:1]
        log_d = jnp.where(allowed, b_col - b_row + i_row, NEG_BIG)
        log_inter = b_col + m_prev
        m_row = jnp.maximum(jnp.max(log_d, axis=-1, keepdims=True), log_inter)
        qh = q_ref[:, h * ML_DQK:(h + 1) * ML_DQK]
        kh = k_ref[:, h * ML_DQK:(h + 1) * ML_DQK]
        vaug = jnp.concatenate([v_ref[:, h * HEAD_DV:(h + 1) * HEAD_DV], ones], axis=1)
        s = (_dot_nt(qh, kh) * scale) * jnp.exp(log_d - m_row)
        w_inter = jnp.exp(log_inter - m_row)
        c_aug = c_ref[h]
        r = _dot(s.astype(BF16), vaug) + (w_inter * scale) * _dot(qh, c_aug.astype(BF16))
        num = r[:, :HEAD_DV]
        den = r[:, HEAD_DV:]
        hout = num / jnp.maximum(jnp.abs(den), jnp.exp(-m_row))
        o_ref[:, h * HEAD_DV:(h + 1) * HEAD_DV] = hout.astype(o_ref.dtype)
        b_last = tot[:, N_HEADS + h:N_HEADS + h + 1]
        log_w = b_last - b_col + i_col
        m_new = jnp.maximum(b_last + m_prev, jnp.max(log_w, axis=0, keepdims=True))
        w_s = jnp.exp(log_w - m_new)
        decay = jnp.exp(b_last + m_prev - m_new)
        kw = (kh.astype(F32) * w_s).astype(BF16)
        c_ref[h] = decay * c_aug + _dot_tn(kw, vaug)
        m_ref[h] = jnp.broadcast_to(m_new, m_ref.shape[1:])


def _mixers_body(hq_ref, hv_ref, hf_ref, lb_ref, mst_ref, hmsk_ref, mq_ref, mk_ref, mv_ref, mg_ref,
                 tri_ref, trit_ref, mmsk_ref, o_hg_ref, o_ml_ref, st_ref, c_ref, m_ref, *, C, n_lv):
    _hgrn2_body(hq_ref, hv_ref, hf_ref, lb_ref, mst_ref, hmsk_ref, o_hg_ref, st_ref, C=C, n_lv=n_lv)
    _mlstm_body(mq_ref, mk_ref, mv_ref, mg_ref, tri_ref, trit_ref, mmsk_ref, o_ml_ref, c_ref, m_ref, C=C)


def _mixers_call(l, hg_a, hg_f, lb4, ml_a, ml_g, lc):
    bsz, tt, _ = hg_a.shape
    C = CHUNK
    nch, nctx = tt // C, lc // C
    mst, hmsk, n_lv = _hgrn2_consts(C)
    mst = jnp.asarray(mst, BF16)
    hmsk = jnp.asarray(hmsk, F32)
    tri_cat, tri_t_cat, tri = _mlstm_consts(C)
    tri_cat = jnp.asarray(tri_cat, BF16)
    tri_t_cat = jnp.asarray(tri_t_cat, BF16)
    mmsk = jnp.asarray(tri, F32)
    cidx = functools.partial(_chunk_index, nctx=nctx, nch=nch)
    qk_w = N_HEADS * ML_DQK
    tok = lambda col: (lambda b, d, j: (b, cidx(d, j), col))
    tok_d = lambda b, d, j: (b, cidx(d, j), d)
    per_dir3 = lambda b, d, j: (d, 0, 0)
    o_spec = pl.BlockSpec((None, None, C, MIX_W), lambda b, d, j: (b, d, cidx(d, j), 0))
    o_shape = jax.ShapeDtypeStruct((bsz, 2, tt, MIX_W), BF16)
    return pl.pallas_call(
        functools.partial(_mixers_body, C=C, n_lv=n_lv),
        grid=(bsz, 2, nch),
        in_specs=[
            pl.BlockSpec((None, C, MIX_W), tok(0)),
            pl.BlockSpec((None, C, MIX_W), tok(1)),
            pl.BlockSpec((None, C, MIX_W), tok_d),
            pl.BlockSpec((None, None, 1, MIX_W), lambda b, d, j: (l, d, 0, 0)),
            pl.BlockSpec((None,) + mst.shape[1:], per_dir3),
            pl.BlockSpec((None,) + hmsk.shape[1:], lambda b, d, j: (d, 0, 0, 0)),
            pl.BlockSpec((None, C, qk_w), tok(0)),
            pl.BlockSpec((None, C, qk_w), tok(1)),
            pl.BlockSpec((None, C, MIX_W), tok(1)),
            pl.BlockSpec((None, C, GATE_LANES), tok_d),
            pl.BlockSpec((None, C, 2 * C), per_dir3),
            pl.BlockSpec((None, 2 * C, C), per_dir3),
            pl.BlockSpec((None, C, C), per_dir3),
        ],
        out_specs=[o_spec, o_spec],
        out_shape=[o_shape, o_shape],
        scratch_shapes=[pltpu.VMEM((N_HEADS, HEAD_DV, HG_DK), F32),
                        pltpu.VMEM((N_HEADS, ML_DQK, 2 * HEAD_DV), F32),
                        pltpu.VMEM((N_HEADS, 8, 128), F32)],
        compiler_params=_cparams(("arbitrary", "arbitrary", "arbitrary")),
        name="mixers",
    )(hg_a, hg_a, hg_f, lb4, mst, hmsk, ml_a, ml_a, ml_a, ml_g, tri_cat, tri_t_cat, mmsk)


def _lru_gates_body(x_ref, xp_ref, xn_ref, cw_ref, cb_ref, wg_ref, bg_ref, lam_ref,
                    a0_ref, u0_ref, a1_ref, u1_ref, ext_ref, *, tc, nctx_t, nt):
    t = pl.program_id(1)
    prev_ok = jnp.logical_and(t != 0, t != nctx_t)
    next_ok = jnp.logical_and(t != nctx_t - 1, t != nt - 1)
    x = x_ref[...]
    ext_ref[0:8, :] = jnp.where(prev_ok, xp_ref[...], 0.0)
    ext_ref[8:8 + tc, :] = x
    ext_ref[8 + tc:16 + tc, :] = jnp.where(next_ok, xn_ref[...], 0.0)
    cw = cw_ref[...]
    xc = (cw[0:1] * ext_ref[6:6 + tc, :] + cw[1:2] * ext_ref[7:7 + tc, :] + cw[2:3] * x
          + cw[3:4] * ext_ref[9:9 + tc, :]) + cb_ref[...]
    gates = _dot(xc.astype(BF16), wg_ref[...]) + bg_ref[...]
    outs = ((a0_ref, u0_ref), (a1_ref, u1_ref))
    for d in range(2):
        r_pre = gates[:, (2 * d) * MIX_W:(2 * d + 1) * MIX_W]
        i_pre = gates[:, (2 * d + 1) * MIX_W:(2 * d + 2) * MIX_W]
        z = -lam_ref[d:d + 1, :]
        softplus = jnp.maximum(z, 0.0) + jnp.log(1.0 + jnp.exp(-jnp.abs(z)))
        log_a = (-LRU_C * softplus) * _sigmoid(r_pre)
        a_ref, u_ref = outs[d]
        a = jnp.exp(log_a)
        a_ref[...] = a
        one_m_a2 = -jnp.tanh(log_a) * (a * a + 1.0)
        u_ref[...] = jnp.sqrt(jnp.maximum(one_m_a2, 0.0)) * (_sigmoid(i_pre) * xc)


def _lru_gates_call(l, lru_x, conv_w, conv_b, wg, bg, lam, lc):
    bsz, tt, w = lru_x.shape
    tc = lc
    nt, nctx_t = tt // tc, lc // tc
    h8 = tc // 8
    out = jax.ShapeDtypeStruct((bsz, tt, w), F32)
    o_spec = pl.BlockSpec((None, tc, w), lambda b, t: (b, t, 0))
    return pl.pallas_call(
        functools.partial(_lru_gates_body, tc=tc, nctx_t=nctx_t, nt=nt),
        grid=(bsz, nt),
        in_specs=[
            pl.BlockSpec((None, tc, w), lambda b, t: (b, t, 0)),
            pl.BlockSpec((None, 8, w), lambda b, t: (b, jnp.maximum(t * h8 - 1, 0), 0)),
            pl.BlockSpec((None, 8, w), lambda b, t: (b, jnp.minimum((t + 1) * h8, tt // 8 - 1), 0)),
            pl.BlockSpec((None, 4, w), lambda b, t: (l, 0, 0)),
            pl.BlockSpec((None, 1, w), lambda b, t: (l, 0, 0)),
            pl.BlockSpec((None, w, 4 * w), lambda b, t: (l, 0, 0)),
            pl.BlockSpec((None, 1, 4 * w), lambda b, t: (l, 0, 0)),
            pl.BlockSpec((None, 2, w), lambda b, t: (l, 0, 0)),
        ],
        out_specs=[o_spec] * 4,
        out_shape=[out] * 4,
        scratch_shapes=[pltpu.VMEM((tc + 16, w), F32)],
        compiler_params=_cparams(("arbitrary", "arbitrary")),
        name="rglru_gates",
    )(lru_x, lru_x, lru_x, conv_w, conv_b, wg, bg, lam)


def _lru_scan_body(a0_ref, u0_ref, a1_ref, u1_ref, hf_ref, hb_ref, sf_ref, sb_ref, *, tc):
    @pl.when(pl.program_id(0) == 0)
    def _():
        sf_ref[...] = jnp.zeros_like(sf_ref)
        sb_ref[...] = jnp.zeros_like(sb_ref)

    def step(i, carry):
        hf, hb = carry
        hf = a0_ref[:, i, :] * hf + u0_ref[:, i, :]
        hf_ref[:, i, :] = hf
        ib = tc - 1 - i
        hb = a1_ref[:, ib, :] * hb + u1_ref[:, ib, :]
        hb_ref[:, ib, :] = hb
        return hf, hb

    hf, hb = lax.fori_loop(0, tc, step, (sf_ref[...], sb_ref[...]), unroll=8)
    sf_ref[...] = hf
    sb_ref[...] = hb


def _lru_scan_call(a0, u0, a1, u1, lc):
    bsz, tt, w = a0.shape
    tc = CHUNK
    nch, nctx = tt // tc, lc // tc
    fw = pl.BlockSpec((bsz, tc, w), lambda j: (0, j, 0))
    bw = pl.BlockSpec((bsz, tc, w), lambda j: (0, _chunk_index(1, j, nctx, nch), 0))
    out = jax.ShapeDtypeStruct((bsz, tt, w), F32)
    return pl.pallas_call(
        functools.partial(_lru_scan_body, tc=tc),
        grid=(nch,),
        in_specs=[fw, fw, bw, bw],
        out_specs=[fw, bw],
        out_shape=[out, out],
        scratch_shapes=[pltpu.VMEM((bsz, w), F32), pltpu.VMEM((bsz, w), F32)],
        compiler_params=_cparams(("arbitrary",)),
        name="rglru_scan",
    )(a0, u0, a1, u1)


def _head_rms(o, w):
    parts = []
    for h in range(N_HEADS):
        oh = o[:, h * HEAD_DV:(h + 1) * HEAD_DV]
        parts.append(oh * lax.rsqrt(jnp.mean(oh * oh, axis=-1, keepdims=True) + EPS))
    return jnp.concatenate(parts, axis=1) * w


def _gelu_tanh(x):
    return 0.5 * x * (1.0 + jnp.tanh(0.7978845608028654 * (x + 0.044715 * (x * x * x))))


def _merge_body(hg0_ref, hg1_ref, ml0_ref, ml1_ref, lf_ref, lb_ref, hgg_ref, mlo_ref, ly_ref, mg_ref,
                ctx_ref, x_ref, gate_ref, hgn_ref, mln_ref, wb_ref, wo_ref, ctx_o_ref, x_o_ref,
                *, t0, d_model):
    t = pl.program_id(1) + t0
    hgg = hgg_ref[...].astype(F32)
    o_hg = hg0_ref[...].astype(F32) + hg1_ref[...].astype(F32)
    o_ml = ml0_ref[...].astype(F32) + ml1_ref[...].astype(F32)
    a_out = _head_rms(o_hg, hgn_ref[...]) * (hgg * _sigmoid(hgg))
    b_out = _head_rms(o_ml, mln_ref[...]) * _sigmoid(mlo_ref[...].astype(F32))
    c_out = (lf_ref[...] + lb_ref[...]) * _gelu_tanh(ly_ref[...].astype(F32))
    merged = None
    for n, br in enumerate((a_out, b_out, c_out)):
        gate = _sigmoid(mg_ref[:, n * d_model:(n + 1) * d_model].astype(F32))
        term = gate * _dot(br.astype(BF16), wb_ref[n])
        merged = term if merged is None else merged + term
    y = gate_ref[...] * _dot(merged.astype(BF16), wo_ref[...])

    if t0 == 0:
        @pl.when(t == 0)
        def _():
            ctx_o_ref[...] = ctx_ref[...] + y
    else:
        @pl.when(t == t0)
        def _():
            ctx_o_ref[...] = ctx_ref[...]

    @pl.when(t > 0)
    def _():
        x_o_ref[...] = x_ref[...] + y


def _merge_call(l, last, o_hg, o_ml, h_f, h_b, hg_a, ml_a, lru_y, mg, ctx, x, mods4,
                hg_norm, ml_norm, w_branch, w_out):
    bsz, lc, d_model = ctx.shape
    seq = x.shape[1]
    tm = lc
    t0 = 1 if last else 0
    nt = 1 + seq // tm - t0

    def tok(b, t):
        return t + t0

    def xi(t):
        return jnp.maximum(t + t0 - 1, 0)

    mix = lambda d: pl.BlockSpec((None, None, tm, MIX_W), lambda b, t: (b, d, tok(b, t), 0))
    lru = pl.BlockSpec((None, tm, MIX_W), lambda b, t: (b, tok(b, t), 0))
    in_specs = [
        mix(0), mix(1), mix(0), mix(1), lru, lru,
        pl.BlockSpec((None, tm, MIX_W), lambda b, t: (b, tok(b, t), 2)),
        pl.BlockSpec((None, tm, MIX_W), lambda b, t: (b, tok(b, t), 2)),
        pl.BlockSpec((None, tm, MIX_W), lambda b, t: (b, tok(b, t), 0)),
        pl.BlockSpec((None, tm, 3 * d_model), lambda b, t: (b, tok(b, t), 0)),
        pl.BlockSpec((None, tm, d_model), lambda b, t: (b, 0, 0)),
        pl.BlockSpec((None, tm, d_model), lambda b, t: (b, xi(t), 0)),
        pl.BlockSpec((None, None, 1, d_model), lambda b, t: (l, jnp.where(t + t0 == 0, bsz, b), 0, 2)),
        pl.BlockSpec((None, 1, MIX_W), lambda b, t: (l, 0, 0)),
        pl.BlockSpec((None, 1, MIX_W), lambda b, t: (l, 0, 0)),
        pl.BlockSpec((None, 3, MIX_W, d_model), lambda b, t: (l, 0, 0, 0)),
        pl.BlockSpec((None, d_model, d_model), lambda b, t: (l, 0, 0)),
    ]
    out_specs = [
        pl.BlockSpec((None, tm, d_model), lambda b, t: (b, 0, 0)),
        pl.BlockSpec((None, tm, d_model), lambda b, t: (b, xi(t), 0)),
    ]
    out_shape = [jax.ShapeDtypeStruct(ctx.shape, F32), jax.ShapeDtypeStruct(x.shape, F32)]
    return pl.pallas_call(
        functools.partial(_merge_body, t0=t0, d_model=d_model),
        grid=(bsz, nt),
        in_specs=in_specs,
        out_specs=out_specs,
        out_shape=out_shape,
        compiler_params=_cparams(("arbitrary", "arbitrary")),
        name="merge",
    )(o_hg, o_hg, o_ml, o_ml, h_f, h_b, hg_a, ml_a, lru_y, mg, ctx, x, mods4,
      hg_norm, ml_norm, w_branch, w_out)


def _ffn_body(x_ref, sh_ref, sc_ref, gt_ref, ln_ref, wg_ref, wu_ref, wo_ref, fn_ref, o_ref,
              h_ref, acc_ref, *, n_k, n_inner, final_norm):
    k = pl.program_id(2)

    @pl.when(k == 0)
    def _():
        h_ref[...] = _norm_mod(x_ref[...], ln_ref[...], sh_ref[...], sc_ref[...]).astype(BF16)
        acc_ref[...] = jnp.zeros_like(acc_ref)

    h = h_ref[...]
    gate = _dot(h, wg_ref[...])
    up = _dot(h, wu_ref[...])
    act = (gate * _sigmoid(gate) * up).astype(BF16)
    acc_ref[...] += _dot(act, wo_ref[...])

    @pl.when(k == n_k - 1)
    def _():
        y = x_ref[...] + gt_ref[...] * acc_ref[...]
        if final_norm:
            y = y * lax.rsqrt(jnp.mean(y * y, axis=-1, keepdims=True) + EPS) * fn_ref[...]
        if n_inner is None:
            o_ref[...] = y
        else:
            for a in range(y.shape[0] // n_inner):
                o_ref[:, a, :] = y[a * n_inner:(a + 1) * n_inner, :]


def _ffn_call(l, x, mods4, ctx_row, ln2, wg, wu, wo, fin, *, tm, n_inner, final_norm):
    bsz, n, d_model = x.shape
    n_k, _, tk = wg.shape[1:]
    nt = n // tm

    def mod_spec(chunk):
        return pl.BlockSpec((None, None, 1, d_model),
                            lambda b, t, k: (l, bsz if ctx_row else b, 0, chunk))

    if n_inner is None:
        o_spec = pl.BlockSpec((None, tm, d_model), lambda b, t, k: (b, t, 0))
        o_shape = jax.ShapeDtypeStruct(x.shape, F32)
    else:
        n_a = tm // n_inner
        o_spec = pl.BlockSpec((None, n_inner, n_a, d_model), lambda b, t, k: (b, 0, t, 0))
        o_shape = jax.ShapeDtypeStruct((bsz, n_inner, n // n_inner, d_model), F32)
    out = pl.pallas_call(
        functools.partial(_ffn_body, n_k=n_k, n_inner=n_inner, final_norm=final_norm),
        grid=(bsz, nt, n_k),
        in_specs=[
            pl.BlockSpec((None, tm, d_model), lambda b, t, k: (b, t, 0)),
            mod_spec(3), mod_spec(4), mod_spec(5),
            pl.BlockSpec((None, 1, d_model), lambda b, t, k: (l, 0, 0)),
            pl.BlockSpec((None, None, d_model, tk), lambda b, t, k: (l, k, 0, 0)),
            pl.BlockSpec((None, None, d_model, tk), lambda b, t, k: (l, k, 0, 0)),
            pl.BlockSpec((None, None, tk, d_model), lambda b, t, k: (l, k, 0, 0)),
            pl.BlockSpec((1, d_model), lambda b, t, k: (0, 0)),
        ],
        out_specs=o_spec,
        out_shape=o_shape,
        scratch_shapes=[pltpu.VMEM((tm, d_model), BF16), pltpu.VMEM((tm, d_model), F32)],
        compiler_params=_cparams(("arbitrary", "arbitrary", "arbitrary")),
        name="ffn",
    )(x, mods4, mods4, mods4, ln2, wg, wu, wo, fin)
    return out.reshape(bsz, n, d_model)


def _prep_in_proj(w_in, b_in):
    depth, d_model, _ = w_in.shape
    hq = N_HEADS * HG_DK
    sizes = (hq, MIX_W, MIX_W, 2 * hq, N_HEADS * ML_DQK, N_HEADS * ML_DQK, MIX_W, MIX_W,
             2 * N_HEADS, 2 * N_HEADS, MIX_W, MIX_W, 3 * d_model)
    offs = np.concatenate([[0], np.cumsum(sizes)])
    assert offs[-1] == w_in.shape[-1]

    def cols(a, lo, hi):
        return a[..., int(offs[lo]):int(offs[hi])]

    def gate_block(a):
        ig = cols(a, 8, 9).reshape(a.shape[:-1] + (2, N_HEADS))
        fg = cols(a, 9, 10).reshape(a.shape[:-1] + (2, N_HEADS))
        pad = jnp.zeros(a.shape[:-1] + (2, GATE_LANES - 2 * N_HEADS), a.dtype)
        return jnp.concatenate([ig, fg, pad], axis=-1).reshape(a.shape[:-1] + (2 * GATE_LANES,))

    groups = [
        (lambda a: cols(a, 0, 3), BF16),
        (lambda a: cols(a, 3, 4), F32),
        (lambda a: cols(a, 4, 8), BF16),
        (gate_block, F32),
        (lambda a: cols(a, 10, 11), F32),
        (lambda a: cols(a, 11, 12), BF16),
        (lambda a: cols(a, 12, 13), BF16),
    ]
    b3 = b_in.reshape(depth, 1, -1)
    ws = [f(w_in).astype(BF16) for f, _ in groups]
    bs = [f(b3) for f, _ in groups]
    return ws, bs, [dt for _, dt in groups]


def _prep_lru_gates(lru_gate_w, lru_gate_b):
    depth = lru_gate_w.shape[0]
    eye = jnp.eye(LRU_BLOCKS, dtype=lru_gate_w.dtype)
    dense = jnp.einsum("lzgnde,nm->lndzgme", lru_gate_w, eye)
    dense = dense.reshape(depth, MIX_W, 4 * MIX_W).astype(BF16)
    return dense, lru_gate_b.reshape(depth, 1, 4 * MIX_W)


def _prep_ffn(w_ffn_in, w_ffn_out):
    depth, d_model, two_h = w_ffn_in.shape
    hidden = two_h // 2
    tk = hidden // 2 if (hidden // 2) % 128 == 0 else hidden
    n_k = hidden // tk
    wi = w_ffn_in.astype(BF16).reshape(depth, d_model, 2, n_k, tk)
    wg = jnp.transpose(wi[:, :, 0], (0, 2, 1, 3))
    wu = jnp.transpose(wi[:, :, 1], (0, 2, 1, 3))
    wo = w_ffn_out.astype(BF16).reshape(depth, n_k, tk, d_model)
    return wg, wu, wo


def kernel(x, c, ctx, c_ctx, w_ada, b_ada, ln1, w_in, b_in, hg_lb_raw, hg_norm, ml_norm, conv_w, conv_b,
           lru_gate_w, lru_gate_b, lru_lambda, w_branch, w_out, ln2, w_ffn_in, w_ffn_out, final_norm):
    bsz, seq, d_model = x.shape
    lc = ctx.shape[1]
    depth = w_ada.shape[0]
    rows = seq // GRID_W
    assert lc % CHUNK == 0 and seq % lc == 0 and seq % FFN_TM == 0 and bsz < 16

    c_all = jnp.zeros((16, d_model), F32).at[:bsz].set(c).at[bsz].set(c_ctx)
    mods4 = _ada_call(c_all, w_ada, b_ada).reshape(depth, 16, 1, 6 * d_model)
    lb4 = _lb_call(hg_lb_raw).reshape(depth, 2, 1, MIX_W)

    ws, bs, out_dtypes = _prep_in_proj(w_in, b_in)
    wgate, bgate = _prep_lru_gates(lru_gate_w, lru_gate_b)
    wg, wu, wo = _prep_ffn(w_ffn_in, w_ffn_out)
    w_branch_b = w_branch.astype(BF16)
    w_out_b = w_out.astype(BF16)
    ln1_3 = ln1.reshape(depth, 1, d_model)
    ln2_3 = ln2.reshape(depth, 1, d_model)
    hgn = hg_norm.reshape(depth, 1, MIX_W)
    mln = ml_norm.reshape(depth, 1, MIX_W)
    conv_b3 = conv_b.reshape(depth, 1, MIX_W)
    fin = final_norm.reshape(1, d_model)

    for l in range(depth):
        last = l == depth - 1
        n_inner = GRID_W if l % 2 == 0 else rows
        hg_a, hg_f, ml_a, ml_g, lru_x, lru_y, mg = _in_proj_call(l, ctx, x, mods4, ln1_3, ws, bs, out_dtypes)
        o_hg, o_ml = _mixers_call(l, hg_a, hg_f, lb4, ml_a, ml_g, lc)
        a0, u0, a1, u1 = _lru_gates_call(l, lru_x, conv_w, conv_b3, wgate, bgate, lru_lambda, lc)
        h_f, h_b = _lru_scan_call(a0, u0, a1, u1, lc)
        ctx_m, x_m = _merge_call(l, last, o_hg, o_ml, h_f, h_b, hg_a, ml_a, lru_y, mg, ctx, x, mods4,
                                 hgn, mln, w_branch_b, w_out_b)
        x = _ffn_call(l, x_m, mods4, False, ln2_3, wg, wu, wo, fin,
                      tm=FFN_TM, n_inner=n_inner, final_norm=last)
        if not last:
            ctx = _ffn_call(l, ctx_m, mods4, True, ln2_3, wg, wu, wo, fin,
                            tm=lc, n_inner=None, final_norm=False)
    return x
```

```python
import functools
import itertools

import numpy as np
import jax
import jax.numpy as jnp
from jax import lax
from jax.experimental import pallas as pl
from jax.experimental.pallas import tpu as pltpu

F32 = jnp.float32
BF16 = jnp.bfloat16

GRID_W = 64
MIX_W = 512
N_HEADS = 4
HEAD_DV = 128
HG_DK = 128
ML_DQK = 64
LRU_BLOCKS = 8
LRU_BD = 64
LRU_C = 8.0
EPS = 1e-6
NEG_BIG = -1e30
LB_TINY = 1e-30
LOG2_E = 1.4426950408889634
GATE_LANES = 128
CHUNK = 128
FFN_TM = 512
VMEM_LIMIT = 56 * 1024 * 1024


def _cparams(sem):
    return pltpu.CompilerParams(dimension_semantics=sem, vmem_limit_bytes=VMEM_LIMIT)


def _sigmoid(x):
    return 1.0 / (1.0 + jnp.exp(-x))


def _log_sigmoid(x):
    return jnp.minimum(x, 0.0) - jnp.log(1.0 + jnp.exp(-jnp.abs(x)))


def _dot(a, b):
    return jnp.dot(a, b, preferred_element_type=F32)


def _dot_nt(a, b):
    return lax.dot_general(a, b, (((1,), (1,)), ((), ())), preferred_element_type=F32)


def _dot_tn(a, b):
    return lax.dot_general(a, b, (((0,), (0,)), ((), ())), preferred_element_type=F32)


def _split_hi_lo(x):
    hi = x.astype(BF16)
    lo = (x - hi.astype(F32)).astype(BF16)
    return hi, lo


def _norm_mod(x, ln, shift, scale):
    y = x * lax.rsqrt(jnp.mean(x * x, axis=-1, keepdims=True) + EPS) * ln
    return y * (1.0 + scale) + shift


def _mirror(m):
    return m[..., ::-1, ::-1].copy()


@functools.lru_cache(maxsize=None)
def _hgrn2_consts(C):
    n_lv = int(np.log2(C))
    mats, masks = [], []
    r = np.arange(C)
    for lv in range(n_lv):
        s = C >> (lv + 1)
        base = (r // (2 * s)) * (2 * s)
        mid = base + s - 1
        odd = r >= base + s
        m = np.zeros((C, C), np.float32)
        for t in range(C):
            if odd[t]:
                m[t, mid[t] + 1:t + 1] = 1.0
            else:
                m[t, t + 1:mid[t] + 1] = 1.0
        mats.append(m)
        same = base[:, None] == base[None, :]
        masks.append((same & odd[:, None] & (~odd)[None, :]).astype(np.float32))
    mats.append(np.tril(np.ones((C, C), np.float32)))
    masks.append(np.eye(C, dtype=np.float32))
    mats = np.stack(mats)
    masks = np.stack(masks)
    mats = np.stack([mats, _mirror(mats)])
    masks = np.stack([masks, _mirror(masks)])
    mst = mats.reshape(2, (n_lv + 1) * C, C)
    mst = np.concatenate([mst, mst], axis=-1)
    return mst, masks, n_lv


@functools.lru_cache(maxsize=None)
def _mlstm_consts(C):
    tri = np.tril(np.ones((C, C), np.float32))
    tri = np.stack([tri, _mirror(tri)])
    tri_cat = np.concatenate([tri, tri], axis=-1)
    tri_t = np.transpose(tri, (0, 2, 1))
    tri_t_cat = np.concatenate([tri_t, tri_t], axis=1)
    return tri_cat, tri_t_cat, tri


def _chunk_index(d, j, nctx, nch):
    bw = jnp.where(j < nctx, nctx - 1 - j, nch - 1 + nctx - j)
    return jnp.where(d == 0, j, bw)


def _ada_body(c_ref, w_ref, b_ref, o_ref):
    cc = c_ref[...]
    s = cc * _sigmoid(cc)
    o_ref[...] = jnp.dot(s, w_ref[...], preferred_element_type=F32,
                         precision=lax.Precision.HIGHEST) + b_ref[...]


def _ada_call(c_all, w_ada, b_ada):
    depth, d_model, n6 = w_ada.shape
    rows = c_all.shape[0]
    tn = 1536
    return pl.pallas_call(
        _ada_body,
        grid=(depth, n6 // tn),
        in_specs=[
            pl.BlockSpec((rows, d_model), lambda l, n: (0, 0)),
            pl.BlockSpec((None, d_model, tn), lambda l, n: (l, 0, n)),
            pl.BlockSpec((None, 1, tn), lambda l, n: (l, 0, n)),
        ],
        out_specs=pl.BlockSpec((None, rows, tn), lambda l, n: (l, 0, n)),
        out_shape=jax.ShapeDtypeStruct((depth, rows, n6), F32),
        compiler_params=_cparams(("arbitrary", "arbitrary")),
        name="ada_mod",
    )(c_all, w_ada, b_ada.reshape(depth, 1, n6))


def _lb_body(raw_ref, o_ref):
    raw = raw_ref[...]
    depth = raw.shape[0]
    e = jnp.exp(raw - jnp.max(raw, axis=0, keepdims=True))
    p = e / jnp.sum(e, axis=0, keepdims=True)
    acc = jnp.zeros_like(p[0:1])
    for l in range(depth):
        acc = acc + p[l:l + 1]
        o_ref[l:l + 1, :] = acc - p[0:1]


def _lb_call(hg_lb_raw):
    depth = hg_lb_raw.shape[0]
    raw = hg_lb_raw.reshape(depth, -1)
    return pl.pallas_call(
        _lb_body,
        out_shape=jax.ShapeDtypeStruct(raw.shape, F32),
        name="hgrn2_lower_bounds",
    )(raw)


def _in_proj_body(ctx_ref, x_ref, sh_ref, sc_ref, ln_ref, *rest, n_out):
    w_refs = rest[:n_out]
    b_refs = rest[n_out:2 * n_out]
    o_refs = rest[2 * n_out:]
    t = pl.program_id(1)
    xin = jnp.where(t == 0, ctx_ref[...], x_ref[...])
    h = _norm_mod(xin, ln_ref[...], sh_ref[...], sc_ref[...]).astype(BF16)
    for w_ref, b_ref, o_ref in zip(w_refs, b_refs, o_refs):
        o_ref[...] = (_dot(h, w_ref[...]) + b_ref[...]).astype(o_ref.dtype)


def _in_proj_call(l, ctx, x, mods4, ln1, ws, bs, out_dtypes):
    bsz, lc, d_model = ctx.shape
    seq = x.shape[1]
    tm = lc
    nt = 1 + seq // tm
    n_out = len(ws)

    def mod_spec(chunk):
        return pl.BlockSpec((None, None, 1, d_model),
                            lambda b, t: (l, jnp.where(t == 0, bsz, b), 0, chunk))

    in_specs = [
        pl.BlockSpec((None, tm, d_model), lambda b, t: (b, 0, 0)),
        pl.BlockSpec((None, tm, d_model), lambda b, t: (b, jnp.maximum(t - 1, 0), 0)),
        mod_spec(0), mod_spec(1),
        pl.BlockSpec((None, 1, d_model), lambda b, t: (l, 0, 0)),
    ]
    in_specs += [pl.BlockSpec((None, d_model, w.shape[-1]), lambda b, t: (l, 0, 0)) for w in ws]
    in_specs += [pl.BlockSpec((None, 1, w.shape[-1]), lambda b, t: (l, 0, 0)) for w in ws]
    out_specs = [pl.BlockSpec((None, tm, w.shape[-1]), lambda b, t: (b, t, 0)) for w in ws]
    out_shape = [jax.ShapeDtypeStruct((bsz, lc + seq, w.shape[-1]), dt) for w, dt in zip(ws, out_dtypes)]
    return pl.pallas_call(
        functools.partial(_in_proj_body, n_out=n_out),
        grid=(bsz, nt),
        in_specs=in_specs,
        out_specs=out_specs,
        out_shape=out_shape,
        compiler_params=_cparams(("arbitrary", "arbitrary")),
        name="in_proj",
    )(ctx, x, mods4, mods4, ln1, *ws, *bs)


def _hgrn2_body(q_ref, v_ref, f_ref, lb_ref, mst_ref, msk_ref, o_ref, st_ref, *, C, n_lv):
    q = q_ref[...].astype(F32)
    q = q * _sigmoid(q)
    v = v_ref[...]
    fp = f_ref[...]
    lb = lb_ref[...]
    e = jnp.exp(-jnp.abs(fp))
    inv = 1.0 / (1.0 + e)
    pos = fp >= 0.0
    sig = jnp.where(pos, inv, e * inv)
    nsig = jnp.where(pos, e * inv, inv)
    g = jnp.log(jnp.maximum(lb, LB_TINY) + (1.0 - lb) * sig) * LOG2_E
    kk = (1.0 - lb) * nsig
    g_hi, g_lo = _split_hi_lo(g)
    ex = _dot(mst_ref[...], jnp.concatenate([g_hi, g_lo], axis=0))
    tot = jnp.sum(g, axis=0, keepdims=True)
    q16 = q.astype(BF16)
    k16 = kk.astype(BF16)

    for h in range(N_HEADS):
        sl = slice(h * HG_DK, (h + 1) * HG_DK)
        qh = q16[:, sl]
        kh = k16[:, sl]
        p = msk_ref[n_lv] * _dot_nt(qh, kh)
        for lv in range(n_lv):
            w = jnp.exp2(ex[lv * C:(lv + 1) * C, sl]).astype(BF16)
            p = p + msk_ref[lv] * _dot_nt(qh * w, kh * w)
            if lv % 2 == 1:
                yield
        st = st_ref[h]
        b_in = ex[n_lv * C:(n_lv + 1) * C, sl]
        qb = (q[:, sl] * jnp.exp2(b_in)).astype(BF16)
        vh = v[:, sl]
        o_ref[:, sl] = (_dot(p.astype(BF16), vh) + _dot_nt(qb, st.astype(BF16))).astype(o_ref.dtype)
        tot_h = tot[:, sl]
        kb = (kk[:, sl] * jnp.exp2(tot_h - b_in)).astype(BF16)
        st_ref[h] = st * jnp.exp2(tot_h) + _dot_tn(vh, kb)
        yield


def _mlstm_body(q_ref, k_ref, v_ref, g_ref, tri_ref, trit_ref, msk_ref, o_ref, c_ref, m_ref, *, C):
    gates = g_ref[...]
    gates_t = gates.T
    lf = _log_sigmoid(gates)
    lf_t = _log_sigmoid(gates_t)
    lf_hi, lf_lo = _split_hi_lo(lf)
    b_cols = _dot(tri_ref[...], jnp.concatenate([lf_hi, lf_lo], axis=0))
    lft_hi, lft_lo = _split_hi_lo(lf_t)
    b_rows = _dot(jnp.concatenate([lft_hi, lft_lo], axis=1), trit_ref[...])
    tot = jnp.sum(lf, axis=0, keepdims=True)
    allowed = msk_ref[...] > 0.0
    ones = jnp.ones((C, HEAD_DV), BF16)
    scale = ML_DQK ** -0.5

    for h in range(N_HEADS):
        b_col = b_cols[:, N_HEADS + h:N_HEADS + h + 1]
        b_row = b_rows[N_HEADS + h:N_HEADS + h + 1, :]
        i_row = gates_t[h:h + 1, :]
        i_col = gates[:, h:h + 1]
        m_prev = m_ref[h][0:1, 0:1]
        log_d = jnp.where(allowed, b_col - b_row + i_row, NEG_BIG)
        log_inter = b_col + m_prev
        m_row = jnp.maximum(jnp.max(log_d, axis=-1, keepdims=True), log_inter)
        yield
        qh = q_ref[:, h * ML_DQK:(h + 1) * ML_DQK]
        kh = k_ref[:, h * ML_DQK:(h + 1) * ML_DQK]
        vaug = jnp.concatenate([v_ref[:, h * HEAD_DV:(h + 1) * HEAD_DV], ones], axis=1)
        s = (_dot_nt(qh, kh) * scale) * jnp.exp(log_d - m_row)
        w_inter = jnp.exp(log_inter - m_row)
        yield
        c_aug = c_ref[h]
        r = _dot(s.astype(BF16), vaug) + (w_inter * scale) * _dot(qh, c_aug.astype(BF16))
        num = r[:, :HEAD_DV]
        den = r[:, HEAD_DV:]
        hout = num / jnp.maximum(jnp.abs(den), jnp.exp(-m_row))
        o_ref[:, h * HEAD_DV:(h + 1) * HEAD_DV] = hout.astype(o_ref.dtype)
        yield
        b_last = tot[:, N_HEADS + h:N_HEADS + h + 1]
        log_w = b_last - b_col + i_col
        m_new = jnp.maximum(b_last + m_prev, jnp.max(log_w, axis=0, keepdims=True))
        w_s = jnp.exp(log_w - m_new)
        decay = jnp.exp(b_last + m_prev - m_new)
        kw = (kh.astype(F32) * w_s).astype(BF16)
        c_ref[h] = decay * c_aug + _dot_tn(kw, vaug)
        m_ref[h] = jnp.broadcast_to(m_new, m_ref.shape[1:])
        yield


def _mixers_body(hq_ref, hv_ref, hf_ref, lb_ref, mst_ref, hmsk_ref, mq_ref, mk_ref, mv_ref, mg_ref,
                 tri_ref, trit_ref, mmsk_ref, o_hg_ref, o_ml_ref, st_ref, c_ref, m_ref, *, C, n_lv):
    @pl.when(pl.program_id(2) == 0)
    def _():
        st_ref[...] = jnp.zeros_like(st_ref)
        c_ref[...] = jnp.zeros_like(c_ref)
        m_ref[...] = jnp.zeros_like(m_ref)

    hg = _hgrn2_body(hq_ref, hv_ref, hf_ref, lb_ref, mst_ref, hmsk_ref, o_hg_ref, st_ref, C=C, n_lv=n_lv)
    ml = _mlstm_body(mq_ref, mk_ref, mv_ref, mg_ref, tri_ref, trit_ref, mmsk_ref, o_ml_ref, c_ref, m_ref, C=C)
    for _ in itertools.zip_longest(hg, ml):
        pass


def _mixers_call(l, hg_a, hg_f, lb4, ml_a, ml_g, lc):
    bsz, tt, _ = hg_a.shape
    C = CHUNK
    nch, nctx = tt // C, lc // C
    mst, hmsk, n_lv = _hgrn2_consts(C)
    mst = jnp.asarray(mst, BF16)
    hmsk = jnp.asarray(hmsk, F32)
    tri_cat, tri_t_cat, tri = _mlstm_consts(C)
    tri_cat = jnp.asarray(tri_cat, BF16)
    tri_t_cat = jnp.asarray(tri_t_cat, BF16)
    mmsk = jnp.asarray(tri, F32)
    cidx = functools.partial(_chunk_index, nctx=nctx, nch=nch)
    qk_w = N_HEADS * ML_DQK
    tok = lambda col: (lambda b, d, j: (b, cidx(d, j), col))
    tok_d = lambda b, d, j: (b, cidx(d, j), d)
    per_dir3 = lambda b, d, j: (d, 0, 0)
    o_spec = pl.BlockSpec((None, None, C, MIX_W), lambda b, d, j: (b, d, cidx(d, j), 0))
    o_shape = jax.ShapeDtypeStruct((bsz, 2, tt, MIX_W), BF16)
    return pl.pallas_call(
        functools.partial(_mixers_body, C=C, n_lv=n_lv),
        grid=(bsz, 2, nch),
        in_specs=[
            pl.BlockSpec((None, C, MIX_W), tok(0)),
            pl.BlockSpec((None, C, MIX_W), tok(1)),
            pl.BlockSpec((None, C, MIX_W), tok_d),
            pl.BlockSpec((None, None, 1, MIX_W), lambda b, d, j: (l, d, 0, 0)),
            pl.BlockSpec((None,) + mst.shape[1:], per_dir3),
            pl.BlockSpec((None,) + hmsk.shape[1:], lambda b, d, j: (d, 0, 0, 0)),
            pl.BlockSpec((None, C, qk_w), tok(0)),
            pl.BlockSpec((None, C, qk_w), tok(1)),
            pl.BlockSpec((None, C, MIX_W), tok(1)),
            pl.BlockSpec((None, C, GATE_LANES), tok_d),
            pl.BlockSpec((None, C, 2 * C), per_dir3),
            pl.BlockSpec((None, 2 * C, C), per_dir3),
            pl.BlockSpec((None, C, C), per_dir3),
        ],
        out_specs=[o_spec, o_spec],
        out_shape=[o_shape, o_shape],
        scratch_shapes=[pltpu.VMEM((N_HEADS, HEAD_DV, HG_DK), F32),
                        pltpu.VMEM((N_HEADS, ML_DQK, 2 * HEAD_DV), F32),
                        pltpu.VMEM((N_HEADS, 8, 128), F32)],
        compiler_params=_cparams(("arbitrary", "arbitrary", "arbitrary")),
        name="mixers",
    )(hg_a, hg_a, hg_f, lb4, mst, hmsk, ml_a, ml_a, ml_a, ml_g, tri_cat, tri_t_cat, mmsk)


def _lru_gates_body(x_ref, xp_ref, xn_ref, cw_ref, cb_ref, wg_ref, bg_ref, lam_ref,
                    a0_ref, u0_ref, a1_ref, u1_ref, ext_ref, *, tc, nctx_t, nt):
    t = pl.program_id(1)
    prev_ok = jnp.logical_and(t != 0, t != nctx_t)
    next_ok = jnp.logical_and(t != nctx_t - 1, t != nt - 1)
    x = x_ref[...]
    ext_ref[0:8, :] = jnp.where(prev_ok, xp_ref[...], 0.0)
    ext_ref[8:8 + tc, :] = x
    ext_ref[8 + tc:16 + tc, :] = jnp.where(next_ok, xn_ref[...], 0.0)
    cw = cw_ref[...]
    xc = (cw[0:1] * ext_ref[6:6 + tc, :] + cw[1:2] * ext_ref[7:7 + tc, :] + cw[2:3] * x
          + cw[3:4] * ext_ref[9:9 + tc, :]) + cb_ref[...]
    gates = _dot(xc.astype(BF16), wg_ref[...]) + bg_ref[...]
    outs = ((a0_ref, u0_ref), (a1_ref, u1_ref))
    for d in range(2):
        r_pre = gates[:, (2 * d) * MIX_W:(2 * d + 1) * MIX_W]
        i_pre = gates[:, (2 * d + 1) * MIX_W:(2 * d + 2) * MIX_W]
        z = -lam_ref[d:d + 1, :]
        softplus = jnp.maximum(z, 0.0) + jnp.log(1.0 + jnp.exp(-jnp.abs(z)))
        log_a = (-LRU_C * softplus) * _sigmoid(r_pre)
        a_ref, u_ref = outs[d]
        a = jnp.exp(log_a)
        a_ref[...] = a
        one_m_a2 = -jnp.tanh(log_a) * (a * a + 1.0)
        u_ref[...] = jnp.sqrt(jnp.maximum(one_m_a2, 0.0)) * (_sigmoid(i_pre) * xc)


def _lru_gates_call(l, lru_x, conv_w, conv_b, wg, bg, lam, lc):
    bsz, tt, w = lru_x.shape
    tc = lc
    nt, nctx_t = tt // tc, lc // tc
    h8 = tc // 8
    out = jax.ShapeDtypeStruct((bsz, tt, w), F32)
    o_spec = pl.BlockSpec((None, tc, w), lambda b, t: (b, t, 0))
    return pl.pallas_call(
        functools.partial(_lru_gates_body, tc=tc, nctx_t=nctx_t, nt=nt),
        grid=(bsz, nt),
        in_specs=[
            pl.BlockSpec((None, tc, w), lambda b, t: (b, t, 0)),
            pl.BlockSpec((None, 8, w), lambda b, t: (b, jnp.maximum(t * h8 - 1, 0), 0)),
            pl.BlockSpec((None, 8, w), lambda b, t: (b, jnp.minimum((t + 1) * h8, tt // 8 - 1), 0)),
            pl.BlockSpec((None, 4, w), lambda b, t: (l, 0, 0)),
            pl.BlockSpec((None, 1, w), lambda b, t: (l, 0, 0)),
            pl.BlockSpec((None, w, 4 * w), lambda b, t: (l, 0, 0)),
            pl.BlockSpec((None, 1, 4 * w), lambda b, t: (l, 0, 0)),
            pl.BlockSpec((None, 2, w), lambda b, t: (l, 0, 0)),
        ],
        out_specs=[o_spec] * 4,
        out_shape=[out] * 4,
        scratch_shapes=[pltpu.VMEM((tc + 16, w), F32)],
        compiler_params=_cparams(("arbitrary", "arbitrary")),
        name="rglru_gates",
    )(lru_x, lru_x, lru_x, conv_w, conv_b, wg, bg, lam)


def _lru_scan_body(a0_ref, u0_ref, a1_ref, u1_ref, hf_ref, hb_ref, sf_ref, sb_ref, *, tc):
    @pl.when(pl.program_id(0) == 0)
    def _():
        sf_ref[...] = jnp.zeros_like(sf_ref)
        sb_ref[...] = jnp.zeros_like(sb_ref)

    def step(i, carry):
        hf, hb = carry
        hf = a0_ref[:, i, :] * hf + u0_ref[:, i, :]
        hf_ref[:, i, :] = hf
        ib = tc - 1 - i
        hb = a1_ref[:, ib, :] * hb + u1_ref[:, ib, :]
        hb_ref[:, ib, :] = hb
        return hf, hb

    hf, hb = lax.fori_loop(0, tc, step, (sf_ref[...], sb_ref[...]), unroll=8)
    sf_ref[...] = hf
    sb_ref[...] = hb


def _lru_scan_call(a0, u0, a1, u1, lc):
    bsz, tt, w = a0.shape
    tc = CHUNK
    nch, nctx = tt // tc, lc // tc
    fw = pl.BlockSpec((bsz, tc, w), lambda j: (0, j, 0))
    bw = pl.BlockSpec((bsz, tc, w), lambda j: (0, _chunk_index(1, j, nctx, nch), 0))
    out = jax.ShapeDtypeStruct((bsz, tt, w), F32)
    return pl.pallas_call(
        functools.partial(_lru_scan_body, tc=tc),
        grid=(nch,),
        in_specs=[fw, fw, bw, bw],
        out_specs=[fw, bw],
        out_shape=[out, out],
        scratch_shapes=[pltpu.VMEM((bsz, w), F32), pltpu.VMEM((bsz, w), F32)],
        compiler_params=_cparams(("arbitrary",)),
        name="rglru_scan",
    )(a0, u0, a1, u1)


def _head_rms(o, w):
    parts = []
    for h in range(N_HEADS):
        oh = o[:, h * HEAD_DV:(h + 1) * HEAD_DV]
        parts.append(oh * lax.rsqrt(jnp.mean(oh * oh, axis=-1, keepdims=True) + EPS))
    return jnp.concatenate(parts, axis=1) * w


def _gelu_tanh(x):
    return 0.5 * x * (1.0 + jnp.tanh(0.7978845608028654 * (x + 0.044715 * (x * x * x))))


def _merge_body(hg0_ref, hg1_ref, ml0_ref, ml1_ref, lf_ref, lb_ref, hgg_ref, mlo_ref, ly_ref, mg_ref,
                ctx_ref, x_ref, gate_ref, hgn_ref, mln_ref, wb_ref, wo_ref, ctx_o_ref, x_o_ref,
                *, t0, d_model):
    t = pl.program_id(1) + t0
    hgg = hgg_ref[...].astype(F32)
    o_hg = hg0_ref[...].astype(F32) + hg1_ref[...].astype(F32)
    o_ml = ml0_ref[...].astype(F32) + ml1_ref[...].astype(F32)
    a_out = _head_rms(o_hg, hgn_ref[...]) * (hgg * _sigmoid(hgg))
    b_out = _head_rms(o_ml, mln_ref[...]) * _sigmoid(mlo_ref[...].astype(F32))
    c_out = (lf_ref[...] + lb_ref[...]) * _gelu_tanh(ly_ref[...].astype(F32))
    merged = None
    for n, br in enumerate((a_out, b_out, c_out)):
        gate = _sigmoid(mg_ref[:, n * d_model:(n + 1) * d_model].astype(F32))
        term = gate * _dot(br.astype(BF16), wb_ref[n])
        merged = term if merged is None else merged + term
    y = gate_ref[...] * _dot(merged.astype(BF16), wo_ref[...])

    if t0 == 0:
        @pl.when(t == 0)
        def _():
            ctx_o_ref[...] = ctx_ref[...] + y
    else:
        @pl.when(t == t0)
        def _():
            ctx_o_ref[...] = ctx_ref[...]

    @pl.when(t > 0)
    def _():
        x_o_ref[...] = x_ref[...] + y


def _merge_call(l, last, o_hg, o_ml, h_f, h_b, hg_a, ml_a, lru_y, mg, ctx, x, mods4,
                hg_norm, ml_norm, w_branch, w_out):
    bsz, lc, d_model = ctx.shape
    seq = x.shape[1]
    tm = lc
    t0 = 1 if last else 0
    nt = 1 + seq // tm - t0

    def tok(b, t):
        return t + t0

    def xi(t):
        return jnp.maximum(t + t0 - 1, 0)

    mix = lambda d: pl.BlockSpec((None, None, tm, MIX_W), lambda b, t: (b, d, tok(b, t), 0))
    lru = pl.BlockSpec((None, tm, MIX_W), lambda b, t: (b, tok(b, t), 0))
    in_specs = [
        mix(0), mix(1), mix(0), mix(1), lru, lru,
        pl.BlockSpec((None, tm, MIX_W), lambda b, t: (b, tok(b, t), 2)),
        pl.BlockSpec((None, tm, MIX_W), lambda b, t: (b, tok(b, t), 2)),
        pl.BlockSpec((None, tm, MIX_W), lambda b, t: (b, tok(b, t), 0)),
        pl.BlockSpec((None, tm, 3 * d_model), lambda b, t: (b, tok(b, t), 0)),
        pl.BlockSpec((None, tm, d_model), lambda b, t: (b, 0, 0)),
        pl.BlockSpec((None, tm, d_model), lambda b, t: (b, xi(t), 0)),
        pl.BlockSpec((None, None, 1, d_model), lambda b, t: (l, jnp.where(t + t0 == 0, bsz, b), 0, 2)),
        pl.BlockSpec((None, 1, MIX_W), lambda b, t: (l, 0, 0)),
        pl.BlockSpec((None, 1, MIX_W), lambda b, t: (l, 0, 0)),
        pl.BlockSpec((None, 3, MIX_W, d_model), lambda b, t: (l, 0, 0, 0)),
        pl.BlockSpec((None, d_model, d_model), lambda b, t: (l, 0, 0)),
    ]
    out_specs = [
        pl.BlockSpec((None, tm, d_model), lambda b, t: (b, 0, 0)),
        pl.BlockSpec((None, tm, d_model), lambda b, t: (b, xi(t), 0)),
    ]
    out_shape = [jax.ShapeDtypeStruct(ctx.shape, F32), jax.ShapeDtypeStruct(x.shape, F32)]
    return pl.pallas_call(
        functools.partial(_merge_body, t0=t0, d_model=d_model),
        grid=(bsz, nt),
        in_specs=in_specs,
        out_specs=out_specs,
        out_shape=out_shape,
        compiler_params=_cparams(("arbitrary", "arbitrary")),
        name="merge",
    )(o_hg, o_hg, o_ml, o_ml, h_f, h_b, hg_a, ml_a, lru_y, mg, ctx, x, mods4,
      hg_norm, ml_norm, w_branch, w_out)


def _ffn_body(x_ref, sh_ref, sc_ref, gt_ref, ln_ref, wg_ref, wu_ref, wo_ref, fn_ref, o_ref,
              h_ref, acc_ref, *, n_k, n_inner, final_norm):
    k = pl.program_id(2)

    @pl.when(k == 0)
    def _():
        h_ref[...] = _norm_mod(x_ref[...], ln_ref[...], sh_ref[...], sc_ref[...]).astype(BF16)
        acc_ref[...] = jnp.zeros_like(acc_ref)

    h = h_ref[...]
    gate = _dot(h, wg_ref[...])
    up = _dot(h, wu_ref[...])
    act = (gate * _sigmoid(gate) * up).astype(BF16)
    acc_ref[...] += _dot(act, wo_ref[...])

    @pl.when(k == n_k - 1)
    def _():
        y = x_ref[...] + gt_ref[...] * acc_ref[...]
        if final_norm:
            y = y * lax.rsqrt(jnp.mean(y * y, axis=-1, keepdims=True) + EPS) * fn_ref[...]
        if n_inner is None:
            o_ref[...] = y
        else:
            for a in range(y.shape[0] // n_inner):
                o_ref[:, a, :] = y[a * n_inner:(a + 1) * n_inner, :]


def _ffn_call(l, x, mods4, ctx_row, ln2, wg, wu, wo, fin, *, tm, n_inner, final_norm):
    bsz, n, d_model = x.shape
    n_k, _, tk = wg.shape[1:]
    nt = n // tm

    def mod_spec(chunk):
        return pl.BlockSpec((None, None, 1, d_model),
                            lambda b, t, k: (l, bsz if ctx_row else b, 0, chunk))

    if n_inner is None:
        o_spec = pl.BlockSpec((None, tm, d_model), lambda b, t, k: (b, t, 0))
        o_shape = jax.ShapeDtypeStruct(x.shape, F32)
    else:
        n_a = tm // n_inner
        o_spec = pl.BlockSpec((None, n_inner, n_a, d_model), lambda b, t, k: (b, 0, t, 0))
        o_shape = jax.ShapeDtypeStruct((bsz, n_inner, n // n_inner, d_model), F32)
    out = pl.pallas_call(
        functools.partial(_ffn_body, n_k=n_k, n_inner=n_inner, final_norm=final_norm),
        grid=(bsz, nt, n_k),
        in_specs=[
            pl.BlockSpec((None, tm, d_model), lambda b, t, k: (b, t, 0)),
            mod_spec(3), mod_spec(4), mod_spec(5),
            pl.BlockSpec((None, 1, d_model), lambda b, t, k: (l, 0, 0)),
            pl.BlockSpec((None, None, d_model, tk), lambda b, t, k: (l, k, 0, 0)),
            pl.BlockSpec((None, None, d_model, tk), lambda b, t, k: (l, k, 0, 0)),
            pl.BlockSpec((None, None, tk, d_model), lambda b, t, k: (l, k, 0, 0)),
            pl.BlockSpec((1, d_model), lambda b, t, k: (0, 0)),
        ],
        out_specs=o_spec,
        out_shape=o_shape,
        scratch_shapes=[pltpu.VMEM((tm, d_model), BF16), pltpu.VMEM((tm, d_model), F32)],
        compiler_params=_cparams(("arbitrary", "arbitrary", "arbitrary")),
        name="ffn",
    )(x, mods4, mods4, mods4, ln2, wg, wu, wo, fin)
    return out.reshape(bsz, n, d_model)


def _prep_in_proj(w_in, b_in):
    depth, d_model, _ = w_in.shape
    hq = N_HEADS * HG_DK
    sizes = (hq, MIX_W, MIX_W, 2 * hq, N_HEADS * ML_DQK, N_HEADS * ML_DQK, MIX_W, MIX_W,
             2 * N_HEADS, 2 * N_HEADS, MIX_W, MIX_W, 3 * d_model)
    offs = np.concatenate([[0], np.cumsum(sizes)])
    assert offs[-1] == w_in.shape[-1]

    def cols(a, lo, hi):
        return a[..., int(offs[lo]):int(offs[hi])]

    def gate_block(a):
        ig = cols(a, 8, 9).reshape(a.shape[:-1] + (2, N_HEADS))
        fg = cols(a, 9, 10).reshape(a.shape[:-1] + (2, N_HEADS))
        pad = jnp.zeros(a.shape[:-1] + (2, GATE_LANES - 2 * N_HEADS), a.dtype)
        return jnp.concatenate([ig, fg, pad], axis=-1).reshape(a.shape[:-1] + (2 * GATE_LANES,))

    groups = [
        (lambda a: cols(a, 0, 3), BF16),
        (lambda a: cols(a, 3, 4), F32),
        (lambda a: cols(a, 4, 8), BF16),
        (gate_block, F32),
        (lambda a: cols(a, 10, 11), F32),
        (lambda a: cols(a, 11, 12), BF16),
        (lambda a: cols(a, 12, 13), BF16),
    ]
    b3 = b_in.reshape(depth, 1, -1)
    ws = [f(w_in).astype(BF16) for f, _ in groups]
    bs = [f(b3) for f, _ in groups]
    return ws, bs, [dt for _, dt in groups]


def _prep_lru_gates(lru_gate_w, lru_gate_b):
    depth = lru_gate_w.shape[0]
    eye = jnp.eye(LRU_BLOCKS, dtype=lru_gate_w.dtype)
    dense = jnp.einsum("lzgnde,nm->lndzgme", lru_gate_w, eye)
    dense = dense.reshape(depth, MIX_W, 4 * MIX_W).astype(BF16)
    return dense, lru_gate_b.reshape(depth, 1, 4 * MIX_W)


def _prep_ffn(w_ffn_in, w_ffn_out):
    depth, d_model, two_h = w_ffn_in.shape
    hidden = two_h // 2
    tk = hidden // 2 if (hidden // 2) % 128 == 0 else hidden
    n_k = hidden // tk
    wi = w_ffn_in.astype(BF16).reshape(depth, d_model, 2, n_k, tk)
    wg = jnp.transpose(wi[:, :, 0], (0, 2, 1, 3))
    wu = jnp.transpose(wi[:, :, 1], (0, 2, 1, 3))
    wo = w_ffn_out.astype(BF16).reshape(depth, n_k, tk, d_model)
    return wg, wu, wo


def kernel(x, c, ctx, c_ctx, w_ada, b_ada, ln1, w_in, b_in, hg_lb_raw, hg_norm, ml_norm, conv_w, conv_b,
           lru_gate_w, lru_gate_b, lru_lambda, w_branch, w_out, ln2, w_ffn_in, w_ffn_out, final_norm):
    bsz, seq, d_model = x.shape
    lc = ctx.shape[1]
    depth = w_ada.shape[0]
    rows = seq // GRID_W
    assert lc % CHUNK == 0 and seq % lc == 0 and seq % FFN_TM == 0 and bsz < 16

    c_all = jnp.zeros((16, d_model), F32).at[:bsz].set(c).at[bsz].set(c_ctx)
    mods4 = _ada_call(c_all, w_ada, b_ada).reshape(depth, 16, 1, 6 * d_model)
    lb4 = _lb_call(hg_lb_raw).reshape(depth, 2, 1, MIX_W)

    ws, bs, out_dtypes = _prep_in_proj(w_in, b_in)
    wgate, bgate = _prep_lru_gates(lru_gate_w, lru_gate_b)
    wg, wu, wo = _prep_ffn(w_ffn_in, w_ffn_out)
    w_branch_b = w_branch.astype(BF16)
    w_out_b = w_out.astype(BF16)
    ln1_3 = ln1.reshape(depth, 1, d_model)
    ln2_3 = ln2.reshape(depth, 1, d_model)
    hgn = hg_norm.reshape(depth, 1, MIX_W)
    mln = ml_norm.reshape(depth, 1, MIX_W)
    conv_b3 = conv_b.reshape(depth, 1, MIX_W)
    fin = final_norm.reshape(1, d_model)

    for l in range(depth):
        last = l == depth - 1
        n_inner = GRID_W if l % 2 == 0 else rows
        hg_a, hg_f, ml_a, ml_g, lru_x, lru_y, mg = _in_proj_call(l, ctx, x, mods4, ln1_3, ws, bs, out_dtypes)
        o_hg, o_ml = _mixers_call(l, hg_a, hg_f, lb4, ml_a, ml_g, lc)
        a0, u0, a1, u1 = _lru_gates_call(l, lru_x, conv_w, conv_b3, wgate, bgate, lru_lambda, lc)
        h_f, h_b = _lru_scan_call(a0, u0, a1, u1, lc)
        ctx_m, x_m = _merge_call(l, last, o_hg, o_ml, h_f, h_b, hg_a, ml_a, lru_y, mg, ctx, x, mods4,
                                 hgn, mln, w_branch_b, w_out_b)
        x = _ffn_call(l, x_m, mods4, False, ln2_3, wg, wu, wo, fin,
                      tm=FFN_TM, n_inner=n_inner, final_norm=last)
        if not last:
            ctx = _ffn_call(l, ctx_m, mods4, True, ln2_3, wg, wu, wo, fin,
                            tm=lc, n_inner=None, final_norm=False)
    return x
```

```python
import functools
import itertools

import numpy as np
import jax
import jax.numpy as jnp
from jax import lax
from jax.experimental import pallas as pl
from jax.experimental.pallas import tpu as pltpu

F32 = jnp.float32
BF16 = jnp.bfloat16

GRID_W = 64
MIX_W = 512
N_HEADS = 4
HEAD_DV = 128
HG_DK = 128
ML_DQK = 64
LRU_BLOCKS = 8
LRU_BD = 64
LRU_C = 8.0
EPS = 1e-6
NEG_BIG = -1e30
LB_TINY = 1e-30
LOG2_E = 1.4426950408889634
GATE_LANES = 128
CHUNK = 128
MIX_NB = 1
FFN_TM = 512
VMEM_LIMIT = 56 * 1024 * 1024


def _cparams(sem):
    return pltpu.CompilerParams(dimension_semantics=sem, vmem_limit_bytes=VMEM_LIMIT)


def _sigmoid(x):
    return 1.0 / (1.0 + jnp.exp(-x))


def _log_sigmoid(x):
    return jnp.minimum(x, 0.0) - jnp.log(1.0 + jnp.exp(-jnp.abs(x)))


def _dot(a, b):
    return jnp.dot(a, b, preferred_element_type=F32)


def _dot_nt(a, b):
    return lax.dot_general(a, b, (((1,), (1,)), ((), ())), preferred_element_type=F32)


def _dot_tn(a, b):
    return lax.dot_general(a, b, (((0,), (0,)), ((), ())), preferred_element_type=F32)


def _split_hi_lo(x):
    hi = x.astype(BF16)
    lo = (x - hi.astype(F32)).astype(BF16)
    return hi, lo


def _norm_mod(x, ln, shift, scale):
    y = x * lax.rsqrt(jnp.mean(x * x, axis=-1, keepdims=True) + EPS) * ln
    return y * (1.0 + scale) + shift


def _mirror(m):
    return m[..., ::-1, ::-1].copy()


@functools.lru_cache(maxsize=None)
def _hgrn2_consts(C):
    n_lv = int(np.log2(C))
    mats, masks = [], []
    r = np.arange(C)
    for lv in range(n_lv):
        s = C >> (lv + 1)
        base = (r // (2 * s)) * (2 * s)
        mid = base + s - 1
        odd = r >= base + s
        m = np.zeros((C, C), np.float32)
        for t in range(C):
            if odd[t]:
                m[t, mid[t] + 1:t + 1] = 1.0
            else:
                m[t, t + 1:mid[t] + 1] = 1.0
        mats.append(m)
        same = base[:, None] == base[None, :]
        masks.append((same & odd[:, None] & (~odd)[None, :]).astype(np.float32))
    mats.append(np.tril(np.ones((C, C), np.float32)))
    masks.append(np.eye(C, dtype=np.float32))
    mats = np.stack(mats)
    masks = np.stack(masks)
    mats = np.stack([mats, _mirror(mats)])
    masks = np.stack([masks, _mirror(masks)])
    mst = mats.reshape(2, (n_lv + 1) * C, C)
    mst = np.concatenate([mst, mst], axis=-1)
    return mst, masks, n_lv


@functools.lru_cache(maxsize=None)
def _mlstm_consts(C):
    tri = np.tril(np.ones((C, C), np.float32))
    tri = np.stack([tri, _mirror(tri)])
    tri_cat = np.concatenate([tri, tri], axis=-1)
    tri_t = np.transpose(tri, (0, 2, 1))
    tri_t_cat = np.concatenate([tri_t, tri_t], axis=1)
    return tri_cat, tri_t_cat, tri


def _chunk_index(d, j, nctx, nch):
    bw = jnp.where(j < nctx, nctx - 1 - j, nch - 1 + nctx - j)
    return jnp.where(d == 0, j, bw)


def _ada_body(c_ref, w_ref, b_ref, o_ref):
    cc = c_ref[...]
    s = cc * _sigmoid(cc)
    o_ref[...] = jnp.dot(s, w_ref[...], preferred_element_type=F32,
                         precision=lax.Precision.HIGHEST) + b_ref[...]


def _ada_call(c_all, w_ada, b_ada):
    depth, d_model, n6 = w_ada.shape
    rows = c_all.shape[0]
    tn = 1536
    return pl.pallas_call(
        _ada_body,
        grid=(depth, n6 // tn),
        in_specs=[
            pl.BlockSpec((rows, d_model), lambda l, n: (0, 0)),
            pl.BlockSpec((None, d_model, tn), lambda l, n: (l, 0, n)),
            pl.BlockSpec((None, 1, tn), lambda l, n: (l, 0, n)),
        ],
        out_specs=pl.BlockSpec((None, rows, tn), lambda l, n: (l, 0, n)),
        out_shape=jax.ShapeDtypeStruct((depth, rows, n6), F32),
        compiler_params=_cparams(("arbitrary", "arbitrary")),
        name="ada_mod",
    )(c_all, w_ada, b_ada.reshape(depth, 1, n6))


def _lb_body(raw_ref, o_ref):
    raw = raw_ref[...]
    depth = raw.shape[0]
    e = jnp.exp(raw - jnp.max(raw, axis=0, keepdims=True))
    p = e / jnp.sum(e, axis=0, keepdims=True)
    acc = jnp.zeros_like(p[0:1])
    for l in range(depth):
        acc = acc + p[l:l + 1]
        o_ref[l:l + 1, :] = acc - p[0:1]


def _lb_call(hg_lb_raw):
    depth = hg_lb_raw.shape[0]
    raw = hg_lb_raw.reshape(depth, -1)
    return pl.pallas_call(
        _lb_body,
        out_shape=jax.ShapeDtypeStruct(raw.shape, F32),
        name="hgrn2_lower_bounds",
    )(raw)


def _log2_forget(f_pre, lb):
    e = jnp.exp(-jnp.abs(f_pre))
    inv = 1.0 / (1.0 + e)
    sig = jnp.where(f_pre >= 0.0, inv, e * inv)
    return jnp.log(jnp.maximum(lb, LB_TINY) + (1.0 - lb) * sig) * LOG2_E


def _in_proj_body(ctx_ref, x_ref, sh_ref, sc_ref, ln_ref, lb_ref, *rest, kinds):
    n_out = len(kinds)
    w_refs = rest[:n_out]
    b_refs = rest[n_out:2 * n_out]
    o_refs = rest[2 * n_out:]
    t = pl.program_id(1)
    xin = jnp.where(t == 0, ctx_ref[...], x_ref[...])
    h = _norm_mod(xin, ln_ref[...], sh_ref[...], sc_ref[...]).astype(BF16)
    for kind, w_ref, b_ref, o_ref in zip(kinds, w_refs, b_refs, o_refs):
        y = _dot(h, w_ref[...]) + b_ref[...]
        if kind == "hgrn2_qig":
            q = y[:, :MIX_W]
            o_ref[:, :MIX_W] = (q * _sigmoid(q)).astype(o_ref.dtype)
            o_ref[:, MIX_W:] = y[:, MIX_W:].astype(o_ref.dtype)
        elif kind == "hgrn2_forget":
            o_ref[...] = _log2_forget(y, lb_ref[...]).astype(o_ref.dtype)
        elif kind == "mlstm_gates":
            lane = lax.broadcasted_iota(jnp.int32, y.shape, 1) % GATE_LANES
            is_f = jnp.logical_and(lane >= N_HEADS, lane < 2 * N_HEADS)
            o_ref[...] = jnp.where(is_f, _log_sigmoid(y), y).astype(o_ref.dtype)
        else:
            o_ref[...] = y.astype(o_ref.dtype)


def _in_proj_call(l, ctx, x, mods4, ln1, lb3, ws, bs, out_dtypes, kinds):
    bsz, lc, d_model = ctx.shape
    seq = x.shape[1]
    tm = lc
    nt = 1 + seq // tm
    n_out = len(ws)

    def mod_spec(chunk):
        return pl.BlockSpec((None, None, 1, d_model),
                            lambda b, t: (l, jnp.where(t == 0, bsz, b), 0, chunk))

    in_specs = [
        pl.BlockSpec((None, tm, d_model), lambda b, t: (b, 0, 0)),
        pl.BlockSpec((None, tm, d_model), lambda b, t: (b, jnp.maximum(t - 1, 0), 0)),
        mod_spec(0), mod_spec(1),
        pl.BlockSpec((None, 1, d_model), lambda b, t: (l, 0, 0)),
        pl.BlockSpec((None, 1, lb3.shape[-1]), lambda b, t: (l, 0, 0)),
    ]
    in_specs += [pl.BlockSpec((None, d_model, w.shape[-1]), lambda b, t: (l, 0, 0)) for w in ws]
    in_specs += [pl.BlockSpec((None, 1, w.shape[-1]), lambda b, t: (l, 0, 0)) for w in ws]
    out_specs = [pl.BlockSpec((None, tm, w.shape[-1]), lambda b, t: (b, t, 0)) for w in ws]
    out_shape = [jax.ShapeDtypeStruct((bsz, lc + seq, w.shape[-1]), dt) for w, dt in zip(ws, out_dtypes)]
    return pl.pallas_call(
        functools.partial(_in_proj_body, kinds=kinds),
        grid=(bsz, nt),
        in_specs=in_specs,
        out_specs=out_specs,
        out_shape=out_shape,
        compiler_params=_cparams(("arbitrary", "arbitrary")),
        name="in_proj",
    )(ctx, x, mods4, mods4, ln1, lb3, *ws, *bs)


def _hgrn2_body(qv_ref, g_ref, lb_ref, mst_ref, msk_ref, o_ref, st_ref, *, C, n_lv):
    q16 = qv_ref[:, :MIX_W]
    v = qv_ref[:, MIX_W:]
    q = q16.astype(F32)
    g = g_ref[...]
    lb = lb_ref[...]
    kk = (1.0 - jnp.exp2(g)) + (jnp.maximum(lb, LB_TINY) - lb)
    g_hi, g_lo = _split_hi_lo(g)
    ex = _dot(mst_ref[...], jnp.concatenate([g_hi, g_lo], axis=0))
    tot = jnp.sum(g, axis=0, keepdims=True)
    k16 = kk.astype(BF16)

    for h in range(N_HEADS):
        sl = slice(h * HG_DK, (h + 1) * HG_DK)
        qh = q16[:, sl]
        kh = k16[:, sl]
        p = msk_ref[n_lv] * _dot_nt(qh, kh)
        for lv in range(n_lv):
            w = jnp.exp2(ex[lv * C:(lv + 1) * C, sl]).astype(BF16)
            p = p + msk_ref[lv] * _dot_nt(qh * w, kh * w)
            if lv % 2 == 1:
                yield
        st = st_ref[h]
        b_in = ex[n_lv * C:(n_lv + 1) * C, sl]
        qb = (q[:, sl] * jnp.exp2(b_in)).astype(BF16)
        vh = v[:, sl]
        o_ref[:, sl] = (_dot(p.astype(BF16), vh) + _dot_nt(qb, st.astype(BF16))).astype(o_ref.dtype)
        tot_h = tot[:, sl]
        kb = (kk[:, sl] * jnp.exp2(tot_h - b_in)).astype(BF16)
        st_ref[h] = st * jnp.exp2(tot_h) + _dot_tn(vh, kb)
        yield


def _mlstm_body(qkv_ref, g_ref, tri_ref, trit_ref, msk_ref, o_ref, c_ref, m_ref, *, C):
    qk_w = N_HEADS * ML_DQK
    gates = g_ref[...]
    gates_t = gates.T
    lf = gates
    lf_t = gates_t
    lf_hi, lf_lo = _split_hi_lo(lf)
    b_cols = _dot(tri_ref[...], jnp.concatenate([lf_hi, lf_lo], axis=0))
    lft_hi, lft_lo = _split_hi_lo(lf_t)
    b_rows = _dot(jnp.concatenate([lft_hi, lft_lo], axis=1), trit_ref[...])
    tot = jnp.sum(lf, axis=0, keepdims=True)
    allowed = msk_ref[...] > 0.0
    ones = jnp.ones((C, HEAD_DV), BF16)
    scale = ML_DQK ** -0.5

    for h in range(N_HEADS):
        b_col = b_cols[:, N_HEADS + h:N_HEADS + h + 1]
        b_row = b_rows[N_HEADS + h:N_HEADS + h + 1, :]
        i_row = gates_t[h:h + 1, :]
        i_col = gates[:, h:h + 1]
        m_prev = m_ref[h][0:1, 0:1]
        log_d = jnp.where(allowed, b_col - b_row + i_row, NEG_BIG)
        log_inter = b_col + m_prev
        m_row = jnp.maximum(jnp.max(log_d, axis=-1, keepdims=True), log_inter)
        yield
        qh = qkv_ref[:, h * ML_DQK:(h + 1) * ML_DQK]
        kh = qkv_ref[:, qk_w + h * ML_DQK:qk_w + (h + 1) * ML_DQK]
        vh = qkv_ref[:, 2 * qk_w + h * HEAD_DV:2 * qk_w + (h + 1) * HEAD_DV]
        vaug = jnp.concatenate([vh, ones], axis=1)
        s = (_dot_nt(qh, kh) * scale) * jnp.exp(log_d - m_row)
        w_inter = jnp.exp(log_inter - m_row)
        yield
        c_aug = c_ref[h]
        r = _dot(s.astype(BF16), vaug) + (w_inter * scale) * _dot(qh, c_aug.astype(BF16))
        num = r[:, :HEAD_DV]
        den = r[:, HEAD_DV:]
        hout = num / jnp.maximum(jnp.abs(den), jnp.exp(-m_row))
        o_ref[:, h * HEAD_DV:(h + 1) * HEAD_DV] = hout.astype(o_ref.dtype)
        yield
        b_last = tot[:, N_HEADS + h:N_HEADS + h + 1]
        log_w = b_last - b_col + i_col
        m_new = jnp.maximum(b_last + m_prev, jnp.max(log_w, axis=0, keepdims=True))
        w_s = jnp.exp(log_w - m_new)
        decay = jnp.exp(b_last + m_prev - m_new)
        kw = (kh.astype(F32) * w_s).astype(BF16)
        c_ref[h] = decay * c_aug + _dot_tn(kw, vaug)
        m_ref[h] = jnp.broadcast_to(m_new, m_ref.shape[1:])
        yield


def _mixers_body(hqv_ref, hg_ref, lb_ref, mst_ref, hmsk_ref, mqkv_ref, mg_ref,
                 tri_ref, trit_ref, mmsk_ref, o_hg_ref, o_ml_ref, st_ref, c_ref, m_ref, *, C, n_lv):
    @pl.when(pl.program_id(2) == 0)
    def _():
        st_ref[...] = jnp.zeros_like(st_ref)
        c_ref[...] = jnp.zeros_like(c_ref)
        m_ref[...] = jnp.zeros_like(m_ref)

    gens = []
    for bb in range(MIX_NB):
        gens.append(_hgrn2_body(hqv_ref.at[bb], hg_ref.at[bb], lb_ref, mst_ref, hmsk_ref,
                                o_hg_ref.at[bb], st_ref.at[bb], C=C, n_lv=n_lv))
        gens.append(_mlstm_body(mqkv_ref.at[bb], mg_ref.at[bb], tri_ref, trit_ref,
                                mmsk_ref, o_ml_ref.at[bb], c_ref.at[bb], m_ref.at[bb], C=C))
    for _ in itertools.zip_longest(*gens):
        pass


def _mixers_call(l, hg_a, hg_f, lb4, ml_a, ml_g, lc):
    bsz, tt, _ = hg_a.shape
    C = CHUNK
    nch, nctx = tt // C, lc // C
    mst, hmsk, n_lv = _hgrn2_consts(C)
    mst = jnp.asarray(mst, BF16)
    hmsk = jnp.asarray(hmsk, F32)
    tri_cat, tri_t_cat, tri = _mlstm_consts(C)
    tri_cat = jnp.asarray(tri_cat, BF16)
    tri_t_cat = jnp.asarray(tri_t_cat, BF16)
    mmsk = jnp.asarray(tri, F32)
    cidx = functools.partial(_chunk_index, nctx=nctx, nch=nch)
    qk_w = N_HEADS * ML_DQK
    tok = lambda col: (lambda b, d, j: (b, cidx(d, j), col))
    tok_d = lambda b, d, j: (b, cidx(d, j), d)
    per_dir3 = lambda b, d, j: (d, 0, 0)
    nb = MIX_NB
    assert bsz % nb == 0
    o_spec = pl.BlockSpec((nb, None, C, MIX_W), lambda b, d, j: (b, d, cidx(d, j), 0))
    o_shape = jax.ShapeDtypeStruct((bsz, 2, tt, MIX_W), BF16)
    return pl.pallas_call(
        functools.partial(_mixers_body, C=C, n_lv=n_lv),
        grid=(bsz // nb, 2, nch),
        in_specs=[
            pl.BlockSpec((nb, C, 2 * MIX_W), tok(0)),
            pl.BlockSpec((nb, C, MIX_W), tok_d),
            pl.BlockSpec((None, None, 1, MIX_W), lambda b, d, j: (l, d, 0, 0)),
            pl.BlockSpec((None,) + mst.shape[1:], per_dir3),
            pl.BlockSpec((None,) + hmsk.shape[1:], lambda b, d, j: (d, 0, 0, 0)),
            pl.BlockSpec((nb, C, 2 * qk_w + MIX_W), tok(0)),
            pl.BlockSpec((nb, C, GATE_LANES), tok_d),
            pl.BlockSpec((None, C, 2 * C), per_dir3),
            pl.BlockSpec((None, 2 * C, C), per_dir3),
            pl.BlockSpec((None, C, C), per_dir3),
        ],
        out_specs=[o_spec, o_spec],
        out_shape=[o_shape, o_shape],
        scratch_shapes=[pltpu.VMEM((nb, N_HEADS, HEAD_DV, HG_DK), F32),
                        pltpu.VMEM((nb, N_HEADS, ML_DQK, 2 * HEAD_DV), F32),
                        pltpu.VMEM((nb, N_HEADS, 8, 128), F32)],
        compiler_params=_cparams(("arbitrary", "arbitrary", "arbitrary")),
        name="mixers",
    )(hg_a, hg_f, lb4, mst, hmsk, ml_a, ml_g, tri_cat, tri_t_cat, mmsk)


def _lru_gates_body(x_ref, xp_ref, xn_ref, cw_ref, cb_ref, wg_ref, bg_ref, lam_ref,
                    a0_ref, u0_ref, a1_ref, u1_ref, ext_ref, *, tc, nctx_t, nt):
    t = pl.program_id(1)
    prev_ok = jnp.logical_and(t != 0, t != nctx_t)
    next_ok = jnp.logical_and(t != nctx_t - 1, t != nt - 1)
    x = x_ref[...]
    ext_ref[0:8, :] = jnp.where(prev_ok, xp_ref[...], 0.0)
    ext_ref[8:8 + tc, :] = x
    ext_ref[8 + tc:16 + tc, :] = jnp.where(next_ok, xn_ref[...], 0.0)
    cw = cw_ref[...]
    xc = (cw[0:1] * ext_ref[6:6 + tc, :] + cw[1:2] * ext_ref[7:7 + tc, :] + cw[2:3] * x
          + cw[3:4] * ext_ref[9:9 + tc, :]) + cb_ref[...]
    gates = _dot(xc.astype(BF16), wg_ref[...]) + bg_ref[...]
    outs = ((a0_ref, u0_ref), (a1_ref, u1_ref))
    for d in range(2):
        r_pre = gates[:, (2 * d) * MIX_W:(2 * d + 1) * MIX_W]
        i_pre = gates[:, (2 * d + 1) * MIX_W:(2 * d + 2) * MIX_W]
        z = -lam_ref[d:d + 1, :]
        softplus = jnp.maximum(z, 0.0) + jnp.log(1.0 + jnp.exp(-jnp.abs(z)))
        log_a = (-LRU_C * softplus) * _sigmoid(r_pre)
        a_ref, u_ref = outs[d]
        a = jnp.exp(log_a)
        a_ref[...] = a
        one_m_a2 = -jnp.tanh(log_a) * (a * a + 1.0)
        u_ref[...] = jnp.sqrt(jnp.maximum(one_m_a2, 0.0)) * (_sigmoid(i_pre) * xc)


def _lru_gates_call(l, lru_x, conv_w, conv_b, wg, bg, lam, lc):
    bsz, tt, w = lru_x.shape
    tc = lc
    nt, nctx_t = tt // tc, lc // tc
    h8 = tc // 8
    out = jax.ShapeDtypeStruct((bsz, tt, w), F32)
    o_spec = pl.BlockSpec((None, tc, w), lambda b, t: (b, t, 0))
    return pl.pallas_call(
        functools.partial(_lru_gates_body, tc=tc, nctx_t=nctx_t, nt=nt),
        grid=(bsz, nt),
        in_specs=[
            pl.BlockSpec((None, tc, w), lambda b, t: (b, t, 0)),
            pl.BlockSpec((None, 8, w), lambda b, t: (b, jnp.maximum(t * h8 - 1, 0), 0)),
            pl.BlockSpec((None, 8, w), lambda b, t: (b, jnp.minimum((t + 1) * h8, tt // 8 - 1), 0)),
            pl.BlockSpec((None, 4, w), lambda b, t: (l, 0, 0)),
            pl.BlockSpec((None, 1, w), lambda b, t: (l, 0, 0)),
            pl.BlockSpec((None, w, 4 * w), lambda b, t: (l, 0, 0)),
            pl.BlockSpec((None, 1, 4 * w), lambda b, t: (l, 0, 0)),
            pl.BlockSpec((None, 2, w), lambda b, t: (l, 0, 0)),
        ],
        out_specs=[o_spec] * 4,
        out_shape=[out] * 4,
        scratch_shapes=[pltpu.VMEM((tc + 16, w), F32)],
        compiler_params=_cparams(("arbitrary", "arbitrary")),
        name="rglru_gates",
    )(lru_x, lru_x, lru_x, conv_w, conv_b, wg, bg, lam)


def _lru_scan_body(a0_ref, u0_ref, a1_ref, u1_ref, hf_ref, hb_ref, sf_ref, sb_ref, *, tc):
    @pl.when(pl.program_id(0) == 0)
    def _():
        sf_ref[...] = jnp.zeros_like(sf_ref)
        sb_ref[...] = jnp.zeros_like(sb_ref)

    def step(i, carry):
        hf, hb = carry
        hf = a0_ref[:, i, :] * hf + u0_ref[:, i, :]
        hf_ref[:, i, :] = hf
        ib = tc - 1 - i
        hb = a1_ref[:, ib, :] * hb + u1_ref[:, ib, :]
        hb_ref[:, ib, :] = hb
        return hf, hb

    hf, hb = lax.fori_loop(0, tc, step, (sf_ref[...], sb_ref[...]), unroll=8)
    sf_ref[...] = hf
    sb_ref[...] = hb


def _lru_scan_call(a0, u0, a1, u1, lc):
    bsz, tt, w = a0.shape
    tc = CHUNK
    nch, nctx = tt // tc, lc // tc
    fw = pl.BlockSpec((bsz, tc, w), lambda j: (0, j, 0))
    bw = pl.BlockSpec((bsz, tc, w), lambda j: (0, _chunk_index(1, j, nctx, nch), 0))
    out = jax.ShapeDtypeStruct((bsz, tt, w), F32)
    return pl.pallas_call(
        functools.partial(_lru_scan_body, tc=tc),
        grid=(nch,),
        in_specs=[fw, fw, bw, bw],
        out_specs=[fw, bw],
        out_shape=[out, out],
        scratch_shapes=[pltpu.VMEM((bsz, w), F32), pltpu.VMEM((bsz, w), F32)],
        compiler_params=_cparams(("arbitrary",)),
        name="rglru_scan",
    )(a0, u0, a1, u1)


def _head_rms(o, w):
    parts = []
    for h in range(N_HEADS):
        oh = o[:, h * HEAD_DV:(h + 1) * HEAD_DV]
        parts.append(oh * lax.rsqrt(jnp.mean(oh * oh, axis=-1, keepdims=True) + EPS))
    return jnp.concatenate(parts, axis=1) * w


def _gelu_tanh(x):
    return 0.5 * x * (1.0 + jnp.tanh(0.7978845608028654 * (x + 0.044715 * (x * x * x))))


def _merge_body(hg0_ref, hg1_ref, ml0_ref, ml1_ref, lf_ref, lb_ref, hgg_ref, mlo_ref, ly_ref, mg_ref,
                ctx_ref, x_ref, gate_ref, hgn_ref, mln_ref, wb_ref, wo_ref, ctx_o_ref, x_o_ref,
                *, t0, d_model):
    t = pl.program_id(1) + t0
    hgg = hgg_ref[...].astype(F32)
    o_hg = hg0_ref[...].astype(F32) + hg1_ref[...].astype(F32)
    o_ml = ml0_ref[...].astype(F32) + ml1_ref[...].astype(F32)
    a_out = _head_rms(o_hg, hgn_ref[...]) * (hgg * _sigmoid(hgg))
    b_out = _head_rms(o_ml, mln_ref[...]) * _sigmoid(mlo_ref[...].astype(F32))
    c_out = (lf_ref[...] + lb_ref[...]) * _gelu_tanh(ly_ref[...].astype(F32))
    merged = None
    for n, br in enumerate((a_out, b_out, c_out)):
        gate = _sigmoid(mg_ref[:, n * d_model:(n + 1) * d_model].astype(F32))
        term = gate * _dot(br.astype(BF16), wb_ref[n])
        merged = term if merged is None else merged + term
    y = gate_ref[...] * _dot(merged.astype(BF16), wo_ref[...])

    if t0 == 0:
        @pl.when(t == 0)
        def _():
            ctx_o_ref[...] = ctx_ref[...] + y
    else:
        @pl.when(t == t0)
        def _():
            ctx_o_ref[...] = ctx_ref[...]

    @pl.when(t > 0)
    def _():
        x_o_ref[...] = x_ref[...] + y


def _merge_call(l, last, o_hg, o_ml, h_f, h_b, hg_a, ml_a, lru_y, mg, ctx, x, mods4,
                hg_norm, ml_norm, w_branch, w_out):
    bsz, lc, d_model = ctx.shape
    seq = x.shape[1]
    tm = lc
    t0 = 1 if last else 0
    nt = 1 + seq // tm - t0

    def tok(b, t):
        return t + t0

    def xi(t):
        return jnp.maximum(t + t0 - 1, 0)

    mix = lambda d: pl.BlockSpec((None, None, tm, MIX_W), lambda b, t: (b, d, tok(b, t), 0))
    lru = pl.BlockSpec((None, tm, MIX_W), lambda b, t: (b, tok(b, t), 0))
    in_specs = [
        mix(0), mix(1), mix(0), mix(1), lru, lru,
        pl.BlockSpec((None, tm, MIX_W), lambda b, t: (b, tok(b, t), 2)),
        pl.BlockSpec((None, tm, MIX_W), lambda b, t: (b, tok(b, t), 2)),
        pl.BlockSpec((None, tm, MIX_W), lambda b, t: (b, tok(b, t), 0)),
        pl.BlockSpec((None, tm, 3 * d_model), lambda b, t: (b, tok(b, t), 0)),
        pl.BlockSpec((None, tm, d_model), lambda b, t: (b, 0, 0)),
        pl.BlockSpec((None, tm, d_model), lambda b, t: (b, xi(t), 0)),
        pl.BlockSpec((None, None, 1, d_model), lambda b, t: (l, jnp.where(t + t0 == 0, bsz, b), 0, 2)),
        pl.BlockSpec((None, 1, MIX_W), lambda b, t: (l, 0, 0)),
        pl.BlockSpec((None, 1, MIX_W), lambda b, t: (l, 0, 0)),
        pl.BlockSpec((None, 3, MIX_W, d_model), lambda b, t: (l, 0, 0, 0)),
        pl.BlockSpec((None, d_model, d_model), lambda b, t: (l, 0, 0)),
    ]
    out_specs = [
        pl.BlockSpec((None, tm, d_model), lambda b, t: (b, 0, 0)),
        pl.BlockSpec((None, tm, d_model), lambda b, t: (b, xi(t), 0)),
    ]
    out_shape = [jax.ShapeDtypeStruct(ctx.shape, F32), jax.ShapeDtypeStruct(x.shape, F32)]
    return pl.pallas_call(
        functools.partial(_merge_body, t0=t0, d_model=d_model),
        grid=(bsz, nt),
        in_specs=in_specs,
        out_specs=out_specs,
        out_shape=out_shape,
        compiler_params=_cparams(("arbitrary", "arbitrary")),
        name="merge",
    )(o_hg, o_hg, o_ml, o_ml, h_f, h_b, hg_a, ml_a, lru_y, mg, ctx, x, mods4,
      hg_norm, ml_norm, w_branch, w_out)


def _ffn_body(x_ref, sh_ref, sc_ref, gt_ref, ln_ref, wg_ref, wu_ref, wo_ref, fn_ref, o_ref,
              h_ref, acc_ref, *, n_k, n_inner, final_norm):
    k = pl.program_id(2)

    @pl.when(k == 0)
    def _():
        h_ref[...] = _norm_mod(x_ref[...], ln_ref[...], sh_ref[...], sc_ref[...]).astype(BF16)
        acc_ref[...] = jnp.zeros_like(acc_ref)

    h = h_ref[...]
    gate = _dot(h, wg_ref[...])
    up = _dot(h, wu_ref[...])
    act = (gate * _sigmoid(gate) * up).astype(BF16)
    acc_ref[...] += _dot(act, wo_ref[...])

    @pl.when(k == n_k - 1)
    def _():
        y = x_ref[...] + gt_ref[...] * acc_ref[...]
        if final_norm:
            y = y * lax.rsqrt(jnp.mean(y * y, axis=-1, keepdims=True) + EPS) * fn_ref[...]
        if n_inner is None:
            o_ref[...] = y
        else:
            for a in range(y.shape[0] // n_inner):
                o_ref[:, a, :] = y[a * n_inner:(a + 1) * n_inner, :]


def _ffn_call(l, x, mods4, ctx_row, ln2, wg, wu, wo, fin, *, tm, n_inner, final_norm):
    bsz, n, d_model = x.shape
    n_k, _, tk = wg.shape[1:]
    nt = n // tm

    def mod_spec(chunk):
        return pl.BlockSpec((None, None, 1, d_model),
                            lambda b, t, k: (l, bsz if ctx_row else b, 0, chunk))

    if n_inner is None:
        o_spec = pl.BlockSpec((None, tm, d_model), lambda b, t, k: (b, t, 0))
        o_shape = jax.ShapeDtypeStruct(x.shape, F32)
    else:
        n_a = tm // n_inner
        o_spec = pl.BlockSpec((None, n_inner, n_a, d_model), lambda b, t, k: (b, 0, t, 0))
        o_shape = jax.ShapeDtypeStruct((bsz, n_inner, n // n_inner, d_model), F32)
    out = pl.pallas_call(
        functools.partial(_ffn_body, n_k=n_k, n_inner=n_inner, final_norm=final_norm),
        grid=(bsz, nt, n_k),
        in_specs=[
            pl.BlockSpec((None, tm, d_model), lambda b, t, k: (b, t, 0)),
            mod_spec(3), mod_spec(4), mod_spec(5),
            pl.BlockSpec((None, 1, d_model), lambda b, t, k: (l, 0, 0)),
            pl.BlockSpec((None, None, d_model, tk), lambda b, t, k: (l, k, 0, 0)),
            pl.BlockSpec((None, None, d_model, tk), lambda b, t, k: (l, k, 0, 0)),
            pl.BlockSpec((None, None, tk, d_model), lambda b, t, k: (l, k, 0, 0)),
            pl.BlockSpec((1, d_model), lambda b, t, k: (0, 0)),
        ],
        out_specs=o_spec,
        out_shape=o_shape,
        scratch_shapes=[pltpu.VMEM((tm, d_model), BF16), pltpu.VMEM((tm, d_model), F32)],
        compiler_params=_cparams(("arbitrary", "arbitrary", "arbitrary")),
        name="ffn",
    )(x, mods4, mods4, mods4, ln2, wg, wu, wo, fin)
    return out.reshape(bsz, n, d_model)


def _prep_in_proj(w_in, b_in):
    depth, d_model, _ = w_in.shape
    hq = N_HEADS * HG_DK
    sizes = (hq, MIX_W, MIX_W, 2 * hq, N_HEADS * ML_DQK, N_HEADS * ML_DQK, MIX_W, MIX_W,
             2 * N_HEADS, 2 * N_HEADS, MIX_W, MIX_W, 3 * d_model)
    offs = np.concatenate([[0], np.cumsum(sizes)])
    assert offs[-1] == w_in.shape[-1]

    def cols(a, lo, hi):
        return a[..., int(offs[lo]):int(offs[hi])]

    def gate_block(a):
        ig = cols(a, 8, 9).reshape(a.shape[:-1] + (2, N_HEADS))
        fg = cols(a, 9, 10).reshape(a.shape[:-1] + (2, N_HEADS))
        pad = jnp.zeros(a.shape[:-1] + (2, GATE_LANES - 2 * N_HEADS), a.dtype)
        return jnp.concatenate([ig, fg, pad], axis=-1).reshape(a.shape[:-1] + (2 * GATE_LANES,))

    groups = [
        (lambda a: cols(a, 0, 3), BF16),
        (lambda a: cols(a, 3, 4), F32),
        (lambda a: cols(a, 4, 8), BF16),
        (gate_block, F32),
        (lambda a: cols(a, 10, 11), F32),
        (lambda a: cols(a, 11, 12), BF16),
        (lambda a: cols(a, 12, 13), BF16),
    ]
    kinds = ("hgrn2_qig", "hgrn2_forget", None, "mlstm_gates", None, None, None)
    b3 = b_in.reshape(depth, 1, -1)
    ws = [f(w_in).astype(BF16) for f, _ in groups]
    bs = [f(b3) for f, _ in groups]
    return ws, bs, [dt for _, dt in groups], kinds


def _prep_lru_gates(lru_gate_w, lru_gate_b):
    depth = lru_gate_w.shape[0]
    eye = jnp.eye(LRU_BLOCKS, dtype=lru_gate_w.dtype)
    dense = jnp.einsum("lzgnde,nm->lndzgme", lru_gate_w, eye)
    dense = dense.reshape(depth, MIX_W, 4 * MIX_W).astype(BF16)
    return dense, lru_gate_b.reshape(depth, 1, 4 * MIX_W)


def _prep_ffn(w_ffn_in, w_ffn_out):
    depth, d_model, two_h = w_ffn_in.shape
    hidden = two_h // 2
    tk = hidden // 2 if (hidden // 2) % 128 == 0 else hidden
    n_k = hidden // tk
    wi = w_ffn_in.astype(BF16).reshape(depth, d_model, 2, n_k, tk)
    wg = jnp.transpose(wi[:, :, 0], (0, 2, 1, 3))
    wu = jnp.transpose(wi[:, :, 1], (0, 2, 1, 3))
    wo = w_ffn_out.astype(BF16).reshape(depth, n_k, tk, d_model)
    return wg, wu, wo


def kernel(x, c, ctx, c_ctx, w_ada, b_ada, ln1, w_in, b_in, hg_lb_raw, hg_norm, ml_norm, conv_w, conv_b,
           lru_gate_w, lru_gate_b, lru_lambda, w_branch, w_out, ln2, w_ffn_in, w_ffn_out, final_norm):
    bsz, seq, d_model = x.shape
    lc = ctx.shape[1]
    depth = w_ada.shape[0]
    rows = seq // GRID_W
    assert lc % CHUNK == 0 and seq % lc == 0 and seq % FFN_TM == 0 and bsz < 16

    c_all = jnp.zeros((16, d_model), F32).at[:bsz].set(c).at[bsz].set(c_ctx)
    mods4 = _ada_call(c_all, w_ada, b_ada).reshape(depth, 16, 1, 6 * d_model)
    lb_all = _lb_call(hg_lb_raw)
    lb3 = lb_all.reshape(depth, 1, 2 * MIX_W)
    lb4 = lb_all.reshape(depth, 2, 1, MIX_W)

    ws, bs, out_dtypes, kinds = _prep_in_proj(w_in, b_in)
    wgate, bgate = _prep_lru_gates(lru_gate_w, lru_gate_b)
    wg, wu, wo = _prep_ffn(w_ffn_in, w_ffn_out)
    w_branch_b = w_branch.astype(BF16)
    w_out_b = w_out.astype(BF16)
    ln1_3 = ln1.reshape(depth, 1, d_model)
    ln2_3 = ln2.reshape(depth, 1, d_model)
    hgn = hg_norm.reshape(depth, 1, MIX_W)
    mln = ml_norm.reshape(depth, 1, MIX_W)
    conv_b3 = conv_b.reshape(depth, 1, MIX_W)
    fin = final_norm.reshape(1, d_model)

    for l in range(depth):
        last = l == depth - 1
        n_inner = GRID_W if l % 2 == 0 else rows
        hg_a, hg_f, ml_a, ml_g, lru_x, lru_y, mg = _in_proj_call(l, ctx, x, mods4, ln1_3, lb3, ws, bs,
                                                                 out_dtypes, kinds)
        o_hg, o_ml = _mixers_call(l, hg_a, hg_f, lb4, ml_a, ml_g, lc)
        a0, u0, a1, u1 = _lru_gates_call(l, lru_x, conv_w, conv_b3, wgate, bgate, lru_lambda, lc)
        h_f, h_b = _lru_scan_call(a0, u0, a1, u1, lc)
        ctx_m, x_m = _merge_call(l, last, o_hg, o_ml, h_f, h_b, hg_a, ml_a, lru_y, mg, ctx, x, mods4,
                                 hgn, mln, w_branch_b, w_out_b)
        x = _ffn_call(l, x_m, mods4, False, ln2_3, wg, wu, wo, fin,
                      tm=FFN_TM, n_inner=n_inner, final_norm=last)
        if not last:
            ctx = _ffn_call(l, ctx_m, mods4, True, ln2_3, wg, wu, wo, fin,
                            tm=lc, n_inner=None, final_norm=False)
    return x
```

```python
import functools
import itertools

import numpy as np
import jax
import jax.numpy as jnp
from jax import lax
from jax.experimental import pallas as pl
from jax.experimental.pallas import tpu as pltpu

F32 = jnp.float32
BF16 = jnp.bfloat16

GRID_W = 64
MIX_W = 512
N_HEADS = 4
HEAD_DV = 128
HG_DK = 128
ML_DQK = 64
LRU_BLOCKS = 8
LRU_BD = 64
LRU_C = 8.0
EPS = 1e-6
NEG_BIG = -1e30
LB_TINY = 1e-30
LOG2_E = 1.4426950408889634
GATE_LANES = 128
CHUNK = 128
MIX_NB = 2
FFN_TM = 512
FFN_ROWS = 256
IN_PROJ_ROWS = 128
VMEM_LIMIT = 56 * 1024 * 1024


def _cparams(sem):
    return pltpu.CompilerParams(dimension_semantics=sem, vmem_limit_bytes=VMEM_LIMIT)


def _sigmoid(x):
    return 1.0 / (1.0 + jnp.exp(-x))


def _sigmoid_t(x):
    return 0.5 * jnp.tanh(0.5 * x) + 0.5


def _log_sigmoid(x):
    return jnp.minimum(x, 0.0) - jnp.log(1.0 + jnp.exp(-jnp.abs(x)))


def _dot(a, b):
    return jnp.dot(a, b, preferred_element_type=F32)


def _dot_nt(a, b):
    return lax.dot_general(a, b, (((1,), (1,)), ((), ())), preferred_element_type=F32)


def _dot_tn(a, b):
    return lax.dot_general(a, b, (((0,), (0,)), ((), ())), preferred_element_type=F32)


def _split_hi_lo(x):
    hi = x.astype(BF16)
    lo = (x - hi.astype(F32)).astype(BF16)
    return hi, lo


def _norm_mod(x, ln, shift, scale):
    y = x * lax.rsqrt(jnp.mean(x * x, axis=-1, keepdims=True) + EPS) * ln
    return y * (1.0 + scale) + shift


def _mirror(m):
    return m[..., ::-1, ::-1].copy()


@functools.lru_cache(maxsize=None)
def _hgrn2_consts(C):
    n_lv = int(np.log2(C))
    mats, masks = [], []
    r = np.arange(C)
    for lv in range(n_lv):
        s = C >> (lv + 1)
        base = (r // (2 * s)) * (2 * s)
        mid = base + s - 1
        odd = r >= base + s
        m = np.zeros((C, C), np.float32)
        for t in range(C):
            if odd[t]:
                m[t, mid[t] + 1:t + 1] = 1.0
            else:
                m[t, t + 1:mid[t] + 1] = 1.0
        mats.append(m)
        same = base[:, None] == base[None, :]
        masks.append((same & odd[:, None] & (~odd)[None, :]).astype(np.float32))
    mats.append(np.tril(np.ones((C, C), np.float32)))
    masks.append(np.eye(C, dtype=np.float32))
    mats = np.stack(mats)
    masks = np.stack(masks)
    mats = np.stack([mats, _mirror(mats)])
    masks = np.stack([masks, _mirror(masks)])
    mst = mats.reshape(2, (n_lv + 1) * C, C)
    mst = np.concatenate([mst, mst], axis=-1)
    return mst, masks, n_lv


@functools.lru_cache(maxsize=None)
def _mlstm_consts(C):
    tri = np.tril(np.ones((C, C), np.float32))
    tri = np.stack([tri, _mirror(tri)])
    tri_cat = np.concatenate([tri, tri], axis=-1)
    tri_t = np.transpose(tri, (0, 2, 1))
    tri_t_cat = np.concatenate([tri_t, tri_t], axis=1)
    return tri_cat, tri_t_cat, tri


def _chunk_index(d, j, nctx, nch):
    bw = jnp.where(j < nctx, nctx - 1 - j, nch - 1 + nctx - j)
    return jnp.where(d == 0, j, bw)


def _ada_body(c_ref, w_ref, b_ref, o_ref):
    cc = c_ref[...]
    s = cc * _sigmoid(cc)
    o_ref[...] = jnp.dot(s, w_ref[...], preferred_element_type=F32,
                         precision=lax.Precision.HIGHEST) + b_ref[...]


def _ada_call(c_all, w_ada, b_ada):
    depth, d_model, n6 = w_ada.shape
    rows = c_all.shape[0]
    tn = 1536
    return pl.pallas_call(
        _ada_body,
        grid=(depth, n6 // tn),
        in_specs=[
            pl.BlockSpec((rows, d_model), lambda l, n: (0, 0)),
            pl.BlockSpec((None, d_model, tn), lambda l, n: (l, 0, n)),
            pl.BlockSpec((None, 1, tn), lambda l, n: (l, 0, n)),
        ],
        out_specs=pl.BlockSpec((None, rows, tn), lambda l, n: (l, 0, n)),
        out_shape=jax.ShapeDtypeStruct((depth, rows, n6), F32),
        compiler_params=_cparams(("arbitrary", "arbitrary")),
        name="ada_mod",
    )(c_all, w_ada, b_ada.reshape(depth, 1, n6))


def _lb_body(raw_ref, o_ref):
    raw = raw_ref[...]
    depth = raw.shape[0]
    e = jnp.exp(raw - jnp.max(raw, axis=0, keepdims=True))
    p = e / jnp.sum(e, axis=0, keepdims=True)
    acc = jnp.zeros_like(p[0:1])
    for l in range(depth):
        acc = acc + p[l:l + 1]
        o_ref[l:l + 1, :] = acc - p[0:1]


def _lb_call(hg_lb_raw):
    depth = hg_lb_raw.shape[0]
    raw = hg_lb_raw.reshape(depth, -1)
    return pl.pallas_call(
        _lb_body,
        out_shape=jax.ShapeDtypeStruct(raw.shape, F32),
        name="hgrn2_lower_bounds",
    )(raw)


def _log2_forget(f_pre, lb):
    e = jnp.exp(-jnp.abs(f_pre))
    inv = 1.0 / (1.0 + e)
    sig = jnp.where(f_pre >= 0.0, inv, e * inv)
    return jnp.log(jnp.maximum(lb, LB_TINY) + (1.0 - lb) * sig) * LOG2_E


def _in_proj_body(ctx_ref, x_ref, sh_ref, sc_ref, ln_ref, lb_ref, *rest, kinds):
    n_out = len(kinds)
    w_refs = rest[:n_out]
    b_refs = rest[n_out:2 * n_out]
    o_refs = rest[2 * n_out:]
    t = pl.program_id(1)
    tm = x_ref.shape[0]
    rows = min(tm, IN_PROJ_ROWS)

    def finish(kind, y, o_ref, rs):
        if kind == "hgrn2_qig":
            q = y[:, :MIX_W]
            o_ref[rs, :MIX_W] = (q * _sigmoid(q)).astype(o_ref.dtype)
            o_ref[rs, MIX_W:] = y[:, MIX_W:].astype(o_ref.dtype)
        elif kind == "hgrn2_forget":
            o_ref[rs, :] = _log2_forget(y, lb_ref[...]).astype(o_ref.dtype)
        elif kind == "mlstm_gates":
            lane = lax.broadcasted_iota(jnp.int32, y.shape, 1) % GATE_LANES
            is_f = jnp.logical_and(lane >= N_HEADS, lane < 2 * N_HEADS)
            o_ref[rs, :] = jnp.where(is_f, _log_sigmoid(y), y).astype(o_ref.dtype)
        else:
            o_ref[rs, :] = y.astype(o_ref.dtype)

    def row_group(r0):
        rs = slice(r0, r0 + rows)
        xin = jnp.where(t == 0, ctx_ref[rs, :], x_ref[rs, :])
        h = _norm_mod(xin, ln_ref[...], sh_ref[...], sc_ref[...]).astype(BF16)
        yield
        for kind, w_ref, b_ref, o_ref in zip(kinds, w_refs, b_refs, o_refs):
            finish(kind, _dot(h, w_ref[...]) + b_ref[...], o_ref, rs)
            yield

    gens = [row_group(r0) for r0 in range(0, tm, rows)]
    n_phase = 1 + n_out
    for step in range(n_phase + len(gens) - 1):
        for i, g in enumerate(gens):
            if 0 <= step - i < n_phase:
                next(g)


def _in_proj_call(l, ctx, x, mods4, ln1, lb3, ws, bs, out_dtypes, kinds):
    bsz, lc, d_model = ctx.shape
    seq = x.shape[1]
    tm = lc
    nt = 1 + seq // tm
    n_out = len(ws)

    def mod_spec(chunk):
        return pl.BlockSpec((None, None, 1, d_model),
                            lambda b, t: (l, jnp.where(t == 0, bsz, b), 0, chunk))

    in_specs = [
        pl.BlockSpec((None, tm, d_model), lambda b, t: (b, 0, 0)),
        pl.BlockSpec((None, tm, d_model), lambda b, t: (b, jnp.maximum(t - 1, 0), 0)),
        mod_spec(0), mod_spec(1),
        pl.BlockSpec((None, 1, d_model), lambda b, t: (l, 0, 0)),
        pl.BlockSpec((None, 1, lb3.shape[-1]), lambda b, t: (l, 0, 0)),
    ]
    in_specs += [pl.BlockSpec((None, d_model, w.shape[-1]), lambda b, t: (l, 0, 0)) for w in ws]
    in_specs += [pl.BlockSpec((None, 1, w.shape[-1]), lambda b, t: (l, 0, 0)) for w in ws]
    out_specs = [pl.BlockSpec((None, tm, w.shape[-1]), lambda b, t: (b, t, 0)) for w in ws]
    out_shape = [jax.ShapeDtypeStruct((bsz, lc + seq, w.shape[-1]), dt) for w, dt in zip(ws, out_dtypes)]
    return pl.pallas_call(
        functools.partial(_in_proj_body, kinds=kinds),
        grid=(bsz, nt),
        in_specs=in_specs,
        out_specs=out_specs,
        out_shape=out_shape,
        compiler_params=_cparams(("arbitrary", "arbitrary")),
        name="in_proj",
    )(ctx, x, mods4, mods4, ln1, lb3, *ws, *bs)


def _hgrn2_body(qv_ref, g_ref, lb_ref, mst_ref, msk_ref, o_ref, st_ref, *, C, n_lv):
    q16 = qv_ref[:, :MIX_W]
    v = qv_ref[:, MIX_W:]
    q = q16.astype(F32)
    g = g_ref[...]
    lb = lb_ref[...]
    kk = (1.0 - jnp.exp2(g)) + (jnp.maximum(lb, LB_TINY) - lb)
    g_hi, g_lo = _split_hi_lo(g)
    ex = _dot(mst_ref[...], jnp.concatenate([g_hi, g_lo], axis=0))
    tot = jnp.sum(g, axis=0, keepdims=True)
    k16 = kk.astype(BF16)

    for h in range(N_HEADS):
        sl = slice(h * HG_DK, (h + 1) * HG_DK)
        qh = q16[:, sl]
        kh = k16[:, sl]
        p = msk_ref[n_lv] * _dot_nt(qh, kh)
        for lv in range(n_lv):
            w = jnp.exp2(ex[lv * C:(lv + 1) * C, sl]).astype(BF16)
            p = p + msk_ref[lv] * _dot_nt(qh * w, kh * w)
            if lv % 2 == 1:
                yield
        st = st_ref[h]
        b_in = ex[n_lv * C:(n_lv + 1) * C, sl]
        qb = (q[:, sl] * jnp.exp2(b_in)).astype(BF16)
        vh = v[:, sl]
        o_ref[:, sl] = (_dot(p.astype(BF16), vh) + _dot_nt(qb, st.astype(BF16))).astype(o_ref.dtype)
        tot_h = tot[:, sl]
        kb = (kk[:, sl] * jnp.exp2(tot_h - b_in)).astype(BF16)
        st_ref[h] = st * jnp.exp2(tot_h) + _dot_tn(vh, kb)
        yield


def _mlstm_body(qkv_ref, g_ref, tri_ref, trit_ref, msk_ref, o_ref, c_ref, m_ref, *, C):
    qk_w = N_HEADS * ML_DQK
    gates = g_ref[...]
    gates_t = gates.T
    lf = gates
    lf_t = gates_t
    lf_hi, lf_lo = _split_hi_lo(lf)
    b_cols = _dot(tri_ref[...], jnp.concatenate([lf_hi, lf_lo], axis=0))
    lft_hi, lft_lo = _split_hi_lo(lf_t)
    b_rows = _dot(jnp.concatenate([lft_hi, lft_lo], axis=1), trit_ref[...])
    tot = jnp.sum(lf, axis=0, keepdims=True)
    allowed = msk_ref[...] > 0.0
    ones = jnp.ones((C, HEAD_DV), BF16)
    scale = ML_DQK ** -0.5

    for h in range(N_HEADS):
        b_col = b_cols[:, N_HEADS + h:N_HEADS + h + 1]
        b_row = b_rows[N_HEADS + h:N_HEADS + h + 1, :]
        i_row = gates_t[h:h + 1, :]
        i_col = gates[:, h:h + 1]
        m_prev = m_ref[h][0:1, 0:1]
        log_d = jnp.where(allowed, b_col - b_row + i_row, NEG_BIG)
        log_inter = b_col + m_prev
        m_row = jnp.maximum(jnp.max(log_d, axis=-1, keepdims=True), log_inter)
        yield
        qh = qkv_ref[:, h * ML_DQK:(h + 1) * ML_DQK]
        kh = qkv_ref[:, qk_w + h * ML_DQK:qk_w + (h + 1) * ML_DQK]
        vh = qkv_ref[:, 2 * qk_w + h * HEAD_DV:2 * qk_w + (h + 1) * HEAD_DV]
        vaug = jnp.concatenate([vh, ones], axis=1)
        s = (_dot_nt(qh, kh) * scale) * jnp.exp(log_d - m_row)
        w_inter = jnp.exp(log_inter - m_row)
        yield
        c_aug = c_ref[h]
        r = _dot(s.astype(BF16), vaug) + (w_inter * scale) * _dot(qh, c_aug.astype(BF16))
        num = r[:, :HEAD_DV]
        den = r[:, HEAD_DV:]
        hout = num / jnp.maximum(jnp.abs(den), jnp.exp(-m_row))
        o_ref[:, h * HEAD_DV:(h + 1) * HEAD_DV] = hout.astype(o_ref.dtype)
        yield
        b_last = tot[:, N_HEADS + h:N_HEADS + h + 1]
        log_w = b_last - b_col + i_col
        m_new = jnp.maximum(b_last + m_prev, jnp.max(log_w, axis=0, keepdims=True))
        w_s = jnp.exp(log_w - m_new)
        decay = jnp.exp(b_last + m_prev - m_new)
        kw = (kh.astype(F32) * w_s).astype(BF16)
        c_ref[h] = decay * c_aug + _dot_tn(kw, vaug)
        m_ref[h] = jnp.broadcast_to(m_new, m_ref.shape[1:])
        yield


def _mixers_body(hqv_ref, hg_ref, lb_ref, mst_ref, hmsk_ref, mqkv_ref, mg_ref,
                 tri_ref, trit_ref, mmsk_ref, o_hg_ref, o_ml_ref, st_ref, c_ref, m_ref, *, C, n_lv):
    @pl.when(pl.program_id(2) == 0)
    def _():
        st_ref[...] = jnp.zeros_like(st_ref)
        c_ref[...] = jnp.zeros_like(c_ref)
        m_ref[...] = jnp.zeros_like(m_ref)

    for bb in range(MIX_NB):
        gens = [_hgrn2_body(hqv_ref.at[bb], hg_ref.at[bb], lb_ref, mst_ref, hmsk_ref,
                            o_hg_ref.at[bb], st_ref.at[bb], C=C, n_lv=n_lv),
                _mlstm_body(mqkv_ref.at[bb], mg_ref.at[bb], tri_ref, trit_ref,
                            mmsk_ref, o_ml_ref.at[bb], c_ref.at[bb], m_ref.at[bb], C=C)]
        for _ in itertools.zip_longest(*gens):
            pass


def _mixers_call(l, hg_a, hg_f, lb4, ml_a, ml_g, lc):
    bsz, tt, _ = hg_a.shape
    C = CHUNK
    nch, nctx = tt // C, lc // C
    mst, hmsk, n_lv = _hgrn2_consts(C)
    mst = jnp.asarray(mst, BF16)
    hmsk = jnp.asarray(hmsk, F32)
    tri_cat, tri_t_cat, tri = _mlstm_consts(C)
    tri_cat = jnp.asarray(tri_cat, BF16)
    tri_t_cat = jnp.asarray(tri_t_cat, BF16)
    mmsk = jnp.asarray(tri, F32)
    cidx = functools.partial(_chunk_index, nctx=nctx, nch=nch)
    qk_w = N_HEADS * ML_DQK
    tok = lambda col: (lambda b, d, j: (b, cidx(d, j), col))
    tok_d = lambda b, d, j: (b, cidx(d, j), d)
    per_dir3 = lambda b, d, j: (d, 0, 0)
    nb = MIX_NB
    assert bsz % nb == 0
    o_spec = pl.BlockSpec((nb, None, C, MIX_W), lambda b, d, j: (b, d, cidx(d, j), 0))
    o_shape = jax.ShapeDtypeStruct((bsz, 2, tt, MIX_W), BF16)
    return pl.pallas_call(
        functools.partial(_mixers_body, C=C, n_lv=n_lv),
        grid=(bsz // nb, 2, nch),
        in_specs=[
            pl.BlockSpec((nb, C, 2 * MIX_W), tok(0)),
            pl.BlockSpec((nb, C, MIX_W), tok_d),
            pl.BlockSpec((None, None, 1, MIX_W), lambda b, d, j: (l, d, 0, 0)),
            pl.BlockSpec((None,) + mst.shape[1:], per_dir3),
            pl.BlockSpec((None,) + hmsk.shape[1:], lambda b, d, j: (d, 0, 0, 0)),
            pl.BlockSpec((nb, C, 2 * qk_w + MIX_W), tok(0)),
            pl.BlockSpec((nb, C, GATE_LANES), tok_d),
            pl.BlockSpec((None, C, 2 * C), per_dir3),
            pl.BlockSpec((None, 2 * C, C), per_dir3),
            pl.BlockSpec((None, C, C), per_dir3),
        ],
        out_specs=[o_spec, o_spec],
        out_shape=[o_shape, o_shape],
        scratch_shapes=[pltpu.VMEM((nb, N_HEADS, HEAD_DV, HG_DK), F32),
                        pltpu.VMEM((nb, N_HEADS, ML_DQK, 2 * HEAD_DV), F32),
                        pltpu.VMEM((nb, N_HEADS, 8, 128), F32)],
        compiler_params=_cparams(("arbitrary", "arbitrary", "arbitrary")),
        name="mixers",
    )(hg_a, hg_f, lb4, mst, hmsk, ml_a, ml_g, tri_cat, tri_t_cat, mmsk)


def _lru_gates_body(x_ref, xp_ref, xn_ref, cw_ref, cb_ref, wg_ref, bg_ref, lam_ref,
                    a0_ref, u0_ref, a1_ref, u1_ref, ext_ref, *, tc, nctx_t, nt):
    t = pl.program_id(1)
    prev_ok = jnp.logical_and(t != 0, t != nctx_t)
    next_ok = jnp.logical_and(t != nctx_t - 1, t != nt - 1)
    x = x_ref[...]
    ext_ref[0:8, :] = jnp.where(prev_ok, xp_ref[...], 0.0)
    ext_ref[8:8 + tc, :] = x
    ext_ref[8 + tc:16 + tc, :] = jnp.where(next_ok, xn_ref[...], 0.0)
    cw = cw_ref[...]
    xc = (cw[0:1] * ext_ref[6:6 + tc, :] + cw[1:2] * ext_ref[7:7 + tc, :] + cw[2:3] * x
          + cw[3:4] * ext_ref[9:9 + tc, :]) + cb_ref[...]
    gates = _dot(xc.astype(BF16), wg_ref[...]) + bg_ref[...]
    outs = ((a0_ref, u0_ref), (a1_ref, u1_ref))
    for d in range(2):
        r_pre = gates[:, (2 * d) * MIX_W:(2 * d + 1) * MIX_W]
        i_pre = gates[:, (2 * d + 1) * MIX_W:(2 * d + 2) * MIX_W]
        z = -lam_ref[d:d + 1, :]
        softplus = jnp.maximum(z, 0.0) + jnp.log(1.0 + jnp.exp(-jnp.abs(z)))
        log_a = (-LRU_C * softplus) * _sigmoid_t(r_pre)
        a_ref, u_ref = outs[d]
        a = jnp.exp(log_a)
        a_ref[...] = a
        one_m_a2 = -jnp.tanh(log_a) * (a * a + 1.0)
        u_ref[...] = jnp.sqrt(jnp.maximum(one_m_a2, 0.0)) * (_sigmoid_t(i_pre) * xc)


def _lru_gates_call(l, lru_x, conv_w, conv_b, wg, bg, lam, lc):
    bsz, tt, w = lru_x.shape
    tc = lc
    nt, nctx_t = tt // tc, lc // tc
    h8 = tc // 8
    out = jax.ShapeDtypeStruct((bsz, tt, w), F32)
    o_spec = pl.BlockSpec((None, tc, w), lambda b, t: (b, t, 0))
    return pl.pallas_call(
        functools.partial(_lru_gates_body, tc=tc, nctx_t=nctx_t, nt=nt),
        grid=(bsz, nt),
        in_specs=[
            pl.BlockSpec((None, tc, w), lambda b, t: (b, t, 0)),
            pl.BlockSpec((None, 8, w), lambda b, t: (b, jnp.maximum(t * h8 - 1, 0), 0)),
            pl.BlockSpec((None, 8, w), lambda b, t: (b, jnp.minimum((t + 1) * h8, tt // 8 - 1), 0)),
            pl.BlockSpec((None, 4, w), lambda b, t: (l, 0, 0)),
            pl.BlockSpec((None, 1, w), lambda b, t: (l, 0, 0)),
            pl.BlockSpec((None, w, 4 * w), lambda b, t: (l, 0, 0)),
            pl.BlockSpec((None, 1, 4 * w), lambda b, t: (l, 0, 0)),
            pl.BlockSpec((None, 2, w), lambda b, t: (l, 0, 0)),
        ],
        out_specs=[o_spec] * 4,
        out_shape=[out] * 4,
        scratch_shapes=[pltpu.VMEM((tc + 16, w), F32)],
        compiler_params=_cparams(("arbitrary", "arbitrary")),
        name="rglru_gates",
    )(lru_x, lru_x, lru_x, conv_w, conv_b, wg, bg, lam)


def _lru_scan_body(a0_ref, u0_ref, a1_ref, u1_ref, hf_ref, hb_ref, sf_ref, sb_ref, *, tc):
    @pl.when(pl.program_id(0) == 0)
    def _():
        sf_ref[...] = jnp.zeros_like(sf_ref)
        sb_ref[...] = jnp.zeros_like(sb_ref)

    def step(i, carry):
        hf, hb = carry
        hf = a0_ref[:, i, :] * hf + u0_ref[:, i, :]
        hf_ref[:, i, :] = hf
        ib = tc - 1 - i
        hb = a1_ref[:, ib, :] * hb + u1_ref[:, ib, :]
        hb_ref[:, ib, :] = hb
        return hf, hb

    hf, hb = lax.fori_loop(0, tc, step, (sf_ref[...], sb_ref[...]), unroll=8)
    sf_ref[...] = hf
    sb_ref[...] = hb


def _lru_scan_call(a0, u0, a1, u1, lc):
    bsz, tt, w = a0.shape
    tc = CHUNK
    nch, nctx = tt // tc, lc // tc
    fw = pl.BlockSpec((bsz, tc, w), lambda j: (0, j, 0))
    bw = pl.BlockSpec((bsz, tc, w), lambda j: (0, _chunk_index(1, j, nctx, nch), 0))
    out = jax.ShapeDtypeStruct((bsz, tt, w), F32)
    return pl.pallas_call(
        functools.partial(_lru_scan_body, tc=tc),
        grid=(nch,),
        in_specs=[fw, fw, bw, bw],
        out_specs=[fw, bw],
        out_shape=[out, out],
        scratch_shapes=[pltpu.VMEM((bsz, w), F32), pltpu.VMEM((bsz, w), F32)],
        compiler_params=_cparams(("arbitrary",)),
        name="rglru_scan",
    )(a0, u0, a1, u1)


def _head_rms(o, w):
    parts = []
    for h in range(N_HEADS):
        oh = o[:, h * HEAD_DV:(h + 1) * HEAD_DV]
        parts.append(oh * lax.rsqrt(jnp.mean(oh * oh, axis=-1, keepdims=True) + EPS))
    return jnp.concatenate(parts, axis=1) * w


def _gelu_tanh(x):
    return 0.5 * x * (1.0 + jnp.tanh(0.7978845608028654 * (x + 0.044715 * (x * x * x))))


def _merge_body(hg0_ref, hg1_ref, ml0_ref, ml1_ref, lf_ref, lb_ref, hgg_ref, mlo_ref, ly_ref, mg_ref,
                ctx_ref, x_ref, gate_ref, hgn_ref, mln_ref, wb_ref, wo_ref, ctx_o_ref, x_o_ref,
                *, t0, d_model):
    t = pl.program_id(1) + t0
    hgg = hgg_ref[...].astype(F32)
    o_hg = hg0_ref[...].astype(F32) + hg1_ref[...].astype(F32)
    o_ml = ml0_ref[...].astype(F32) + ml1_ref[...].astype(F32)
    a_out = _head_rms(o_hg, hgn_ref[...]) * (hgg * _sigmoid_t(hgg))
    b_out = _head_rms(o_ml, mln_ref[...]) * _sigmoid_t(mlo_ref[...].astype(F32))
    c_out = (lf_ref[...] + lb_ref[...]) * _gelu_tanh(ly_ref[...].astype(F32))
    merged = None
    for n, br in enumerate((a_out, b_out, c_out)):
        gate = _sigmoid_t(mg_ref[:, n * d_model:(n + 1) * d_model])
        term = gate * _dot(br.astype(BF16), wb_ref[n]).astype(BF16)
        merged = term if merged is None else merged + term
    y = gate_ref[...] * _dot(merged, wo_ref[...])

    if t0 == 0:
        @pl.when(t == 0)
        def _():
            ctx_o_ref[...] = ctx_ref[...] + y
    else:
        @pl.when(t == t0)
        def _():
            ctx_o_ref[...] = ctx_ref[...]

    @pl.when(t > 0)
    def _():
        x_o_ref[...] = x_ref[...] + y


def _merge_call(l, last, o_hg, o_ml, h_f, h_b, hg_a, ml_a, lru_y, mg, ctx, x, mods4,
                hg_norm, ml_norm, w_branch, w_out):
    bsz, lc, d_model = ctx.shape
    seq = x.shape[1]
    tm = lc
    t0 = 1 if last else 0
    nt = 1 + seq // tm - t0

    def tok(b, t):
        return t + t0

    def xi(t):
        return jnp.maximum(t + t0 - 1, 0)

    mix = lambda d: pl.BlockSpec((None, None, tm, MIX_W), lambda b, t: (b, d, tok(b, t), 0))
    lru = pl.BlockSpec((None, tm, MIX_W), lambda b, t: (b, tok(b, t), 0))
    in_specs = [
        mix(0), mix(1), mix(0), mix(1), lru, lru,
        pl.BlockSpec((None, tm, MIX_W), lambda b, t: (b, tok(b, t), 2)),
        pl.BlockSpec((None, tm, MIX_W), lambda b, t: (b, tok(b, t), 2)),
        pl.BlockSpec((None, tm, MIX_W), lambda b, t: (b, tok(b, t), 0)),
        pl.BlockSpec((None, tm, 3 * d_model), lambda b, t: (b, tok(b, t), 0)),
        pl.BlockSpec((None, tm, d_model), lambda b, t: (b, 0, 0)),
        pl.BlockSpec((None, tm, d_model), lambda b, t: (b, xi(t), 0)),
        pl.BlockSpec((None, None, 1, d_model), lambda b, t: (l, jnp.where(t + t0 == 0, bsz, b), 0, 2)),
        pl.BlockSpec((None, 1, MIX_W), lambda b, t: (l, 0, 0)),
        pl.BlockSpec((None, 1, MIX_W), lambda b, t: (l, 0, 0)),
        pl.BlockSpec((None, 3, MIX_W, d_model), lambda b, t: (l, 0, 0, 0)),
        pl.BlockSpec((None, d_model, d_model), lambda b, t: (l, 0, 0)),
    ]
    out_specs = [
        pl.BlockSpec((None, tm, d_model), lambda b, t: (b, 0, 0)),
        pl.BlockSpec((None, tm, d_model), lambda b, t: (b, xi(t), 0)),
    ]
    out_shape = [jax.ShapeDtypeStruct(ctx.shape, F32), jax.ShapeDtypeStruct(x.shape, F32)]
    return pl.pallas_call(
        functools.partial(_merge_body, t0=t0, d_model=d_model),
        grid=(bsz, nt),
        in_specs=in_specs,
        out_specs=out_specs,
        out_shape=out_shape,
        compiler_params=_cparams(("arbitrary", "arbitrary")),
        name="merge",
    )(o_hg, o_hg, o_ml, o_ml, h_f, h_b, hg_a, ml_a, lru_y, mg, ctx, x, mods4,
      hg_norm, ml_norm, w_branch, w_out)


def _ffn_rows(x_ref, sh_ref, sc_ref, gt_ref, ln_ref, wi_ref, wo_ref, fn_ref, o_ref, r0, rows, *,
              n_inner, final_norm):
    x = x_ref[r0:r0 + rows, :]
    h = _norm_mod(x, ln_ref[...], sh_ref[...], sc_ref[...]).astype(BF16)
    yield
    gu = _dot(h, wi_ref[...])
    yield
    hidden = gu.shape[-1] // 2
    gate = gu[:, :hidden]
    act = (gate * _sigmoid_t(gate) * gu[:, hidden:]).astype(BF16)
    yield
    f = _dot(act, wo_ref[...])
    yield
    y = x + gt_ref[...] * f
    if final_norm:
        y = y * lax.rsqrt(jnp.mean(y * y, axis=-1, keepdims=True) + EPS) * fn_ref[...]
    if n_inner is None:
        o_ref[r0:r0 + rows, :] = y
    else:
        for a in range(rows // n_inner):
            o_ref[:, r0 // n_inner + a, :] = y[a * n_inner:(a + 1) * n_inner, :]
    yield


def _ffn_body(x_ref, sh_ref, sc_ref, gt_ref, ln_ref, wi_ref, wo_ref, fn_ref, o_ref, *, rows, n_inner,
              final_norm):
    gens = [_ffn_rows(x_ref, sh_ref, sc_ref, gt_ref, ln_ref, wi_ref, wo_ref, fn_ref, o_ref, r0, rows,
                      n_inner=n_inner, final_norm=final_norm)
            for r0 in range(0, x_ref.shape[0], rows)]
    skew = 2
    n_phase = 5
    for step in range(n_phase + skew * (len(gens) - 1)):
        for i, g in enumerate(gens):
            if 0 <= step - skew * i < n_phase:
                next(g)


def _resident(shape, index_map):
    return pl.BlockSpec(shape, index_map, pipeline_mode=pl.Buffered(1))


def _ffn_call(l, x, mods4, ctx_row, ln2, wi, wo, fin, *, tm, n_inner, final_norm):
    bsz, n, d_model = x.shape
    hidden = wo.shape[1]
    nt = n // tm
    rows = min(tm, FFN_ROWS)

    def mod_spec(chunk):
        return pl.BlockSpec((None, None, 1, d_model),
                            lambda b, t: (l, bsz if ctx_row else b, 0, chunk))

    if n_inner is None:
        o_spec = pl.BlockSpec((None, tm, d_model), lambda b, t: (b, t, 0))
        o_shape = jax.ShapeDtypeStruct(x.shape, F32)
    else:
        assert rows % n_inner == 0
        n_a = tm // n_inner
        o_spec = pl.BlockSpec((None, n_inner, n_a, d_model), lambda b, t: (b, 0, t, 0))
        o_shape = jax.ShapeDtypeStruct((bsz, n_inner, n // n_inner, d_model), F32)
    out = pl.pallas_call(
        functools.partial(_ffn_body, rows=rows, n_inner=n_inner, final_norm=final_norm),
        grid=(bsz, nt),
        in_specs=[
            pl.BlockSpec((None, tm, d_model), lambda b, t: (b, t, 0)),
            mod_spec(3), mod_spec(4), mod_spec(5),
            pl.BlockSpec((None, 1, d_model), lambda b, t: (l, 0, 0)),
            _resident((None, d_model, 2 * hidden), lambda b, t: (l, 0, 0)),
            _resident((None, hidden, d_model), lambda b, t: (l, 0, 0)),
            pl.BlockSpec((1, d_model), lambda b, t: (0, 0)),
        ],
        out_specs=o_spec,
        out_shape=o_shape,
        compiler_params=_cparams(("arbitrary", "arbitrary")),
        name="ffn",
    )(x, mods4, mods4, mods4, ln2, wi, wo, fin)
    return out.reshape(bsz, n, d_model)


def _prep_in_proj(w_in, b_in):
    depth, d_model, _ = w_in.shape
    hq = N_HEADS * HG_DK
    sizes = (hq, MIX_W, MIX_W, 2 * hq, N_HEADS * ML_DQK, N_HEADS * ML_DQK, MIX_W, MIX_W,
             2 * N_HEADS, 2 * N_HEADS, MIX_W, MIX_W, 3 * d_model)
    offs = np.concatenate([[0], np.cumsum(sizes)])
    assert offs[-1] == w_in.shape[-1]

    def cols(a, lo, hi):
        return a[..., int(offs[lo]):int(offs[hi])]

    def gate_block(a):
        ig = cols(a, 8, 9).reshape(a.shape[:-1] + (2, N_HEADS))
        fg = cols(a, 9, 10).reshape(a.shape[:-1] + (2, N_HEADS))
        pad = jnp.zeros(a.shape[:-1] + (2, GATE_LANES - 2 * N_HEADS), a.dtype)
        return jnp.concatenate([ig, fg, pad], axis=-1).reshape(a.shape[:-1] + (2 * GATE_LANES,))

    groups = [
        (lambda a: cols(a, 0, 3), BF16),
        (lambda a: cols(a, 3, 4), F32),
        (lambda a: cols(a, 4, 8), BF16),
        (gate_block, F32),
        (lambda a: cols(a, 10, 11), F32),
        (lambda a: cols(a, 11, 12), BF16),
        (lambda a: cols(a, 12, 13), BF16),
    ]
    kinds = ("hgrn2_qig", "hgrn2_forget", None, "mlstm_gates", None, None, None)
    b3 = b_in.reshape(depth, 1, -1)
    ws = [f(w_in).astype(BF16) for f, _ in groups]
    bs = [f(b3) for f, _ in groups]
    return ws, bs, [dt for _, dt in groups], kinds


def _prep_lru_gates(lru_gate_w, lru_gate_b):
    depth = lru_gate_w.shape[0]
    eye = jnp.eye(LRU_BLOCKS, dtype=lru_gate_w.dtype)
    dense = jnp.einsum("lzgnde,nm->lndzgme", lru_gate_w, eye)
    dense = dense.reshape(depth, MIX_W, 4 * MIX_W).astype(BF16)
    return dense, lru_gate_b.reshape(depth, 1, 4 * MIX_W)


def kernel(x, c, ctx, c_ctx, w_ada, b_ada, ln1, w_in, b_in, hg_lb_raw, hg_norm, ml_norm, conv_w, conv_b,
           lru_gate_w, lru_gate_b, lru_lambda, w_branch, w_out, ln2, w_ffn_in, w_ffn_out, final_norm):
    bsz, seq, d_model = x.shape
    lc = ctx.shape[1]
    depth = w_ada.shape[0]
    rows = seq // GRID_W
    assert lc % CHUNK == 0 and seq % lc == 0 and seq % FFN_TM == 0 and bsz < 16

    c_all = jnp.zeros((16, d_model), F32).at[:bsz].set(c).at[bsz].set(c_ctx)
    mods4 = _ada_call(c_all, w_ada, b_ada).reshape(depth, 16, 1, 6 * d_model)
    lb_all = _lb_call(hg_lb_raw)
    lb3 = lb_all.reshape(depth, 1, 2 * MIX_W)
    lb4 = lb_all.reshape(depth, 2, 1, MIX_W)

    ws, bs, out_dtypes, kinds = _prep_in_proj(w_in, b_in)
    wgate, bgate = _prep_lru_gates(lru_gate_w, lru_gate_b)
    w_ffn_in_b = w_ffn_in.astype(BF16)
    w_ffn_out_b = w_ffn_out.astype(BF16)
    w_branch_b = w_branch.astype(BF16)
    w_out_b = w_out.astype(BF16)
    ln1_3 = ln1.reshape(depth, 1, d_model)
    ln2_3 = ln2.reshape(depth, 1, d_model)
    hgn = hg_norm.reshape(depth, 1, MIX_W)
    mln = ml_norm.reshape(depth, 1, MIX_W)
    conv_b3 = conv_b.reshape(depth, 1, MIX_W)
    fin = final_norm.reshape(1, d_model)

    for l in range(depth):
        last = l == depth - 1
        n_inner = GRID_W if l % 2 == 0 else rows
        hg_a, hg_f, ml_a, ml_g, lru_x, lru_y, mg = _in_proj_call(l, ctx, x, mods4, ln1_3, lb3, ws, bs,
                                                                 out_dtypes, kinds)
        o_hg, o_ml = _mixers_call(l, hg_a, hg_f, lb4, ml_a, ml_g, lc)
        a0, u0, a1, u1 = _lru_gates_call(l, lru_x, conv_w, conv_b3, wgate, bgate, lru_lambda, lc)
        h_f, h_b = _lru_scan_call(a0, u0, a1, u1, lc)
        ctx_m, x_m = _merge_call(l, last, o_hg, o_ml, h_f, h_b, hg_a, ml_a, lru_y, mg, ctx, x, mods4,
                                 hgn, mln, w_branch_b, w_out_b)
        x = _ffn_call(l, x_m, mods4, False, ln2_3, w_ffn_in_b, w_ffn_out_b, fin,
                      tm=FFN_TM, n_inner=n_inner, final_norm=last)
        if not last:
            ctx = _ffn_call(l, ctx_m, mods4, True, ln2_3, w_ffn_in_b, w_ffn_out_b, fin,
                            tm=lc, n_inner=None, final_norm=False)
    return x
```

```python
import functools
import itertools

import numpy as np
import jax
import jax.numpy as jnp
from jax import lax
from jax.experimental import pallas as pl
from jax.experimental.pallas import tpu as pltpu

F32 = jnp.float32
BF16 = jnp.bfloat16

GRID_W = 64
MIX_W = 512
N_HEADS = 4
HEAD_DV = 128
HG_DK = 128
ML_DQK = 64
LRU_BLOCKS = 8
LRU_BD = 64
LRU_C = 8.0
EPS = 1e-6
NEG_BIG = -1e30
LB_TINY = 1e-30
LOG2_E = 1.4426950408889634
GATE_LANES = 128
CHUNK = 128
MIX_NB = 4
MIX_SKEW = 14
FFN_TM = 512
FFN_ROWS = 256
IN_PROJ_ROWS = 128
VMEM_LIMIT = 56 * 1024 * 1024


def _cparams(sem):
    return pltpu.CompilerParams(dimension_semantics=sem, vmem_limit_bytes=VMEM_LIMIT)


def _sigmoid(x):
    return 1.0 / (1.0 + jnp.exp(-x))


def _sigmoid_t(x):
    return 0.5 * jnp.tanh(0.5 * x) + 0.5


def _log_sigmoid(x):
    return jnp.minimum(x, 0.0) - jnp.log(1.0 + jnp.exp(-jnp.abs(x)))


def _dot(a, b):
    return jnp.dot(a, b, preferred_element_type=F32)


def _dot_nt(a, b):
    return lax.dot_general(a, b, (((1,), (1,)), ((), ())), preferred_element_type=F32)


def _dot_tn(a, b):
    return lax.dot_general(a, b, (((0,), (0,)), ((), ())), preferred_element_type=F32)


def _split_hi_lo(x):
    hi = x.astype(BF16)
    lo = (x - hi.astype(F32)).astype(BF16)
    return hi, lo


def _norm_mod(x, ln, shift, scale):
    y = x * lax.rsqrt(jnp.mean(x * x, axis=-1, keepdims=True) + EPS) * ln
    return y * (1.0 + scale) + shift


def _mirror(m):
    return m[..., ::-1, ::-1].copy()


@functools.lru_cache(maxsize=None)
def _hgrn2_consts(C):
    n_lv = int(np.log2(C))
    mats, masks = [], []
    r = np.arange(C)
    for lv in range(n_lv):
        s = C >> (lv + 1)
        base = (r // (2 * s)) * (2 * s)
        mid = base + s - 1
        odd = r >= base + s
        m = np.zeros((C, C), np.float32)
        for t in range(C):
            if odd[t]:
                m[t, mid[t] + 1:t + 1] = 1.0
            else:
                m[t, t + 1:mid[t] + 1] = 1.0
        mats.append(m)
        same = base[:, None] == base[None, :]
        masks.append((same & odd[:, None] & (~odd)[None, :]).astype(np.float32))
    mats.append(np.tril(np.ones((C, C), np.float32)))
    masks.append(np.eye(C, dtype=np.float32))
    mats = np.stack(mats)
    masks = np.stack(masks)
    mats = np.stack([mats, _mirror(mats)])
    masks = np.stack([masks, _mirror(masks)])
    mst = mats.reshape(2, (n_lv + 1) * C, C)
    mst = np.concatenate([mst, mst], axis=-1)
    return mst, masks, n_lv


@functools.lru_cache(maxsize=None)
def _mlstm_consts(C):
    tri = np.tril(np.ones((C, C), np.float32))
    tri = np.stack([tri, _mirror(tri)])
    tri_cat = np.concatenate([tri, tri], axis=-1)
    tri_t = np.transpose(tri, (0, 2, 1))
    tri_t_cat = np.concatenate([tri_t, tri_t], axis=1)
    return tri_cat, tri_t_cat, tri


def _chunk_index(d, j, nctx, nch):
    bw = jnp.where(j < nctx, nctx - 1 - j, nch - 1 + nctx - j)
    return jnp.where(d == 0, j, bw)


def _ada_body(c_ref, w_ref, b_ref, o_ref):
    cc = c_ref[...]
    s = cc * _sigmoid(cc)
    o_ref[...] = jnp.dot(s, w_ref[...], preferred_element_type=F32,
                         precision=lax.Precision.HIGHEST) + b_ref[...]


def _ada_call(c_all, w_ada, b_ada):
    depth, d_model, n6 = w_ada.shape
    rows = c_all.shape[0]
    tn = 1536
    return pl.pallas_call(
        _ada_body,
        grid=(depth, n6 // tn),
        in_specs=[
            pl.BlockSpec((rows, d_model), lambda l, n: (0, 0)),
            pl.BlockSpec((None, d_model, tn), lambda l, n: (l, 0, n)),
            pl.BlockSpec((None, 1, tn), lambda l, n: (l, 0, n)),
        ],
        out_specs=pl.BlockSpec((None, rows, tn), lambda l, n: (l, 0, n)),
        out_shape=jax.ShapeDtypeStruct((depth, rows, n6), F32),
        compiler_params=_cparams(("arbitrary", "arbitrary")),
        name="ada_mod",
    )(c_all, w_ada, b_ada.reshape(depth, 1, n6))


def _lb_body(raw_ref, o_ref):
    raw = raw_ref[...]
    depth = raw.shape[0]
    e = jnp.exp(raw - jnp.max(raw, axis=0, keepdims=True))
    p = e / jnp.sum(e, axis=0, keepdims=True)
    acc = jnp.zeros_like(p[0:1])
    for l in range(depth):
        acc = acc + p[l:l + 1]
        o_ref[l:l + 1, :] = acc - p[0:1]


def _lb_call(hg_lb_raw):
    depth = hg_lb_raw.shape[0]
    raw = hg_lb_raw.reshape(depth, -1)
    return pl.pallas_call(
        _lb_body,
        out_shape=jax.ShapeDtypeStruct(raw.shape, F32),
        name="hgrn2_lower_bounds",
    )(raw)


def _log2_forget(f_pre, lb):
    e = jnp.exp(-jnp.abs(f_pre))
    inv = 1.0 / (1.0 + e)
    sig = jnp.where(f_pre >= 0.0, inv, e * inv)
    return jnp.log(jnp.maximum(lb, LB_TINY) + (1.0 - lb) * sig) * LOG2_E


def _in_proj_body(ctx_ref, x_ref, sh_ref, sc_ref, ln_ref, lb_ref, *rest, kinds):
    n_out = len(kinds)
    w_refs = rest[:n_out]
    b_refs = rest[n_out:2 * n_out]
    o_refs = rest[2 * n_out:]
    t = pl.program_id(1)
    tm = x_ref.shape[0]
    rows = min(tm, IN_PROJ_ROWS)

    def finish(kind, y, o_ref, rs):
        if kind == "hgrn2_qig":
            q = y[:, :MIX_W]
            o_ref[rs, :MIX_W] = (q * _sigmoid(q)).astype(o_ref.dtype)
            o_ref[rs, MIX_W:] = y[:, MIX_W:].astype(o_ref.dtype)
        elif kind == "hgrn2_forget":
            o_ref[rs, :] = _log2_forget(y, lb_ref[...]).astype(o_ref.dtype)
        elif kind == "mlstm_gates":
            lane = lax.broadcasted_iota(jnp.int32, y.shape, 1) % GATE_LANES
            is_f = jnp.logical_and(lane >= N_HEADS, lane < 2 * N_HEADS)
            o_ref[rs, :] = jnp.where(is_f, _log_sigmoid(y), y).astype(o_ref.dtype)
        else:
            o_ref[rs, :] = y.astype(o_ref.dtype)

    def row_group(r0):
        rs = slice(r0, r0 + rows)
        xin = jnp.where(t == 0, ctx_ref[rs, :], x_ref[rs, :])
        h = _norm_mod(xin, ln_ref[...], sh_ref[...], sc_ref[...]).astype(BF16)
        yield
        for kind, w_ref, b_ref, o_ref in zip(kinds, w_refs, b_refs, o_refs):
            finish(kind, _dot(h, w_ref[...]) + b_ref[...], o_ref, rs)
            yield

    gens = [row_group(r0) for r0 in range(0, tm, rows)]
    n_phase = 1 + n_out
    for step in range(n_phase + len(gens) - 1):
        for i, g in enumerate(gens):
            if 0 <= step - i < n_phase:
                next(g)


def _in_proj_call(l, ctx, x, mods4, ln1, lb3, ws, bs, out_dtypes, kinds):
    bsz, lc, d_model = ctx.shape
    seq = x.shape[1]
    tm = lc
    nt = 1 + seq // tm
    n_out = len(ws)

    def mod_spec(chunk):
        return pl.BlockSpec((None, None, 1, d_model),
                            lambda b, t: (l, jnp.where(t == 0, bsz, b), 0, chunk))

    in_specs = [
        pl.BlockSpec((None, tm, d_model), lambda b, t: (b, 0, 0)),
        pl.BlockSpec((None, tm, d_model), lambda b, t: (b, jnp.maximum(t - 1, 0), 0)),
        mod_spec(0), mod_spec(1),
        pl.BlockSpec((None, 1, d_model), lambda b, t: (l, 0, 0)),
        pl.BlockSpec((None, 1, lb3.shape[-1]), lambda b, t: (l, 0, 0)),
    ]
    in_specs += [pl.BlockSpec((None, d_model, w.shape[-1]), lambda b, t: (l, 0, 0)) for w in ws]
    in_specs += [pl.BlockSpec((None, 1, w.shape[-1]), lambda b, t: (l, 0, 0)) for w in ws]
    out_specs = [pl.BlockSpec((None, tm, w.shape[-1]), lambda b, t: (b, t, 0)) for w in ws]
    out_shape = [jax.ShapeDtypeStruct((bsz, lc + seq, w.shape[-1]), dt) for w, dt in zip(ws, out_dtypes)]
    return pl.pallas_call(
        functools.partial(_in_proj_body, kinds=kinds),
        grid=(bsz, nt),
        in_specs=in_specs,
        out_specs=out_specs,
        out_shape=out_shape,
        compiler_params=_cparams(("arbitrary", "arbitrary")),
        name="in_proj",
    )(ctx, x, mods4, mods4, ln1, lb3, *ws, *bs)


def _hgrn2_body(qv_ref, g_ref, lb_ref, mst_ref, msk_ref, o_ref, st_ref, *, C, n_lv):
    q16 = qv_ref[:, :MIX_W]
    v = qv_ref[:, MIX_W:]
    q = q16.astype(F32)
    g = g_ref[...]
    lb = lb_ref[...]
    kk = (1.0 - jnp.exp2(g)) + (jnp.maximum(lb, LB_TINY) - lb)
    g_hi, g_lo = _split_hi_lo(g)
    ex = _dot(mst_ref[...], jnp.concatenate([g_hi, g_lo], axis=0))
    tot = jnp.sum(g, axis=0, keepdims=True)
    k16 = kk.astype(BF16)

    for h in range(N_HEADS):
        sl = slice(h * HG_DK, (h + 1) * HG_DK)
        qh = q16[:, sl]
        kh = k16[:, sl]
        p = msk_ref[n_lv] * _dot_nt(qh, kh)
        for lv in range(n_lv):
            w = jnp.exp2(ex[lv * C:(lv + 1) * C, sl]).astype(BF16)
            p = p + msk_ref[lv] * _dot_nt(qh * w, kh * w)
            if lv % 2 == 1:
                yield
        st = st_ref[h]
        b_in = ex[n_lv * C:(n_lv + 1) * C, sl]
        qb = (q[:, sl] * jnp.exp2(b_in)).astype(BF16)
        vh = v[:, sl]
        o_ref[:, sl] = (_dot(p.astype(BF16), vh) + _dot_nt(qb, st.astype(BF16))).astype(o_ref.dtype)
        tot_h = tot[:, sl]
        kb = (kk[:, sl] * jnp.exp2(tot_h - b_in)).astype(BF16)
        st_ref[h] = st * jnp.exp2(tot_h) + _dot_tn(vh, kb)
        yield


def _mlstm_body(qkv_ref, g_ref, tri_ref, trit_ref, msk_ref, o_ref, c_ref, m_ref, *, C):
    qk_w = N_HEADS * ML_DQK
    gates = g_ref[...]
    gates_t = gates.T
    lf = gates
    lf_t = gates_t
    lf_hi, lf_lo = _split_hi_lo(lf)
    b_cols = _dot(tri_ref[...], jnp.concatenate([lf_hi, lf_lo], axis=0))
    lft_hi, lft_lo = _split_hi_lo(lf_t)
    b_rows = _dot(jnp.concatenate([lft_hi, lft_lo], axis=1), trit_ref[...])
    tot = jnp.sum(lf, axis=0, keepdims=True)
    allowed = msk_ref[...] > 0.0
    ones = jnp.ones((C, HEAD_DV), BF16)
    scale = ML_DQK ** -0.5

    for h in range(N_HEADS):
        b_col = b_cols[:, N_HEADS + h:N_HEADS + h + 1]
        b_row = b_rows[N_HEADS + h:N_HEADS + h + 1, :]
        i_row = gates_t[h:h + 1, :]
        i_col = gates[:, h:h + 1]
        m_prev = m_ref[h][0:1, 0:1]
        log_d = jnp.where(allowed, b_col - b_row + i_row, NEG_BIG)
        log_inter = b_col + m_prev
        m_row = jnp.maximum(jnp.max(log_d, axis=-1, keepdims=True), log_inter)
        yield
        qh = qkv_ref[:, h * ML_DQK:(h + 1) * ML_DQK]
        kh = qkv_ref[:, qk_w + h * ML_DQK:qk_w + (h + 1) * ML_DQK]
        vh = qkv_ref[:, 2 * qk_w + h * HEAD_DV:2 * qk_w + (h + 1) * HEAD_DV]
        vaug = jnp.concatenate([vh, ones], axis=1)
        s = (_dot_nt(qh, kh) * scale) * jnp.exp(log_d - m_row)
        w_inter = jnp.exp(log_inter - m_row)
        yield
        c_aug = c_ref[h]
        r = _dot(s.astype(BF16), vaug) + (w_inter * scale) * _dot(qh, c_aug.astype(BF16))
        num = r[:, :HEAD_DV]
        den = r[:, HEAD_DV:]
        hout = num / jnp.maximum(jnp.abs(den), jnp.exp(-m_row))
        o_ref[:, h * HEAD_DV:(h + 1) * HEAD_DV] = hout.astype(o_ref.dtype)
        yield
        b_last = tot[:, N_HEADS + h:N_HEADS + h + 1]
        log_w = b_last - b_col + i_col
        m_new = jnp.maximum(b_last + m_prev, jnp.max(log_w, axis=0, keepdims=True))
        w_s = jnp.exp(log_w - m_new)
        decay = jnp.exp(b_last + m_prev - m_new)
        kw = (kh.astype(F32) * w_s).astype(BF16)
        c_ref[h] = decay * c_aug + _dot_tn(kw, vaug)
        m_ref[h] = jnp.broadcast_to(m_new, m_ref.shape[1:])
        yield


def _mixers_body(hqv_ref, hg_ref, lb_ref, mst_ref, hmsk_ref, mqkv_ref, mg_ref,
                 tri_ref, trit_ref, mmsk_ref, o_hg_ref, o_ml_ref, st_ref, c_ref, m_ref, *, C, n_lv):
    @pl.when(pl.program_id(2) == 0)
    def _():
        st_ref[...] = jnp.zeros_like(st_ref)
        c_ref[...] = jnp.zeros_like(c_ref)
        m_ref[...] = jnp.zeros_like(m_ref)

    def element(bb):
        gens = [_hgrn2_body(hqv_ref.at[bb], hg_ref.at[bb], lb_ref, mst_ref, hmsk_ref,
                            o_hg_ref.at[bb], st_ref.at[bb], C=C, n_lv=n_lv),
                _mlstm_body(mqkv_ref.at[bb], mg_ref.at[bb], tri_ref, trit_ref,
                            mmsk_ref, o_ml_ref.at[bb], c_ref.at[bb], m_ref.at[bb], C=C)]
        for _ in itertools.zip_longest(*gens):
            yield

    streams = [element(bb) for bb in range(hqv_ref.shape[0])]
    live = []
    while streams or live:
        if streams and (not live or live[-1][1] >= MIX_SKEW):
            live.append([streams.pop(0), 0])
        for item in list(live):
            try:
                next(item[0])
                item[1] += 1
            except StopIteration:
                live.remove(item)


def _mixers_call(l, hg_a, hg_f, lb4, ml_a, ml_g, lc):
    bsz, tt, _ = hg_a.shape
    C = CHUNK
    nch, nctx = tt // C, lc // C
    mst, hmsk, n_lv = _hgrn2_consts(C)
    mst = jnp.asarray(mst, BF16)
    hmsk = jnp.asarray(hmsk, F32)
    tri_cat, tri_t_cat, tri = _mlstm_consts(C)
    tri_cat = jnp.asarray(tri_cat, BF16)
    tri_t_cat = jnp.asarray(tri_t_cat, BF16)
    mmsk = jnp.asarray(tri, F32)
    cidx = functools.partial(_chunk_index, nctx=nctx, nch=nch)
    qk_w = N_HEADS * ML_DQK
    tok = lambda col: (lambda b, d, j: (b, cidx(d, j), col))
    tok_d = lambda b, d, j: (b, cidx(d, j), d)
    per_dir3 = lambda b, d, j: (d, 0, 0)
    nb = max(n for n in range(1, MIX_NB + 1) if bsz % n == 0)
    o_spec = pl.BlockSpec((nb, None, C, MIX_W), lambda b, d, j: (b, d, cidx(d, j), 0))
    o_shape = jax.ShapeDtypeStruct((bsz, 2, tt, MIX_W), BF16)
    return pl.pallas_call(
        functools.partial(_mixers_body, C=C, n_lv=n_lv),
        grid=(bsz // nb, 2, nch),
        in_specs=[
            pl.BlockSpec((nb, C, 2 * MIX_W), tok(0)),
            pl.BlockSpec((nb, C, MIX_W), tok_d),
            pl.BlockSpec((None, None, 1, MIX_W), lambda b, d, j: (l, d, 0, 0)),
            pl.BlockSpec((None,) + mst.shape[1:], per_dir3),
            pl.BlockSpec((None,) + hmsk.shape[1:], lambda b, d, j: (d, 0, 0, 0)),
            pl.BlockSpec((nb, C, 2 * qk_w + MIX_W), tok(0)),
            pl.BlockSpec((nb, C, GATE_LANES), tok_d),
            pl.BlockSpec((None, C, 2 * C), per_dir3),
            pl.BlockSpec((None, 2 * C, C), per_dir3),
            pl.BlockSpec((None, C, C), per_dir3),
        ],
        out_specs=[o_spec, o_spec],
        out_shape=[o_shape, o_shape],
        scratch_shapes=[pltpu.VMEM((nb, N_HEADS, HEAD_DV, HG_DK), F32),
                        pltpu.VMEM((nb, N_HEADS, ML_DQK, 2 * HEAD_DV), F32),
                        pltpu.VMEM((nb, N_HEADS, 8, 128), F32)],
        compiler_params=_cparams(("arbitrary", "arbitrary", "arbitrary")),
        name="mixers",
    )(hg_a, hg_f, lb4, mst, hmsk, ml_a, ml_g, tri_cat, tri_t_cat, mmsk)


def _lru_gates_body(x_ref, xp_ref, xn_ref, cw_ref, cb_ref, wg_ref, bg_ref, lam_ref,
                    a0_ref, u0_ref, a1_ref, u1_ref, ext_ref, *, tc, nctx_t, nt):
    t = pl.program_id(1)
    prev_ok = jnp.logical_and(t != 0, t != nctx_t)
    next_ok = jnp.logical_and(t != nctx_t - 1, t != nt - 1)
    x = x_ref[...]
    ext_ref[0:8, :] = jnp.where(prev_ok, xp_ref[...], 0.0)
    ext_ref[8:8 + tc, :] = x
    ext_ref[8 + tc:16 + tc, :] = jnp.where(next_ok, xn_ref[...], 0.0)
    cw = cw_ref[...]
    xc = (cw[0:1] * ext_ref[6:6 + tc, :] + cw[1:2] * ext_ref[7:7 + tc, :] + cw[2:3] * x
          + cw[3:4] * ext_ref[9:9 + tc, :]) + cb_ref[...]
    gates = _dot(xc.astype(BF16), wg_ref[...]) + bg_ref[...]
    outs = ((a0_ref, u0_ref), (a1_ref, u1_ref))
    for d in range(2):
        r_pre = gates[:, (2 * d) * MIX_W:(2 * d + 1) * MIX_W]
        i_pre = gates[:, (2 * d + 1) * MIX_W:(2 * d + 2) * MIX_W]
        z = -lam_ref[d:d + 1, :]
        softplus = jnp.maximum(z, 0.0) + jnp.log(1.0 + jnp.exp(-jnp.abs(z)))
        log_a = (-LRU_C * softplus) * _sigmoid_t(r_pre)
        a_ref, u_ref = outs[d]
        a = jnp.exp(log_a)
        a_ref[...] = a
        one_m_a2 = -jnp.tanh(log_a) * (a * a + 1.0)
        u_ref[...] = jnp.sqrt(jnp.maximum(one_m_a2, 0.0)) * (_sigmoid_t(i_pre) * xc)


def _lru_gates_call(l, lru_x, conv_w, conv_b, wg, bg, lam, lc):
    bsz, tt, w = lru_x.shape
    tc = lc
    nt, nctx_t = tt // tc, lc // tc
    h8 = tc // 8
    out = jax.ShapeDtypeStruct((bsz, tt, w), F32)
    o_spec = pl.BlockSpec((None, tc, w), lambda b, t: (b, t, 0))
    return pl.pallas_call(
        functools.partial(_lru_gates_body, tc=tc, nctx_t=nctx_t, nt=nt),
        grid=(bsz, nt),
        in_specs=[
            pl.BlockSpec((None, tc, w), lambda b, t: (b, t, 0)),
            pl.BlockSpec((None, 8, w), lambda b, t: (b, jnp.maximum(t * h8 - 1, 0), 0)),
            pl.BlockSpec((None, 8, w), lambda b, t: (b, jnp.minimum((t + 1) * h8, tt // 8 - 1), 0)),
            pl.BlockSpec((None, 4, w), lambda b, t: (l, 0, 0)),
            pl.BlockSpec((None, 1, w), lambda b, t: (l, 0, 0)),
            pl.BlockSpec((None, w, 4 * w), lambda b, t: (l, 0, 0)),
            pl.BlockSpec((None, 1, 4 * w), lambda b, t: (l, 0, 0)),
            pl.BlockSpec((None, 2, w), lambda b, t: (l, 0, 0)),
        ],
        out_specs=[o_spec] * 4,
        out_shape=[out] * 4,
        scratch_shapes=[pltpu.VMEM((tc + 16, w), F32)],
        compiler_params=_cparams(("arbitrary", "arbitrary")),
        name="rglru_gates",
    )(lru_x, lru_x, lru_x, conv_w, conv_b, wg, bg, lam)


def _lru_scan_body(a0_ref, u0_ref, a1_ref, u1_ref, hf_ref, hb_ref, sf_ref, sb_ref, *, tc):
    @pl.when(pl.program_id(0) == 0)
    def _():
        sf_ref[...] = jnp.zeros_like(sf_ref)
        sb_ref[...] = jnp.zeros_like(sb_ref)

    def step(i, carry):
        hf, hb = carry
        hf = a0_ref[:, i, :] * hf + u0_ref[:, i, :]
        hf_ref[:, i, :] = hf
        ib = tc - 1 - i
        hb = a1_ref[:, ib, :] * hb + u1_ref[:, ib, :]
        hb_ref[:, ib, :] = hb
        return hf, hb

    hf, hb = lax.fori_loop(0, tc, step, (sf_ref[...], sb_ref[...]), unroll=8)
    sf_ref[...] = hf
    sb_ref[...] = hb


def _lru_scan_call(a0, u0, a1, u1, lc):
    bsz, tt, w = a0.shape
    tc = CHUNK
    nch, nctx = tt // tc, lc // tc
    fw = pl.BlockSpec((bsz, tc, w), lambda j: (0, j, 0))
    bw = pl.BlockSpec((bsz, tc, w), lambda j: (0, _chunk_index(1, j, nctx, nch), 0))
    out = jax.ShapeDtypeStruct((bsz, tt, w), F32)
    return pl.pallas_call(
        functools.partial(_lru_scan_body, tc=tc),
        grid=(nch,),
        in_specs=[fw, fw, bw, bw],
        out_specs=[fw, bw],
        out_shape=[out, out],
        scratch_shapes=[pltpu.VMEM((bsz, w), F32), pltpu.VMEM((bsz, w), F32)],
        compiler_params=_cparams(("arbitrary",)),
        name="rglru_scan",
    )(a0, u0, a1, u1)


def _head_rms(o, w):
    parts = []
    for h in range(N_HEADS):
        oh = o[:, h * HEAD_DV:(h + 1) * HEAD_DV]
        parts.append(oh * lax.rsqrt(jnp.mean(oh * oh, axis=-1, keepdims=True) + EPS))
    return jnp.concatenate(parts, axis=1) * w


def _gelu_tanh(x):
    return 0.5 * x * (1.0 + jnp.tanh(0.7978845608028654 * (x + 0.044715 * (x * x * x))))


def _merge_body(hg0_ref, hg1_ref, ml0_ref, ml1_ref, lf_ref, lb_ref, hgg_ref, mlo_ref, ly_ref, mg_ref,
                ctx_ref, x_ref, gate_ref, hgn_ref, mln_ref, wb_ref, wo_ref, ctx_o_ref, x_o_ref,
                *, t0, d_model):
    t = pl.program_id(1) + t0
    hgg = hgg_ref[...].astype(F32)
    o_hg = hg0_ref[...].astype(F32) + hg1_ref[...].astype(F32)
    o_ml = ml0_ref[...].astype(F32) + ml1_ref[...].astype(F32)
    a_out = _head_rms(o_hg, hgn_ref[...]) * (hgg * _sigmoid_t(hgg))
    b_out = _head_rms(o_ml, mln_ref[...]) * _sigmoid_t(mlo_ref[...].astype(F32))
    c_out = (lf_ref[...] + lb_ref[...]) * _gelu_tanh(ly_ref[...].astype(F32))
    merged = None
    for n, br in enumerate((a_out, b_out, c_out)):
        gate = _sigmoid_t(mg_ref[:, n * d_model:(n + 1) * d_model])
        term = gate * _dot(br.astype(BF16), wb_ref[n]).astype(BF16)
        merged = term if merged is None else merged + term
    y = gate_ref[...] * _dot(merged, wo_ref[...])

    if t0 == 0:
        @pl.when(t == 0)
        def _():
            ctx_o_ref[...] = ctx_ref[...] + y
    else:
        @pl.when(t == t0)
        def _():
            ctx_o_ref[...] = ctx_ref[...]

    @pl.when(t > 0)
    def _():
        x_o_ref[...] = x_ref[...] + y


def _merge_call(l, last, o_hg, o_ml, h_f, h_b, hg_a, ml_a, lru_y, mg, ctx, x, mods4,
                hg_norm, ml_norm, w_branch, w_out):
    bsz, lc, d_model = ctx.shape
    seq = x.shape[1]
    tm = lc
    t0 = 1 if last else 0
    nt = 1 + seq // tm - t0

    def tok(b, t):
        return t + t0

    def xi(t):
        return jnp.maximum(t + t0 - 1, 0)

    mix = lambda d: pl.BlockSpec((None, None, tm, MIX_W), lambda b, t: (b, d, tok(b, t), 0))
    lru = pl.BlockSpec((None, tm, MIX_W), lambda b, t: (b, tok(b, t), 0))
    in_specs = [
        mix(0), mix(1), mix(0), mix(1), lru, lru,
        pl.BlockSpec((None, tm, MIX_W), lambda b, t: (b, tok(b, t), 2)),
        pl.BlockSpec((None, tm, MIX_W), lambda b, t: (b, tok(b, t), 2)),
        pl.BlockSpec((None, tm, MIX_W), lambda b, t: (b, tok(b, t), 0)),
        pl.BlockSpec((None, tm, 3 * d_model), lambda b, t: (b, tok(b, t), 0)),
        pl.BlockSpec((None, tm, d_model), lambda b, t: (b, 0, 0)),
        pl.BlockSpec((None, tm, d_model), lambda b, t: (b, xi(t), 0)),
        pl.BlockSpec((None, None, 1, d_model), lambda b, t: (l, jnp.where(t + t0 == 0, bsz, b), 0, 2)),
        pl.BlockSpec((None, 1, MIX_W), lambda b, t: (l, 0, 0)),
        pl.BlockSpec((None, 1, MIX_W), lambda b, t: (l, 0, 0)),
        pl.BlockSpec((None, 3, MIX_W, d_model), lambda b, t: (l, 0, 0, 0)),
        pl.BlockSpec((None, d_model, d_model), lambda b, t: (l, 0, 0)),
    ]
    out_specs = [
        pl.BlockSpec((None, tm, d_model), lambda b, t: (b, 0, 0)),
        pl.BlockSpec((None, tm, d_model), lambda b, t: (b, xi(t), 0)),
    ]
    out_shape = [jax.ShapeDtypeStruct(ctx.shape, F32), jax.ShapeDtypeStruct(x.shape, F32)]
    return pl.pallas_call(
        functools.partial(_merge_body, t0=t0, d_model=d_model),
        grid=(bsz, nt),
        in_specs=in_specs,
        out_specs=out_specs,
        out_shape=out_shape,
        compiler_params=_cparams(("arbitrary", "arbitrary")),
        name="merge",
    )(o_hg, o_hg, o_ml, o_ml, h_f, h_b, hg_a, ml_a, lru_y, mg, ctx, x, mods4,
      hg_norm, ml_norm, w_branch, w_out)


def _ffn_rows(x_ref, sh_ref, sc_ref, gt_ref, ln_ref, wi_ref, wo_ref, fn_ref, o_ref, r0, rows, *,
              n_inner, final_norm):
    x = x_ref[r0:r0 + rows, :]
    h = _norm_mod(x, ln_ref[...], sh_ref[...], sc_ref[...]).astype(BF16)
    yield
    gu = _dot(h, wi_ref[...])
    yield
    hidden = gu.shape[-1] // 2
    gate = gu[:, :hidden]
    act = (gate * _sigmoid_t(gate) * gu[:, hidden:]).astype(BF16)
    yield
    f = _dot(act, wo_ref[...])
    yield
    y = x + gt_ref[...] * f
    if final_norm:
        y = y * lax.rsqrt(jnp.mean(y * y, axis=-1, keepdims=True) + EPS) * fn_ref[...]
    if n_inner is None:
        o_ref[r0:r0 + rows, :] = y
    else:
        for a in range(rows // n_inner):
            o_ref[:, r0 // n_inner + a, :] = y[a * n_inner:(a + 1) * n_inner, :]
    yield


def _ffn_body(x_ref, sh_ref, sc_ref, gt_ref, ln_ref, wi_ref, wo_ref, fn_ref, o_ref, *, rows, n_inner,
              final_norm):
    gens = [_ffn_rows(x_ref, sh_ref, sc_ref, gt_ref, ln_ref, wi_ref, wo_ref, fn_ref, o_ref, r0, rows,
                      n_inner=n_inner, final_norm=final_norm)
            for r0 in range(0, x_ref.shape[0], rows)]
    skew = 2
    n_phase = 5
    for step in range(n_phase + skew * (len(gens) - 1)):
        for i, g in enumerate(gens):
            if 0 <= step - skew * i < n_phase:
                next(g)


def _resident(shape, index_map):
    return pl.BlockSpec(shape, index_map, pipeline_mode=pl.Buffered(1))


def _ffn_call(l, x, mods4, ctx_row, ln2, wi, wo, fin, *, tm, n_inner, final_norm):
    bsz, n, d_model = x.shape
    hidden = wo.shape[1]
    nt = n // tm
    rows = min(tm, FFN_ROWS)

    def mod_spec(chunk):
        return pl.BlockSpec((None, None, 1, d_model),
                            lambda b, t: (l, bsz if ctx_row else b, 0, chunk))

    if n_inner is None:
        o_spec = pl.BlockSpec((None, tm, d_model), lambda b, t: (b, t, 0))
        o_shape = jax.ShapeDtypeStruct(x.shape, F32)
    else:
        assert rows % n_inner == 0
        n_a = tm // n_inner
        o_spec = pl.BlockSpec((None, n_inner, n_a, d_model), lambda b, t: (b, 0, t, 0))
        o_shape = jax.ShapeDtypeStruct((bsz, n_inner, n // n_inner, d_model), F32)
    out = pl.pallas_call(
        functools.partial(_ffn_body, rows=rows, n_inner=n_inner, final_norm=final_norm),
        grid=(bsz, nt),
        in_specs=[
            pl.BlockSpec((None, tm, d_model), lambda b, t: (b, t, 0)),
            mod_spec(3), mod_spec(4), mod_spec(5),
            pl.BlockSpec((None, 1, d_model), lambda b, t: (l, 0, 0)),
            _resident((None, d_model, 2 * hidden), lambda b, t: (l, 0, 0)),
            _resident((None, hidden, d_model), lambda b, t: (l, 0, 0)),
            pl.BlockSpec((1, d_model), lambda b, t: (0, 0)),
        ],
        out_specs=o_spec,
        out_shape=o_shape,
        compiler_params=_cparams(("arbitrary", "arbitrary")),
        name="ffn",
    )(x, mods4, mods4, mods4, ln2, wi, wo, fin)
    return out.reshape(bsz, n, d_model)


def _prep_in_proj(w_in, b_in):
    depth, d_model, _ = w_in.shape
    hq = N_HEADS * HG_DK
    sizes = (hq, MIX_W, MIX_W, 2 * hq, N_HEADS * ML_DQK, N_HEADS * ML_DQK, MIX_W, MIX_W,
             2 * N_HEADS, 2 * N_HEADS, MIX_W, MIX_W, 3 * d_model)
    offs = np.concatenate([[0], np.cumsum(sizes)])
    assert offs[-1] == w_in.shape[-1]

    def cols(a, lo, hi):
        return a[..., int(offs[lo]):int(offs[hi])]

    def gate_block(a):
        ig = cols(a, 8, 9).reshape(a.shape[:-1] + (2, N_HEADS))
        fg = cols(a, 9, 10).reshape(a.shape[:-1] + (2, N_HEADS))
        pad = jnp.zeros(a.shape[:-1] + (2, GATE_LANES - 2 * N_HEADS), a.dtype)
        return jnp.concatenate([ig, fg, pad], axis=-1).reshape(a.shape[:-1] + (2 * GATE_LANES,))

    groups = [
        (lambda a: cols(a, 0, 3), BF16),
        (lambda a: cols(a, 3, 4), F32),
        (lambda a: cols(a, 4, 8), BF16),
        (gate_block, F32),
        (lambda a: cols(a, 10, 11), F32),
        (lambda a: cols(a, 11, 12), BF16),
        (lambda a: cols(a, 12, 13), BF16),
    ]
    kinds = ("hgrn2_qig", "hgrn2_forget", None, "mlstm_gates", None, None, None)
    b3 = b_in.reshape(depth, 1, -1)
    ws = [f(w_in).astype(BF16) for f, _ in groups]
    bs = [f(b3) for f, _ in groups]
    return ws, bs, [dt for _, dt in groups], kinds


def _prep_lru_gates(lru_gate_w, lru_gate_b):
    depth = lru_gate_w.shape[0]
    eye = jnp.eye(LRU_BLOCKS, dtype=lru_gate_w.dtype)
    dense = jnp.einsum("lzgnde,nm->lndzgme", lru_gate_w, eye)
    dense = dense.reshape(depth, MIX_W, 4 * MIX_W).astype(BF16)
    return dense, lru_gate_b.reshape(depth, 1, 4 * MIX_W)


def kernel(x, c, ctx, c_ctx, w_ada, b_ada, ln1, w_in, b_in, hg_lb_raw, hg_norm, ml_norm, conv_w, conv_b,
           lru_gate_w, lru_gate_b, lru_lambda, w_branch, w_out, ln2, w_ffn_in, w_ffn_out, final_norm):
    bsz, seq, d_model = x.shape
    lc = ctx.shape[1]
    depth = w_ada.shape[0]
    rows = seq // GRID_W
    assert lc % CHUNK == 0 and seq % lc == 0 and seq % FFN_TM == 0 and bsz < 16

    c_all = jnp.zeros((16, d_model), F32).at[:bsz].set(c).at[bsz].set(c_ctx)
    mods4 = _ada_call(c_all, w_ada, b_ada).reshape(depth, 16, 1, 6 * d_model)
    lb_all = _lb_call(hg_lb_raw)
    lb3 = lb_all.reshape(depth, 1, 2 * MIX_W)
    lb4 = lb_all.reshape(depth, 2, 1, MIX_W)

    ws, bs, out_dtypes, kinds = _prep_in_proj(w_in, b_in)
    wgate, bgate = _prep_lru_gates(lru_gate_w, lru_gate_b)
    w_ffn_in_b = w_ffn_in.astype(BF16)
    w_ffn_out_b = w_ffn_out.astype(BF16)
    w_branch_b = w_branch.astype(BF16)
    w_out_b = w_out.astype(BF16)
    ln1_3 = ln1.reshape(depth, 1, d_model)
    ln2_3 = ln2.reshape(depth, 1, d_model)
    hgn = hg_norm.reshape(depth, 1, MIX_W)
    mln = ml_norm.reshape(depth, 1, MIX_W)
    conv_b3 = conv_b.reshape(depth, 1, MIX_W)
    fin = final_norm.reshape(1, d_model)

    for l in range(depth):
        last = l == depth - 1
        n_inner = GRID_W if l % 2 == 0 else rows
        hg_a, hg_f, ml_a, ml_g, lru_x, lru_y, mg = _in_proj_call(l, ctx, x, mods4, ln1_3, lb3, ws, bs,
                                                                 out_dtypes, kinds)
        o_hg, o_ml = _mixers_call(l, hg_a, hg_f, lb4, ml_a, ml_g, lc)
        a0, u0, a1, u1 = _lru_gates_call(l, lru_x, conv_w, conv_b3, wgate, bgate, lru_lambda, lc)
        h_f, h_b = _lru_scan_call(a0, u0, a1, u1, lc)
        ctx_m, x_m = _merge_call(l, last, o_hg, o_ml, h_f, h_b, hg_a, ml_a, lru_y, mg, ctx, x, mods4,
                                 hgn, mln, w_branch_b, w_out_b)
        x = _ffn_call(l, x_m, mods4, False, ln2_3, w_ffn_in_b, w_ffn_out_b, fin,
                      tm=FFN_TM, n_inner=n_inner, final_norm=last)
        if not last:
            ctx = _ffn_call(l, ctx_m, mods4, True, ln2_3, w_ffn_in_b, w_ffn_out_b, fin,
                            tm=lc, n_inner=None, final_norm=False)
    return x
```

```python
import functools
import itertools

import numpy as np
import jax
import jax.numpy as jnp
from jax import lax
from jax.experimental import pallas as pl
from jax.experimental.pallas import tpu as pltpu

F32 = jnp.float32
BF16 = jnp.bfloat16

GRID_W = 64
MIX_W = 512
N_HEADS = 4
HEAD_DV = 128
HG_DK = 128
ML_DQK = 64
LRU_BLOCKS = 8
LRU_BD = 64
LRU_C = 8.0
EPS = 1e-6
NEG_BIG = -1e30
LB_TINY = 1e-30
LOG2_E = 1.4426950408889634
GATE_LANES = 128
CHUNK = 128
HG_GROUP = 1
MIX_NB = 4
MIX_SKEW = 14
FFN_TM = 512
FFN_ROWS = 256
MERGE_ROWS = 128
IN_PROJ_ROWS = 128
VMEM_LIMIT = 56 * 1024 * 1024


def _cparams(sem):
    return pltpu.CompilerParams(dimension_semantics=sem, vmem_limit_bytes=VMEM_LIMIT)


def _resident(shape, index_map):
    return pl.BlockSpec(shape, index_map, pipeline_mode=pl.Buffered(1))


def _sigmoid(x):
    return 1.0 / (1.0 + jnp.exp(-x))


def _sigmoid_t(x):
    return 0.5 * jnp.tanh(0.5 * x) + 0.5


def _log_sigmoid(x):
    return jnp.minimum(x, 0.0) - jnp.log(1.0 + jnp.exp(-jnp.abs(x)))


def _dot(a, b):
    return jnp.dot(a, b, preferred_element_type=F32)


def _dot_nt(a, b):
    return lax.dot_general(a, b, (((1,), (1,)), ((), ())), preferred_element_type=F32)


def _dot_tn(a, b):
    return lax.dot_general(a, b, (((0,), (0,)), ((), ())), preferred_element_type=F32)


def _split_hi_lo(x):
    hi = x.astype(BF16)
    lo = (x - hi.astype(F32)).astype(BF16)
    return hi, lo


def _norm_mod(x, ln, shift, scale):
    y = x * lax.rsqrt(jnp.mean(x * x, axis=-1, keepdims=True) + EPS) * ln
    return y * (1.0 + scale) + shift


def _mirror(m):
    return m[..., ::-1, ::-1].copy()


@functools.lru_cache(maxsize=None)
def _hgrn2_consts(C):
    n_lv = int(np.log2(C))
    mats, masks = [], []
    r = np.arange(C)
    for lv in range(n_lv):
        s = C >> (lv + 1)
        base = (r // (2 * s)) * (2 * s)
        mid = base + s - 1
        odd = r >= base + s
        m = np.zeros((C, C), np.float32)
        for t in range(C):
            if odd[t]:
                m[t, mid[t] + 1:t + 1] = 1.0
            else:
                m[t, t + 1:mid[t] + 1] = 1.0
        mats.append(m)
        same = base[:, None] == base[None, :]
        masks.append((same & odd[:, None] & (~odd)[None, :]).astype(np.float32))
    mats.append(np.tril(np.ones((C, C), np.float32)))
    masks.append(np.eye(C, dtype=np.float32))
    mats = np.stack(mats)
    masks = np.stack(masks)
    mats = np.stack([mats, _mirror(mats)])
    masks = np.stack([masks, _mirror(masks)])
    mst = mats.reshape(2, (n_lv + 1) * C, C)
    mst = np.concatenate([mst, mst], axis=-1)
    return mst, masks, n_lv


@functools.lru_cache(maxsize=None)
def _mlstm_consts(C):
    tri = np.tril(np.ones((C, C), np.float32))
    tri = np.stack([tri, _mirror(tri)])
    tri_cat = np.concatenate([tri, tri], axis=-1)
    tri_t = np.transpose(tri, (0, 2, 1))
    tri_t_cat = np.concatenate([tri_t, tri_t], axis=1)
    return tri_cat, tri_t_cat, tri


def _chunk_index(d, j, nctx, nch):
    bw = jnp.where(j < nctx, nctx - 1 - j, nch - 1 + nctx - j)
    return jnp.where(d == 0, j, bw)


def _ada_body(c_ref, w_ref, b_ref, o_ref):
    cc = c_ref[...]
    s = cc * _sigmoid(cc)
    o_ref[...] = jnp.dot(s, w_ref[...], preferred_element_type=F32,
                         precision=lax.Precision.HIGHEST) + b_ref[...]


def _ada_call(c_all, w_ada, b_ada):
    depth, d_model, n6 = w_ada.shape
    rows = c_all.shape[0]
    tn = 1536
    return pl.pallas_call(
        _ada_body,
        grid=(depth, n6 // tn),
        in_specs=[
            pl.BlockSpec((rows, d_model), lambda l, n: (0, 0)),
            pl.BlockSpec((None, d_model, tn), lambda l, n: (l, 0, n)),
            pl.BlockSpec((None, 1, tn), lambda l, n: (l, 0, n)),
        ],
        out_specs=pl.BlockSpec((None, rows, tn), lambda l, n: (l, 0, n)),
        out_shape=jax.ShapeDtypeStruct((depth, rows, n6), F32),
        compiler_params=_cparams(("arbitrary", "arbitrary")),
        name="ada_mod",
    )(c_all, w_ada, b_ada.reshape(depth, 1, n6))


def _lb_body(raw_ref, o_ref):
    raw = raw_ref[...]
    depth = raw.shape[0]
    e = jnp.exp(raw - jnp.max(raw, axis=0, keepdims=True))
    p = e / jnp.sum(e, axis=0, keepdims=True)
    acc = jnp.zeros_like(p[0:1])
    for l in range(depth):
        acc = acc + p[l:l + 1]
        o_ref[l:l + 1, :] = acc - p[0:1]


def _lb_call(hg_lb_raw):
    depth = hg_lb_raw.shape[0]
    raw = hg_lb_raw.reshape(depth, -1)
    return pl.pallas_call(
        _lb_body,
        out_shape=jax.ShapeDtypeStruct(raw.shape, F32),
        name="hgrn2_lower_bounds",
    )(raw)


def _log2_forget(f_pre, lb):
    e = jnp.exp(-jnp.abs(f_pre))
    inv = 1.0 / (1.0 + e)
    sig = jnp.where(f_pre >= 0.0, inv, e * inv)
    return jnp.log(jnp.maximum(lb, LB_TINY) + (1.0 - lb) * sig) * LOG2_E


def _in_proj_body(ctx_ref, x_ref, sh_ref, sc_ref, ln_ref, lb_ref, *rest, kinds):
    n_out = len(kinds)
    w_refs = rest[:n_out]
    b_refs = rest[n_out:2 * n_out]
    o_refs = rest[2 * n_out:]
    t = pl.program_id(1)
    tm = x_ref.shape[0]
    rows = min(tm, IN_PROJ_ROWS)

    def finish(kind, y, o_ref, rs):
        if kind == "hgrn2_qig":
            q = y[:, :MIX_W]
            o_ref[rs, :MIX_W] = (q * _sigmoid_t(q)).astype(o_ref.dtype)
            o_ref[rs, MIX_W:] = y[:, MIX_W:].astype(o_ref.dtype)
        elif kind == "hgrn2_forget":
            o_ref[rs, :] = _log2_forget(y, lb_ref[...]).astype(o_ref.dtype)
        elif kind == "mlstm_gates":
            lane = lax.broadcasted_iota(jnp.int32, y.shape, 1) % GATE_LANES
            is_f = jnp.logical_and(lane >= N_HEADS, lane < 2 * N_HEADS)
            o_ref[rs, :] = (jnp.where(is_f, _log_sigmoid(y), y) * LOG2_E).astype(o_ref.dtype)
        else:
            o_ref[rs, :] = y.astype(o_ref.dtype)

    def row_group(r0):
        rs = slice(r0, r0 + rows)
        xin = jnp.where(t == 0, ctx_ref[rs, :], x_ref[rs, :])
        h = _norm_mod(xin, ln_ref[...], sh_ref[...], sc_ref[...]).astype(BF16)
        yield
        for kind, w_ref, b_ref, o_ref in zip(kinds, w_refs, b_refs, o_refs):
            finish(kind, _dot(h, w_ref[...]) + b_ref[...], o_ref, rs)
            yield

    gens = [row_group(r0) for r0 in range(0, tm, rows)]
    n_phase = 1 + n_out
    for step in range(n_phase + len(gens) - 1):
        for i, g in enumerate(gens):
            if 0 <= step - i < n_phase:
                next(g)


def _in_proj_call(l, ctx, x, mods4, ln1, lb3, ws, bs, out_dtypes, kinds):
    bsz, lc, d_model = ctx.shape
    seq = x.shape[1]
    tm = lc
    nt = 1 + seq // tm
    n_out = len(ws)

    def mod_spec(chunk):
        return pl.BlockSpec((None, None, 1, d_model),
                            lambda b, t: (l, jnp.where(t == 0, bsz, b), 0, chunk))

    in_specs = [
        pl.BlockSpec((None, tm, d_model), lambda b, t: (b, 0, 0)),
        pl.BlockSpec((None, tm, d_model), lambda b, t: (b, jnp.maximum(t - 1, 0), 0)),
        mod_spec(0), mod_spec(1),
        pl.BlockSpec((None, 1, d_model), lambda b, t: (l, 0, 0)),
        pl.BlockSpec((None, 1, lb3.shape[-1]), lambda b, t: (l, 0, 0)),
    ]
    in_specs += [pl.BlockSpec((None, d_model, w.shape[-1]), lambda b, t: (l, 0, 0)) for w in ws]
    in_specs += [pl.BlockSpec((None, 1, w.shape[-1]), lambda b, t: (l, 0, 0)) for w in ws]
    out_specs = [pl.BlockSpec((None, tm, w.shape[-1]), lambda b, t: (b, t, 0)) for w in ws]
    out_shape = [jax.ShapeDtypeStruct((bsz, lc + seq, w.shape[-1]), dt) for w, dt in zip(ws, out_dtypes)]
    return pl.pallas_call(
        functools.partial(_in_proj_body, kinds=kinds),
        grid=(bsz, nt),
        in_specs=in_specs,
        out_specs=out_specs,
        out_shape=out_shape,
        compiler_params=_cparams(("arbitrary", "arbitrary")),
        name="in_proj",
    )(ctx, x, mods4, mods4, ln1, lb3, *ws, *bs)


def _hgrn2_body(qv_ref, g_ref, lb_ref, mst_ref, msk_ref, o_ref, st_ref, *, C, n_lv):
    q16 = qv_ref[:, :MIX_W]
    v = qv_ref[:, MIX_W:]
    q = q16.astype(F32)
    g = g_ref[...]
    lb = lb_ref[...]
    kk = (1.0 - jnp.exp2(g)) + (jnp.maximum(lb, LB_TINY) - lb)
    g_hi, g_lo = _split_hi_lo(g)
    ex = _dot(mst_ref[...], jnp.concatenate([g_hi, g_lo], axis=0))
    tot = jnp.sum(g, axis=0, keepdims=True)
    k16 = kk.astype(BF16)

    for h0 in range(0, N_HEADS, HG_GROUP):
        heads = range(h0, h0 + HG_GROUP)
        sls = {h: slice(h * HG_DK, (h + 1) * HG_DK) for h in heads}
        p = {h: msk_ref[n_lv] * _dot_nt(q16[:, sls[h]], k16[:, sls[h]]) for h in heads}
        for lv in range(n_lv):
            for h in heads:
                w = jnp.exp2(ex[lv * C:(lv + 1) * C, sls[h]]).astype(BF16)
                p[h] = p[h] + msk_ref[lv] * _dot_nt(q16[:, sls[h]] * w, k16[:, sls[h]] * w)
            if lv % 2 == 1:
                yield
        for h in heads:
            sl = sls[h]
            st = st_ref[h]
            b_in = ex[n_lv * C:(n_lv + 1) * C, sl]
            qb = (q[:, sl] * jnp.exp2(b_in)).astype(BF16)
            vh = v[:, sl]
            o_ref[:, sl] = (_dot(p[h].astype(BF16), vh) + _dot_nt(qb, st.astype(BF16))).astype(o_ref.dtype)
            tot_h = tot[:, sl]
            kb = (kk[:, sl] * jnp.exp2(tot_h - b_in)).astype(BF16)
            st_ref[h] = st * jnp.exp2(tot_h) + _dot_tn(vh, kb)
            yield


def _mlstm_body(qkv_ref, g_ref, tri_ref, trit_ref, msk_ref, o_ref, c_ref, m_ref, *, C):
    qk_w = N_HEADS * ML_DQK
    gates = g_ref[...]
    gates_t = gates.T
    lf_hi, lf_lo = _split_hi_lo(gates)
    b_cols = _dot(tri_ref[...], jnp.concatenate([lf_hi, lf_lo], axis=0))
    lft_hi, lft_lo = _split_hi_lo(gates_t)
    b_rows = _dot(jnp.concatenate([lft_hi, lft_lo], axis=1), trit_ref[...])
    tot = jnp.sum(gates, axis=0, keepdims=True)
    allowed = msk_ref[...] > 0.0
    ones = jnp.ones((C, HEAD_DV), BF16)
    log2_scale = 0.5 * np.log2(ML_DQK)

    for h in range(N_HEADS):
        b_col = b_cols[:, N_HEADS + h:N_HEADS + h + 1]
        c_col = gates[:, h:h + 1] - b_col
        c_row = gates_t[h:h + 1, :] - b_rows[N_HEADS + h:N_HEADS + h + 1, :]
        m_prev = m_ref[h][0:1, 0:1]
        cm = jnp.where(allowed, c_row, NEG_BIG)
        m_t = jnp.maximum(jnp.max(cm, axis=-1, keepdims=True), m_prev)
        yield
        qh = qkv_ref[:, h * ML_DQK:(h + 1) * ML_DQK]
        kh = qkv_ref[:, qk_w + h * ML_DQK:qk_w + (h + 1) * ML_DQK]
        vh = qkv_ref[:, 2 * qk_w + h * HEAD_DV:2 * qk_w + (h + 1) * HEAD_DV]
        vaug = jnp.concatenate([vh, ones], axis=1)
        s = _dot_nt(qh, kh) * jnp.exp2(cm - (m_t + log2_scale))
        qs = (qh.astype(F32) * jnp.exp2(m_prev - m_t - log2_scale)).astype(BF16)
        yield
        c_aug = c_ref[h]
        r = _dot(jnp.concatenate([s.astype(BF16), qs], axis=1),
                 jnp.concatenate([vaug, c_aug.astype(BF16)], axis=0))
        num = r[:, :HEAD_DV]
        den = r[:, HEAD_DV:]
        hout = num / jnp.maximum(jnp.abs(den), jnp.exp2(-(b_col + m_t)))
        o_ref[:, h * HEAD_DV:(h + 1) * HEAD_DV] = hout.astype(o_ref.dtype)
        yield
        b_last = tot[:, N_HEADS + h:N_HEADS + h + 1]
        log_w = b_last + c_col
        m_new = jnp.maximum(b_last + m_prev, jnp.max(log_w, axis=0, keepdims=True))
        kw = (kh.astype(F32) * jnp.exp2(log_w - m_new)).astype(BF16)
        c_ref[h] = jnp.exp2(b_last + m_prev - m_new) * c_aug + _dot_tn(kw, vaug)
        m_ref[h] = jnp.broadcast_to(m_new, m_ref.shape[1:])
        yield


def _mixers_body(hqv_ref, hg_ref, lb_ref, mst_ref, hmsk_ref, mqkv_ref, mg_ref,
                 tri_ref, trit_ref, mmsk_ref, o_hg_ref, o_ml_ref, st_ref, c_ref, m_ref, *, C, n_lv):
    @pl.when(pl.program_id(2) == 0)
    def _():
        st_ref[...] = jnp.zeros_like(st_ref)
        c_ref[...] = jnp.zeros_like(c_ref)
        m_ref[...] = jnp.zeros_like(m_ref)

    def element(bb):
        gens = [_hgrn2_body(hqv_ref.at[bb], hg_ref.at[bb], lb_ref, mst_ref, hmsk_ref,
                            o_hg_ref.at[bb], st_ref.at[bb], C=C, n_lv=n_lv),
                _mlstm_body(mqkv_ref.at[bb], mg_ref.at[bb], tri_ref, trit_ref,
                            mmsk_ref, o_ml_ref.at[bb], c_ref.at[bb], m_ref.at[bb], C=C)]
        for _ in itertools.zip_longest(*gens):
            yield

    streams = [element(bb) for bb in range(hqv_ref.shape[0])]
    live = []
    while streams or live:
        if streams and (not live or live[-1][1] >= MIX_SKEW):
            live.append([streams.pop(0), 0])
        for item in list(live):
            try:
                next(item[0])
                item[1] += 1
            except StopIteration:
                live.remove(item)


def _mixers_call(l, hg_a, hg_f, lb4, ml_a, ml_g, lc):
    bsz, tt, _ = hg_a.shape
    C = CHUNK
    nch, nctx = tt // C, lc // C
    mst, hmsk, n_lv = _hgrn2_consts(C)
    mst = jnp.asarray(mst, BF16)
    hmsk = jnp.asarray(hmsk, F32)
    tri_cat, tri_t_cat, tri = _mlstm_consts(C)
    tri_cat = jnp.asarray(tri_cat, BF16)
    tri_t_cat = jnp.asarray(tri_t_cat, BF16)
    mmsk = jnp.asarray(tri, F32)
    cidx = functools.partial(_chunk_index, nctx=nctx, nch=nch)
    qk_w = N_HEADS * ML_DQK
    tok = lambda col: (lambda b, d, j: (b, cidx(d, j), col))
    tok_d = lambda b, d, j: (b, cidx(d, j), d)
    per_dir3 = lambda b, d, j: (d, 0, 0)
    nb = max(n for n in range(1, MIX_NB + 1) if bsz % n == 0)
    o_spec = pl.BlockSpec((nb, None, C, MIX_W), lambda b, d, j: (b, d, cidx(d, j), 0))
    o_shape = jax.ShapeDtypeStruct((bsz, 2, tt, MIX_W), BF16)
    return pl.pallas_call(
        functools.partial(_mixers_body, C=C, n_lv=n_lv),
        grid=(bsz // nb, 2, nch),
        in_specs=[
            pl.BlockSpec((nb, C, 2 * MIX_W), tok(0)),
            pl.BlockSpec((nb, C, MIX_W), tok_d),
            pl.BlockSpec((None, None, 1, MIX_W), lambda b, d, j: (l, d, 0, 0)),
            pl.BlockSpec((None,) + mst.shape[1:], per_dir3),
            pl.BlockSpec((None,) + hmsk.shape[1:], lambda b, d, j: (d, 0, 0, 0)),
            pl.BlockSpec((nb, C, 2 * qk_w + MIX_W), tok(0)),
            pl.BlockSpec((nb, C, GATE_LANES), tok_d),
            pl.BlockSpec((None, C, 2 * C), per_dir3),
            pl.BlockSpec((None, 2 * C, C), per_dir3),
            pl.BlockSpec((None, C, C), per_dir3),
        ],
        out_specs=[o_spec, o_spec],
        out_shape=[o_shape, o_shape],
        scratch_shapes=[pltpu.VMEM((nb, N_HEADS, HEAD_DV, HG_DK), F32),
                        pltpu.VMEM((nb, N_HEADS, ML_DQK, 2 * HEAD_DV), F32),
                        pltpu.VMEM((nb, N_HEADS, 8, 128), F32)],
        compiler_params=_cparams(("arbitrary", "arbitrary", "arbitrary")),
        name="mixers",
    )(hg_a, hg_f, lb4, mst, hmsk, ml_a, ml_g, tri_cat, tri_t_cat, mmsk)


def _lru_gates_body(x_ref, xp_ref, xn_ref, cw_ref, cb_ref, wg_ref, bg_ref, lam_ref,
                    a0_ref, u0_ref, a1_ref, u1_ref, ext_ref, *, tc, nctx_t, nt):
    t = pl.program_id(1)
    prev_ok = jnp.logical_and(t != 0, t != nctx_t)
    next_ok = jnp.logical_and(t != nctx_t - 1, t != nt - 1)
    x = x_ref[...]
    ext_ref[0:8, :] = jnp.where(prev_ok, xp_ref[...], 0.0)
    ext_ref[8:8 + tc, :] = x
    ext_ref[8 + tc:16 + tc, :] = jnp.where(next_ok, xn_ref[...], 0.0)
    cw = cw_ref[...]
    xc = (cw[0:1] * ext_ref[6:6 + tc, :] + cw[1:2] * ext_ref[7:7 + tc, :] + cw[2:3] * x
          + cw[3:4] * ext_ref[9:9 + tc, :]) + cb_ref[...]
    gates = _dot(xc.astype(BF16), wg_ref[...]) + bg_ref[...]
    outs = ((a0_ref, u0_ref), (a1_ref, u1_ref))
    for d in range(2):
        r_pre = gates[:, (2 * d) * MIX_W:(2 * d + 1) * MIX_W]
        i_pre = gates[:, (2 * d + 1) * MIX_W:(2 * d + 2) * MIX_W]
        z = -lam_ref[d:d + 1, :]
        softplus = jnp.maximum(z, 0.0) + jnp.log(1.0 + jnp.exp(-jnp.abs(z)))
        log_a = (-LRU_C * softplus) * _sigmoid_t(r_pre)
        a_ref, u_ref = outs[d]
        a = jnp.exp(log_a)
        a_ref[...] = a
        one_m_a2 = -jnp.tanh(log_a) * (a * a + 1.0)
        u_ref[...] = jnp.sqrt(jnp.maximum(one_m_a2, 0.0)) * (_sigmoid_t(i_pre) * xc)


def _lru_gates_call(l, lru_x, conv_w, conv_b, wg, bg, lam, lc):
    bsz, tt, w = lru_x.shape
    tc = lc
    nt, nctx_t = tt // tc, lc // tc
    h8 = tc // 8
    out = jax.ShapeDtypeStruct((bsz, tt, w), F32)
    o_spec = pl.BlockSpec((None, tc, w), lambda b, t: (b, t, 0))
    return pl.pallas_call(
        functools.partial(_lru_gates_body, tc=tc, nctx_t=nctx_t, nt=nt),
        grid=(bsz, nt),
        in_specs=[
            pl.BlockSpec((None, tc, w), lambda b, t: (b, t, 0)),
            pl.BlockSpec((None, 8, w), lambda b, t: (b, jnp.maximum(t * h8 - 1, 0), 0)),
            pl.BlockSpec((None, 8, w), lambda b, t: (b, jnp.minimum((t + 1) * h8, tt // 8 - 1), 0)),
            pl.BlockSpec((None, 4, w), lambda b, t: (l, 0, 0)),
            pl.BlockSpec((None, 1, w), lambda b, t: (l, 0, 0)),
            pl.BlockSpec((None, w, 4 * w), lambda b, t: (l, 0, 0)),
            pl.BlockSpec((None, 1, 4 * w), lambda b, t: (l, 0, 0)),
            pl.BlockSpec((None, 2, w), lambda b, t: (l, 0, 0)),
        ],
        out_specs=[o_spec] * 4,
        out_shape=[out] * 4,
        scratch_shapes=[pltpu.VMEM((tc + 16, w), F32)],
        compiler_params=_cparams(("arbitrary", "arbitrary")),
        name="rglru_gates",
    )(lru_x, lru_x, lru_x, conv_w, conv_b, wg, bg, lam)


def _lru_scan_body(a0_ref, u0_ref, a1_ref, u1_ref, hf_ref, hb_ref, sf_ref, sb_ref, *, tc):
    @pl.when(pl.program_id(0) == 0)
    def _():
        sf_ref[...] = jnp.zeros_like(sf_ref)
        sb_ref[...] = jnp.zeros_like(sb_ref)

    def step(i, carry):
        hf, hb = carry
        hf = a0_ref[:, i, :] * hf + u0_ref[:, i, :]
        hf_ref[:, i, :] = hf
        ib = tc - 1 - i
        hb = a1_ref[:, ib, :] * hb + u1_ref[:, ib, :]
        hb_ref[:, ib, :] = hb
        return hf, hb

    hf, hb = lax.fori_loop(0, tc, step, (sf_ref[...], sb_ref[...]), unroll=8)
    sf_ref[...] = hf
    sb_ref[...] = hb


def _lru_scan_call(a0, u0, a1, u1, lc):
    bsz, tt, w = a0.shape
    tc = CHUNK
    nch, nctx = tt // tc, lc // tc
    fw = pl.BlockSpec((bsz, tc, w), lambda j: (0, j, 0))
    bw = pl.BlockSpec((bsz, tc, w), lambda j: (0, _chunk_index(1, j, nctx, nch), 0))
    out = jax.ShapeDtypeStruct((bsz, tt, w), F32)
    return pl.pallas_call(
        functools.partial(_lru_scan_body, tc=tc),
        grid=(nch,),
        in_specs=[fw, fw, bw, bw],
        out_specs=[fw, bw],
        out_shape=[out, out],
        scratch_shapes=[pltpu.VMEM((bsz, w), F32), pltpu.VMEM((bsz, w), F32)],
        compiler_params=_cparams(("arbitrary",)),
        name="rglru_scan",
    )(a0, u0, a1, u1)


def _head_rms(o, w):
    parts = []
    for h in range(N_HEADS):
        oh = o[:, h * HEAD_DV:(h + 1) * HEAD_DV]
        parts.append(oh * lax.rsqrt(jnp.mean(oh * oh, axis=-1, keepdims=True) + EPS))
    return jnp.concatenate(parts, axis=1) * w


def _gelu_tanh(x):
    return 0.5 * x * (1.0 + jnp.tanh(0.7978845608028654 * (x + 0.044715 * (x * x * x))))


def _merge_rows(hg0_ref, hg1_ref, ml0_ref, ml1_ref, lf_ref, lb_ref, hgg_ref, mlo_ref, ly_ref, mg_ref,
                res_ref, gate_ref, hgn_ref, mln_ref, wb_ref, wo_ref, o_ref, r0, rows, d_model):
    rs = slice(r0, r0 + rows)
    hgg = hgg_ref[rs, :].astype(F32)
    o_hg = hg0_ref[rs, :].astype(F32) + hg1_ref[rs, :].astype(F32)
    o_ml = ml0_ref[rs, :].astype(F32) + ml1_ref[rs, :].astype(F32)
    a_out = _head_rms(o_hg, hgn_ref[...]) * (hgg * _sigmoid_t(hgg))
    b_out = _head_rms(o_ml, mln_ref[...]) * _sigmoid_t(mlo_ref[rs, :].astype(F32))
    c_out = (lf_ref[rs, :] + lb_ref[rs, :]) * _gelu_tanh(ly_ref[rs, :].astype(F32))
    yield
    merged = None
    for n, br in enumerate((a_out, b_out, c_out)):
        gate = _sigmoid_t(mg_ref[rs, n * d_model:(n + 1) * d_model])
        term = gate * _dot(br.astype(BF16), wb_ref[n]).astype(BF16)
        merged = term if merged is None else merged + term
        yield
    o_ref[rs, :] = res_ref[rs, :] + gate_ref[...] * _dot(merged, wo_ref[...])
    yield


def _merge_body(*refs, rows, d_model):
    tm = refs[-1].shape[0]
    gens = [_merge_rows(*refs, r0, rows, d_model) for r0 in range(0, tm, rows)]
    n_phase = 5
    for step in range(n_phase + len(gens) - 1):
        for i, g in enumerate(gens):
            if 0 <= step - i < n_phase:
                next(g)


def _merge_call(l, is_ctx, tm, o_hg, o_ml, h_f, h_b, hg_a, ml_a, lru_y, mg, res, mods4,
                hg_norm, ml_norm, w_branch, w_out):
    bsz, n, d_model = res.shape
    t0 = 0 if is_ctx else 1
    rows = min(tm, MERGE_ROWS)

    def tok(col):
        return lambda b, t: (b, t + t0, col)

    mix = lambda d: pl.BlockSpec((None, None, tm, MIX_W), lambda b, t: (b, d, t + t0, 0))
    in_specs = [
        mix(0), mix(1), mix(0), mix(1),
        pl.BlockSpec((None, tm, MIX_W), tok(0)),
        pl.BlockSpec((None, tm, MIX_W), tok(0)),
        pl.BlockSpec((None, tm, MIX_W), tok(2)),
        pl.BlockSpec((None, tm, MIX_W), tok(2)),
        pl.BlockSpec((None, tm, MIX_W), tok(0)),
        pl.BlockSpec((None, tm, 3 * d_model), tok(0)),
        pl.BlockSpec((None, tm, d_model), lambda b, t: (b, t, 0)),
        pl.BlockSpec((None, None, 1, d_model), lambda b, t: (l, bsz if is_ctx else b, 0, 2)),
        pl.BlockSpec((None, 1, MIX_W), lambda b, t: (l, 0, 0)),
        pl.BlockSpec((None, 1, MIX_W), lambda b, t: (l, 0, 0)),
        _resident((None, 3, MIX_W, d_model), lambda b, t: (l, 0, 0, 0)),
        _resident((None, d_model, d_model), lambda b, t: (l, 0, 0)),
    ]
    return pl.pallas_call(
        functools.partial(_merge_body, rows=rows, d_model=d_model),
        grid=(bsz, n // tm),
        in_specs=in_specs,
        out_specs=pl.BlockSpec((None, tm, d_model), lambda b, t: (b, t, 0)),
        out_shape=jax.ShapeDtypeStruct(res.shape, F32),
        compiler_params=_cparams(("arbitrary", "arbitrary")),
        name="merge",
    )(o_hg, o_hg, o_ml, o_ml, h_f, h_b, hg_a, ml_a, lru_y, mg, res, mods4,
      hg_norm, ml_norm, w_branch, w_out)


def _ffn_rows(x_ref, sh_ref, sc_ref, gt_ref, ln_ref, wi_ref, wo_ref, fn_ref, o_ref, r0, rows, *,
              n_inner, final_norm):
    x = x_ref[r0:r0 + rows, :]
    h = _norm_mod(x, ln_ref[...], sh_ref[...], sc_ref[...]).astype(BF16)
    yield
    gu = _dot(h, wi_ref[...])
    yield
    hidden = gu.shape[-1] // 2
    gate = gu[:, :hidden]
    act = (gate * _sigmoid_t(gate) * gu[:, hidden:]).astype(BF16)
    yield
    f = _dot(act, wo_ref[...])
    yield
    y = x + gt_ref[...] * f
    if final_norm:
        y = y * lax.rsqrt(jnp.mean(y * y, axis=-1, keepdims=True) + EPS) * fn_ref[...]
    if n_inner is None:
        o_ref[r0:r0 + rows, :] = y
    else:
        for a in range(rows // n_inner):
            o_ref[:, r0 // n_inner + a, :] = y[a * n_inner:(a + 1) * n_inner, :]
    yield


def _ffn_body(x_ref, sh_ref, sc_ref, gt_ref, ln_ref, wi_ref, wo_ref, fn_ref, o_ref, *, rows, n_inner,
              final_norm):
    gens = [_ffn_rows(x_ref, sh_ref, sc_ref, gt_ref, ln_ref, wi_ref, wo_ref, fn_ref, o_ref, r0, rows,
                      n_inner=n_inner, final_norm=final_norm)
            for r0 in range(0, x_ref.shape[0], rows)]
    skew = 2
    n_phase = 5
    for step in range(n_phase + skew * (len(gens) - 1)):
        for i, g in enumerate(gens):
            if 0 <= step - skew * i < n_phase:
                next(g)


def _ffn_call(l, x, mods4, ctx_row, ln2, wi, wo, fin, *, tm, n_inner, final_norm):
    bsz, n, d_model = x.shape
    hidden = wo.shape[1]
    nt = n // tm
    rows = min(tm, FFN_ROWS)

    def mod_spec(chunk):
        return pl.BlockSpec((None, None, 1, d_model),
                            lambda b, t: (l, bsz if ctx_row else b, 0, chunk))

    if n_inner is None:
        o_spec = pl.BlockSpec((None, tm, d_model), lambda b, t: (b, t, 0))
        o_shape = jax.ShapeDtypeStruct(x.shape, F32)
    else:
        assert rows % n_inner == 0
        n_a = tm // n_inner
        o_spec = pl.BlockSpec((None, n_inner, n_a, d_model), lambda b, t: (b, 0, t, 0))
        o_shape = jax.ShapeDtypeStruct((bsz, n_inner, n // n_inner, d_model), F32)
    out = pl.pallas_call(
        functools.partial(_ffn_body, rows=rows, n_inner=n_inner, final_norm=final_norm),
        grid=(bsz, nt),
        in_specs=[
            pl.BlockSpec((None, tm, d_model), lambda b, t: (b, t, 0)),
            mod_spec(3), mod_spec(4), mod_spec(5),
            pl.BlockSpec((None, 1, d_model), lambda b, t: (l, 0, 0)),
            _resident((None, d_model, 2 * hidden), lambda b, t: (l, 0, 0)),
            _resident((None, hidden, d_model), lambda b, t: (l, 0, 0)),
            pl.BlockSpec((1, d_model), lambda b, t: (0, 0)),
        ],
        out_specs=o_spec,
        out_shape=o_shape,
        compiler_params=_cparams(("arbitrary", "arbitrary")),
        name="ffn",
    )(x, mods4, mods4, mods4, ln2, wi, wo, fin)
    return out.reshape(bsz, n, d_model)


def _prep_in_proj(w_in, b_in):
    depth, d_model, _ = w_in.shape
    hq = N_HEADS * HG_DK
    sizes = (hq, MIX_W, MIX_W, 2 * hq, N_HEADS * ML_DQK, N_HEADS * ML_DQK, MIX_W, MIX_W,
             2 * N_HEADS, 2 * N_HEADS, MIX_W, MIX_W, 3 * d_model)
    offs = np.concatenate([[0], np.cumsum(sizes)])
    assert offs[-1] == w_in.shape[-1]

    def cols(a, lo, hi):
        return a[..., int(offs[lo]):int(offs[hi])]

    def gate_block(a):
        ig = cols(a, 8, 9).reshape(a.shape[:-1] + (2, N_HEADS))
        fg = cols(a, 9, 10).reshape(a.shape[:-1] + (2, N_HEADS))
        pad = jnp.zeros(a.shape[:-1] + (2, GATE_LANES - 2 * N_HEADS), a.dtype)
        return jnp.concatenate([ig, fg, pad], axis=-1).reshape(a.shape[:-1] + (2 * GATE_LANES,))

    groups = [
        (lambda a: cols(a, 0, 3), BF16),
        (lambda a: cols(a, 3, 4), F32),
        (lambda a: cols(a, 4, 8), BF16),
        (gate_block, F32),
        (lambda a: cols(a, 10, 11), F32),
        (lambda a: cols(a, 11, 12), BF16),
        (lambda a: cols(a, 12, 13), BF16),
    ]
    kinds = ("hgrn2_qig", "hgrn2_forget", None, "mlstm_gates", None, None, None)
    b3 = b_in.reshape(depth, 1, -1)
    ws = [f(w_in).astype(BF16) for f, _ in groups]
    bs = [f(b3) for f, _ in groups]
    return ws, bs, [dt for _, dt in groups], kinds


def _prep_lru_gates(lru_gate_w, lru_gate_b):
    depth = lru_gate_w.shape[0]
    eye = jnp.eye(LRU_BLOCKS, dtype=lru_gate_w.dtype)
    dense = jnp.einsum("lzgnde,nm->lndzgme", lru_gate_w, eye)
    dense = dense.reshape(depth, MIX_W, 4 * MIX_W).astype(BF16)
    return dense, lru_gate_b.reshape(depth, 1, 4 * MIX_W)


def kernel(x, c, ctx, c_ctx, w_ada, b_ada, ln1, w_in, b_in, hg_lb_raw, hg_norm, ml_norm, conv_w, conv_b,
           lru_gate_w, lru_gate_b, lru_lambda, w_branch, w_out, ln2, w_ffn_in, w_ffn_out, final_norm):
    bsz, seq, d_model = x.shape
    lc = ctx.shape[1]
    depth = w_ada.shape[0]
    rows = seq // GRID_W
    assert lc % CHUNK == 0 and seq % lc == 0 and seq % FFN_TM == 0 and bsz < 16

    c_all = jnp.zeros((16, d_model), F32).at[:bsz].set(c).at[bsz].set(c_ctx)
    mods4 = _ada_call(c_all, w_ada, b_ada).reshape(depth, 16, 1, 6 * d_model)
    lb_all = _lb_call(hg_lb_raw)
    lb3 = lb_all.reshape(depth, 1, 2 * MIX_W)
    lb4 = lb_all.reshape(depth, 2, 1, MIX_W)

    ws, bs, out_dtypes, kinds = _prep_in_proj(w_in, b_in)
    wgate, bgate = _prep_lru_gates(lru_gate_w, lru_gate_b)
    w_ffn_in_b = w_ffn_in.astype(BF16)
    w_ffn_out_b = w_ffn_out.astype(BF16)
    w_branch_b = w_branch.astype(BF16)
    w_out_b = w_out.astype(BF16)
    ln1_3 = ln1.reshape(depth, 1, d_model)
    ln2_3 = ln2.reshape(depth, 1, d_model)
    hgn = hg_norm.reshape(depth, 1, MIX_W)
    mln = ml_norm.reshape(depth, 1, MIX_W)
    conv_b3 = conv_b.reshape(depth, 1, MIX_W)
    fin = final_norm.reshape(1, d_model)

    for l in range(depth):
        last = l == depth - 1
        n_inner = GRID_W if l % 2 == 0 else rows
        hg_a, hg_f, ml_a, ml_g, lru_x, lru_y, mg = _in_proj_call(l, ctx, x, mods4, ln1_3, lb3, ws, bs,
                                                                 out_dtypes, kinds)
        o_hg, o_ml = _mixers_call(l, hg_a, hg_f, lb4, ml_a, ml_g, lc)
        a0, u0, a1, u1 = _lru_gates_call(l, lru_x, conv_w, conv_b3, wgate, bgate, lru_lambda, lc)
        h_f, h_b = _lru_scan_call(a0, u0, a1, u1, lc)
        branches = (o_hg, o_ml, h_f, h_b, hg_a, ml_a, lru_y, mg)
        merge_w = (hgn, mln, w_branch_b, w_out_b)
        x_m = _merge_call(l, False, lc, *branches, x, mods4, *merge_w)
        x = _ffn_call(l, x_m, mods4, False, ln2_3, w_ffn_in_b, w_ffn_out_b, fin,
                      tm=FFN_TM, n_inner=n_inner, final_norm=last)
        if not last:
            ctx_m = _merge_call(l, True, lc, *branches, ctx, mods4, *merge_w)
            ctx = _ffn_call(l, ctx_m, mods4, True, ln2_3, w_ffn_in_b, w_ffn_out_b, fin,
                            tm=lc, n_inner=None, final_norm=False)
    return x
```

```python
import functools
import itertools

import numpy as np
import jax
import jax.numpy as jnp
from jax import lax
from jax.experimental import pallas as pl
from jax.experimental.pallas import tpu as pltpu

F32 = jnp.float32
BF16 = jnp.bfloat16

GRID_W = 64
MIX_W = 512
N_HEADS = 4
HEAD_DV = 128
HG_DK = 128
ML_DQK = 64
LRU_BLOCKS = 8
LRU_BD = 64
LRU_C = 8.0
EPS = 1e-6
NEG_BIG = -1e30
LB_TINY = 1e-30
LOG2_E = 1.4426950408889634
GATE_LANES = 128
CHUNK = 128
HG_GROUP = 1
MIX_NB = 4
MIX_SKEW = 14
FFN_TM = 512
FFN_ROWS = 256
MERGE_ROWS = 128
IN_PROJ_ROWS = 128
VMEM_LIMIT = 56 * 1024 * 1024


def _cparams(sem):
    return pltpu.CompilerParams(dimension_semantics=sem, vmem_limit_bytes=VMEM_LIMIT)


def _resident(shape, index_map):
    return pl.BlockSpec(shape, index_map, pipeline_mode=pl.Buffered(1))


def _sigmoid(x):
    return 1.0 / (1.0 + jnp.exp(-x))


def _sigmoid_t(x):
    return 0.5 * jnp.tanh(0.5 * x) + 0.5


def _log_sigmoid(x):
    return jnp.minimum(x, 0.0) - jnp.log(1.0 + jnp.exp(-jnp.abs(x)))


def _dot(a, b):
    return jnp.dot(a, b, preferred_element_type=F32)


def _dot_nt(a, b):
    return lax.dot_general(a, b, (((1,), (1,)), ((), ())), preferred_element_type=F32)


def _dot_tn(a, b):
    return lax.dot_general(a, b, (((0,), (0,)), ((), ())), preferred_element_type=F32)


def _split_hi_lo(x):
    hi = x.astype(BF16)
    lo = (x - hi.astype(F32)).astype(BF16)
    return hi, lo


def _norm_mod(x, ln, shift, scale):
    y = x * lax.rsqrt(jnp.mean(x * x, axis=-1, keepdims=True) + EPS) * ln
    return y * (1.0 + scale) + shift


def _mirror(m):
    return m[..., ::-1, ::-1].copy()


@functools.lru_cache(maxsize=None)
def _hgrn2_consts(C):
    n_lv = int(np.log2(C))
    mats, masks = [], []
    r = np.arange(C)
    for lv in range(n_lv):
        s = C >> (lv + 1)
        base = (r // (2 * s)) * (2 * s)
        mid = base + s - 1
        odd = r >= base + s
        m = np.zeros((C, C), np.float32)
        for t in range(C):
            if odd[t]:
                m[t, mid[t] + 1:t + 1] = 1.0
            else:
                m[t, t + 1:mid[t] + 1] = 1.0
        mats.append(m)
        same = base[:, None] == base[None, :]
        masks.append((same & odd[:, None] & (~odd)[None, :]).astype(np.float32))
    mats.append(np.tril(np.ones((C, C), np.float32)))
    masks.append(np.eye(C, dtype=np.float32))
    mats = np.stack(mats)
    masks = np.stack(masks)
    mats = np.stack([mats, _mirror(mats)])
    masks = np.stack([masks, _mirror(masks)])
    mst = mats.reshape(2, (n_lv + 1) * C, C)
    mst = np.concatenate([mst, mst], axis=-1)
    return mst, masks, n_lv


@functools.lru_cache(maxsize=None)
def _mlstm_consts(C):
    tri = np.tril(np.ones((C, C), np.float32))
    tri = np.stack([tri, _mirror(tri)])
    tri_cat = np.concatenate([tri, tri], axis=-1)
    tri_t = np.transpose(tri, (0, 2, 1))
    tri_t_cat = np.concatenate([tri_t, tri_t], axis=1)
    return tri_cat, tri_t_cat, tri


def _chunk_index(d, j, nctx, nch):
    bw = jnp.where(j < nctx, nctx - 1 - j, nch - 1 + nctx - j)
    return jnp.where(d == 0, j, bw)


def _ada_body(c_ref, w_ref, b_ref, o_ref):
    cc = c_ref[...]
    s = cc * _sigmoid(cc)
    o_ref[...] = jnp.dot(s, w_ref[...], preferred_element_type=F32,
                         precision=lax.Precision.HIGHEST) + b_ref[...]


def _ada_call(c_all, w_ada, b_ada):
    depth, d_model, n6 = w_ada.shape
    rows = c_all.shape[0]
    tn = 1536
    return pl.pallas_call(
        _ada_body,
        grid=(depth, n6 // tn),
        in_specs=[
            pl.BlockSpec((rows, d_model), lambda l, n: (0, 0)),
            pl.BlockSpec((None, d_model, tn), lambda l, n: (l, 0, n)),
            pl.BlockSpec((None, 1, tn), lambda l, n: (l, 0, n)),
        ],
        out_specs=pl.BlockSpec((None, rows, tn), lambda l, n: (l, 0, n)),
        out_shape=jax.ShapeDtypeStruct((depth, rows, n6), F32),
        compiler_params=_cparams(("arbitrary", "arbitrary")),
        name="ada_mod",
    )(c_all, w_ada, b_ada.reshape(depth, 1, n6))


def _lb_body(raw_ref, o_ref):
    raw = raw_ref[...]
    depth = raw.shape[0]
    e = jnp.exp(raw - jnp.max(raw, axis=0, keepdims=True))
    p = e / jnp.sum(e, axis=0, keepdims=True)
    acc = jnp.zeros_like(p[0:1])
    for l in range(depth):
        acc = acc + p[l:l + 1]
        o_ref[l:l + 1, :] = acc - p[0:1]


def _lb_call(hg_lb_raw):
    depth = hg_lb_raw.shape[0]
    raw = hg_lb_raw.reshape(depth, -1)
    return pl.pallas_call(
        _lb_body,
        out_shape=jax.ShapeDtypeStruct(raw.shape, F32),
        name="hgrn2_lower_bounds",
    )(raw)


def _log2_forget(f_pre, lb):
    e = jnp.exp(-jnp.abs(f_pre))
    inv = 1.0 / (1.0 + e)
    sig = jnp.where(f_pre >= 0.0, inv, e * inv)
    return jnp.log(jnp.maximum(lb, LB_TINY) + (1.0 - lb) * sig) * LOG2_E


def _in_proj_body(ctx_ref, x_ref, sh_ref, sc_ref, ln_ref, lb_ref, *rest, kinds):
    n_out = len(kinds)
    w_refs = rest[:n_out]
    b_refs = rest[n_out:2 * n_out]
    o_refs = rest[2 * n_out:3 * n_out]
    h_ref = rest[3 * n_out]
    t = pl.program_id(1)
    tm = x_ref.shape[0]
    rows = min(tm, IN_PROJ_ROWS)

    def finish(kind, y, o_ref, rs):
        if kind == "hgrn2_qig":
            q = y[:, :MIX_W]
            o_ref[rs, :MIX_W] = (q * _sigmoid_t(q)).astype(o_ref.dtype)
            o_ref[rs, MIX_W:] = y[:, MIX_W:].astype(o_ref.dtype)
        elif kind == "hgrn2_forget":
            o_ref[rs, :] = _log2_forget(y, lb_ref[...]).astype(o_ref.dtype)
        elif kind == "mlstm_gates":
            lane = lax.broadcasted_iota(jnp.int32, y.shape, 1) % GATE_LANES
            is_f = jnp.logical_and(lane >= N_HEADS, lane < 2 * N_HEADS)
            o_ref[rs, :] = (jnp.where(is_f, _log_sigmoid(y), y) * LOG2_E).astype(o_ref.dtype)
        else:
            o_ref[rs, :] = y.astype(o_ref.dtype)

    def row_group(r0):
        rs = slice(r0, r0 + rows)
        xin = jnp.where(t == 0, ctx_ref[rs, :], x_ref[rs, :])
        h = _norm_mod(xin, ln_ref[...], sh_ref[...], sc_ref[...]).astype(BF16)
        h_ref[rs, :] = h
        yield
        for kind, w_ref, b_ref, o_ref in zip(kinds, w_refs, b_refs, o_refs):
            finish(kind, _dot(h, w_ref[...]) + b_ref[...], o_ref, rs)
            yield

    gens = [row_group(r0) for r0 in range(0, tm, rows)]
    n_phase = 1 + n_out
    for step in range(n_phase + len(gens) - 1):
        for i, g in enumerate(gens):
            if 0 <= step - i < n_phase:
                next(g)


def _in_proj_call(l, ctx, x, mods4, ln1, lb3, ws, bs, out_dtypes, kinds):
    bsz, lc, d_model = ctx.shape
    seq = x.shape[1]
    tm = lc
    nt = 1 + seq // tm
    n_out = len(ws)

    def mod_spec(chunk):
        return pl.BlockSpec((None, None, 1, d_model),
                            lambda b, t: (l, jnp.where(t == 0, bsz, b), 0, chunk))

    in_specs = [
        pl.BlockSpec((None, tm, d_model), lambda b, t: (b, 0, 0)),
        pl.BlockSpec((None, tm, d_model), lambda b, t: (b, jnp.maximum(t - 1, 0), 0)),
        mod_spec(0), mod_spec(1),
        pl.BlockSpec((None, 1, d_model), lambda b, t: (l, 0, 0)),
        pl.BlockSpec((None, 1, lb3.shape[-1]), lambda b, t: (l, 0, 0)),
    ]
    in_specs += [pl.BlockSpec((None, d_model, w.shape[-1]), lambda b, t: (l, 0, 0)) for w in ws]
    in_specs += [pl.BlockSpec((None, 1, w.shape[-1]), lambda b, t: (l, 0, 0)) for w in ws]
    out_specs = [pl.BlockSpec((None, tm, w.shape[-1]), lambda b, t: (b, t, 0)) for w in ws]
    out_shape = [jax.ShapeDtypeStruct((bsz, lc + seq, w.shape[-1]), dt) for w, dt in zip(ws, out_dtypes)]
    out_specs.append(pl.BlockSpec((None, tm, d_model), lambda b, t: (b, t, 0)))
    out_shape.append(jax.ShapeDtypeStruct((bsz, lc + seq, d_model), BF16))
    return pl.pallas_call(
        functools.partial(_in_proj_body, kinds=kinds),
        grid=(bsz, nt),
        in_specs=in_specs,
        out_specs=out_specs,
        out_shape=out_shape,
        compiler_params=_cparams(("arbitrary", "arbitrary")),
        name="in_proj",
    )(ctx, x, mods4, mods4, ln1, lb3, *ws, *bs)


def _hgrn2_body(qv_ref, g_ref, lb_ref, mst_ref, msk_ref, o_ref, st_ref, *, C, n_lv):
    q16 = qv_ref[:, :MIX_W]
    v = qv_ref[:, MIX_W:]
    q = q16.astype(F32)
    g = g_ref[...]
    lb = lb_ref[...]
    kk = (1.0 - jnp.exp2(g)) + (jnp.maximum(lb, LB_TINY) - lb)
    g_hi, g_lo = _split_hi_lo(g)
    ex = _dot(mst_ref[...], jnp.concatenate([g_hi, g_lo], axis=0))
    tot = jnp.sum(g, axis=0, keepdims=True)
    k16 = kk.astype(BF16)

    for h0 in range(0, N_HEADS, HG_GROUP):
        heads = range(h0, h0 + HG_GROUP)
        sls = {h: slice(h * HG_DK, (h + 1) * HG_DK) for h in heads}
        p = {h: msk_ref[n_lv] * _dot_nt(q16[:, sls[h]], k16[:, sls[h]]) for h in heads}
        for lv in range(n_lv):
            for h in heads:
                w = jnp.exp2(ex[lv * C:(lv + 1) * C, sls[h]]).astype(BF16)
                p[h] = p[h] + msk_ref[lv] * _dot_nt(q16[:, sls[h]] * w, k16[:, sls[h]] * w)
            if lv % 2 == 1:
                yield
        for h in heads:
            sl = sls[h]
            st = st_ref[h]
            b_in = ex[n_lv * C:(n_lv + 1) * C, sl]
            qb = (q[:, sl] * jnp.exp2(b_in)).astype(BF16)
            vh = v[:, sl]
            o_ref[:, sl] = (_dot(p[h].astype(BF16), vh) + _dot_nt(qb, st.astype(BF16))).astype(o_ref.dtype)
            tot_h = tot[:, sl]
            kb = (kk[:, sl] * jnp.exp2(tot_h - b_in)).astype(BF16)
            st_ref[h] = st * jnp.exp2(tot_h) + _dot_tn(vh, kb)
            yield


def _mlstm_body(qkv_ref, g_ref, tri_ref, trit_ref, msk_ref, o_ref, c_ref, m_ref, *, C):
    qk_w = N_HEADS * ML_DQK
    gates = g_ref[...]
    gates_t = gates.T
    lf_hi, lf_lo = _split_hi_lo(gates)
    b_cols = _dot(tri_ref[...], jnp.concatenate([lf_hi, lf_lo], axis=0))
    lft_hi, lft_lo = _split_hi_lo(gates_t)
    b_rows = _dot(jnp.concatenate([lft_hi, lft_lo], axis=1), trit_ref[...])
    tot = jnp.sum(gates, axis=0, keepdims=True)
    allowed = msk_ref[...] > 0.0
    ones = jnp.ones((C, HEAD_DV), BF16)
    log2_scale = 0.5 * np.log2(ML_DQK)

    for h in range(N_HEADS):
        b_col = b_cols[:, N_HEADS + h:N_HEADS + h + 1]
        c_col = gates[:, h:h + 1] - b_col
        c_row = gates_t[h:h + 1, :] - b_rows[N_HEADS + h:N_HEADS + h + 1, :]
        m_prev = m_ref[h][0:1, 0:1]
        cm = jnp.where(allowed, c_row, NEG_BIG)
        m_t = jnp.maximum(jnp.max(cm, axis=-1, keepdims=True), m_prev)
        yield
        qh = qkv_ref[:, h * ML_DQK:(h + 1) * ML_DQK]
        kh = qkv_ref[:, qk_w + h * ML_DQK:qk_w + (h + 1) * ML_DQK]
        vh = qkv_ref[:, 2 * qk_w + h * HEAD_DV:2 * qk_w + (h + 1) * HEAD_DV]
        vaug = jnp.concatenate([vh, ones], axis=1)
        s = _dot_nt(qh, kh) * jnp.exp2(cm - (m_t + log2_scale))
        qs = (qh.astype(F32) * jnp.exp2(m_prev - m_t - log2_scale)).astype(BF16)
        yield
        c_aug = c_ref[h]
        r = _dot(jnp.concatenate([s.astype(BF16), qs], axis=1),
                 jnp.concatenate([vaug, c_aug.astype(BF16)], axis=0))
        num = r[:, :HEAD_DV]
        den = r[:, HEAD_DV:]
        hout = num / jnp.maximum(jnp.abs(den), jnp.exp2(-(b_col + m_t)))
        o_ref[:, h * HEAD_DV:(h + 1) * HEAD_DV] = hout.astype(o_ref.dtype)
        yield
        b_last = tot[:, N_HEADS + h:N_HEADS + h + 1]
        log_w = b_last + c_col
        m_new = jnp.maximum(b_last + m_prev, jnp.max(log_w, axis=0, keepdims=True))
        kw = (kh.astype(F32) * jnp.exp2(log_w - m_new)).astype(BF16)
        c_ref[h] = jnp.exp2(b_last + m_prev - m_new) * c_aug + _dot_tn(kw, vaug)
        m_ref[h] = jnp.broadcast_to(m_new, m_ref.shape[1:])
        yield


def _mixers_body(hqv_ref, hg_ref, lb_ref, mst_ref, hmsk_ref, mqkv_ref, mg_ref,
                 tri_ref, trit_ref, mmsk_ref, o_hg_ref, o_ml_ref, st_ref, c_ref, m_ref, *, C, n_lv):
    @pl.when(pl.program_id(2) == 0)
    def _():
        st_ref[...] = jnp.zeros_like(st_ref)
        c_ref[...] = jnp.zeros_like(c_ref)
        m_ref[...] = jnp.zeros_like(m_ref)

    def element(bb):
        gens = [_hgrn2_body(hqv_ref.at[bb], hg_ref.at[bb], lb_ref, mst_ref, hmsk_ref,
                            o_hg_ref.at[bb], st_ref.at[bb], C=C, n_lv=n_lv),
                _mlstm_body(mqkv_ref.at[bb], mg_ref.at[bb], tri_ref, trit_ref,
                            mmsk_ref, o_ml_ref.at[bb], c_ref.at[bb], m_ref.at[bb], C=C)]
        for _ in itertools.zip_longest(*gens):
            yield

    streams = [element(bb) for bb in range(hqv_ref.shape[0])]
    live = []
    while streams or live:
        if streams and (not live or live[-1][1] >= MIX_SKEW):
            live.append([streams.pop(0), 0])
        for item in list(live):
            try:
                next(item[0])
                item[1] += 1
            except StopIteration:
                live.remove(item)


def _mixers_call(l, hg_a, hg_f, lb4, ml_a, ml_g, lc):
    bsz, tt, _ = hg_a.shape
    C = CHUNK
    nch, nctx = tt // C, lc // C
    mst, hmsk, n_lv = _hgrn2_consts(C)
    mst = jnp.asarray(mst, BF16)
    hmsk = jnp.asarray(hmsk, F32)
    tri_cat, tri_t_cat, tri = _mlstm_consts(C)
    tri_cat = jnp.asarray(tri_cat, BF16)
    tri_t_cat = jnp.asarray(tri_t_cat, BF16)
    mmsk = jnp.asarray(tri, F32)
    cidx = functools.partial(_chunk_index, nctx=nctx, nch=nch)
    qk_w = N_HEADS * ML_DQK
    tok = lambda col: (lambda b, d, j: (b, cidx(d, j), col))
    tok_d = lambda b, d, j: (b, cidx(d, j), d)
    per_dir3 = lambda b, d, j: (d, 0, 0)
    nb = max(n for n in range(1, MIX_NB + 1) if bsz % n == 0)
    o_spec = pl.BlockSpec((nb, None, C, MIX_W), lambda b, d, j: (b, d, cidx(d, j), 0))
    o_shape = jax.ShapeDtypeStruct((bsz, 2, tt, MIX_W), BF16)
    return pl.pallas_call(
        functools.partial(_mixers_body, C=C, n_lv=n_lv),
        grid=(bsz // nb, 2, nch),
        in_specs=[
            pl.BlockSpec((nb, C, 2 * MIX_W), tok(0)),
            pl.BlockSpec((nb, C, MIX_W), tok_d),
            pl.BlockSpec((None, None, 1, MIX_W), lambda b, d, j: (l, d, 0, 0)),
            pl.BlockSpec((None,) + mst.shape[1:], per_dir3),
            pl.BlockSpec((None,) + hmsk.shape[1:], lambda b, d, j: (d, 0, 0, 0)),
            pl.BlockSpec((nb, C, 2 * qk_w + MIX_W), tok(0)),
            pl.BlockSpec((nb, C, GATE_LANES), tok_d),
            pl.BlockSpec((None, C, 2 * C), per_dir3),
            pl.BlockSpec((None, 2 * C, C), per_dir3),
            pl.BlockSpec((None, C, C), per_dir3),
        ],
        out_specs=[o_spec, o_spec],
        out_shape=[o_shape, o_shape],
        scratch_shapes=[pltpu.VMEM((nb, N_HEADS, HEAD_DV, HG_DK), F32),
                        pltpu.VMEM((nb, N_HEADS, ML_DQK, 2 * HEAD_DV), F32),
                        pltpu.VMEM((nb, N_HEADS, 8, 128), F32)],
        compiler_params=_cparams(("arbitrary", "arbitrary", "arbitrary")),
        name="mixers",
    )(hg_a, hg_f, lb4, mst, hmsk, ml_a, ml_g, tri_cat, tri_t_cat, mmsk)


def _lru_gates_body(x_ref, xp_ref, xn_ref, cw_ref, cb_ref, wg_ref, bg_ref, lam_ref,
                    a0_ref, u0_ref, a1_ref, u1_ref, ext_ref, *, tc, nctx_t, nt):
    t = pl.program_id(1)
    prev_ok = jnp.logical_and(t != 0, t != nctx_t)
    next_ok = jnp.logical_and(t != nctx_t - 1, t != nt - 1)
    x = x_ref[...]
    ext_ref[0:8, :] = jnp.where(prev_ok, xp_ref[...], 0.0)
    ext_ref[8:8 + tc, :] = x
    ext_ref[8 + tc:16 + tc, :] = jnp.where(next_ok, xn_ref[...], 0.0)
    cw = cw_ref[...]
    xc = (cw[0:1] * ext_ref[6:6 + tc, :] + cw[1:2] * ext_ref[7:7 + tc, :] + cw[2:3] * x
          + cw[3:4] * ext_ref[9:9 + tc, :]) + cb_ref[...]
    gates = _dot(xc.astype(BF16), wg_ref[...]) + bg_ref[...]
    outs = ((a0_ref, u0_ref), (a1_ref, u1_ref))
    for d in range(2):
        r_pre = gates[:, (2 * d) * MIX_W:(2 * d + 1) * MIX_W]
        i_pre = gates[:, (2 * d + 1) * MIX_W:(2 * d + 2) * MIX_W]
        z = -lam_ref[d:d + 1, :]
        softplus = jnp.maximum(z, 0.0) + jnp.log(1.0 + jnp.exp(-jnp.abs(z)))
        log_a = (-LRU_C * softplus) * _sigmoid_t(r_pre)
        a_ref, u_ref = outs[d]
        a = jnp.exp(log_a)
        a_ref[...] = a
        one_m_a2 = -jnp.tanh(log_a) * (a * a + 1.0)
        u_ref[...] = jnp.sqrt(jnp.maximum(one_m_a2, 0.0)) * (_sigmoid_t(i_pre) * xc)


def _lru_gates_call(l, lru_x, conv_w, conv_b, wg, bg, lam, lc):
    bsz, tt, w = lru_x.shape
    tc = lc
    nt, nctx_t = tt // tc, lc // tc
    h8 = tc // 8
    out = jax.ShapeDtypeStruct((bsz, tt, w), F32)
    o_spec = pl.BlockSpec((None, tc, w), lambda b, t: (b, t, 0))
    return pl.pallas_call(
        functools.partial(_lru_gates_body, tc=tc, nctx_t=nctx_t, nt=nt),
        grid=(bsz, nt),
        in_specs=[
            pl.BlockSpec((None, tc, w), lambda b, t: (b, t, 0)),
            pl.BlockSpec((None, 8, w), lambda b, t: (b, jnp.maximum(t * h8 - 1, 0), 0)),
            pl.BlockSpec((None, 8, w), lambda b, t: (b, jnp.minimum((t + 1) * h8, tt // 8 - 1), 0)),
            pl.BlockSpec((None, 4, w), lambda b, t: (l, 0, 0)),
            pl.BlockSpec((None, 1, w), lambda b, t: (l, 0, 0)),
            pl.BlockSpec((None, w, 4 * w), lambda b, t: (l, 0, 0)),
            pl.BlockSpec((None, 1, 4 * w), lambda b, t: (l, 0, 0)),
            pl.BlockSpec((None, 2, w), lambda b, t: (l, 0, 0)),
        ],
        out_specs=[o_spec] * 4,
        out_shape=[out] * 4,
        scratch_shapes=[pltpu.VMEM((tc + 16, w), F32)],
        compiler_params=_cparams(("arbitrary", "arbitrary")),
        name="rglru_gates",
    )(lru_x, lru_x, lru_x, conv_w, conv_b, wg, bg, lam)


def _lru_scan_body(a0_ref, u0_ref, a1_ref, u1_ref, hf_ref, hb_ref, sf_ref, sb_ref, tf_ref, tb_ref, *, tc):
    @pl.when(pl.program_id(0) == 0)
    def _():
        sf_ref[...] = jnp.zeros_like(sf_ref)
        sb_ref[...] = jnp.zeros_like(sb_ref)

    def step(i, carry):
        hf, hb = carry
        hf = a0_ref[:, i, :] * hf + u0_ref[:, i, :]
        tf_ref[:, i, :] = hf
        ib = tc - 1 - i
        hb = a1_ref[:, ib, :] * hb + u1_ref[:, ib, :]
        tb_ref[:, ib, :] = hb
        return hf, hb

    hf, hb = lax.fori_loop(0, tc, step, (sf_ref[...], sb_ref[...]), unroll=8)
    sf_ref[...] = hf
    sb_ref[...] = hb
    hf_ref[...] = tf_ref[...].astype(hf_ref.dtype)
    hb_ref[...] = tb_ref[...].astype(hb_ref.dtype)


def _lru_scan_call(a0, u0, a1, u1, lc):
    bsz, tt, w = a0.shape
    tc = CHUNK
    nch, nctx = tt // tc, lc // tc
    fw = pl.BlockSpec((bsz, tc, w), lambda j: (0, j, 0))
    bw = pl.BlockSpec((bsz, tc, w), lambda j: (0, _chunk_index(1, j, nctx, nch), 0))
    out = jax.ShapeDtypeStruct((bsz, tt, w), BF16)
    return pl.pallas_call(
        functools.partial(_lru_scan_body, tc=tc),
        grid=(nch,),
        in_specs=[fw, fw, bw, bw],
        out_specs=[fw, bw],
        out_shape=[out, out],
        scratch_shapes=[pltpu.VMEM((bsz, w), F32), pltpu.VMEM((bsz, w), F32),
                        pltpu.VMEM((bsz, tc, w), F32), pltpu.VMEM((bsz, tc, w), F32)],
        compiler_params=_cparams(("arbitrary",)),
        name="rglru_scan",
    )(a0, u0, a1, u1)


def _head_rms(o, w):
    parts = []
    for h in range(N_HEADS):
        oh = o[:, h * HEAD_DV:(h + 1) * HEAD_DV]
        parts.append(oh * lax.rsqrt(jnp.mean(oh * oh, axis=-1, keepdims=True) + EPS))
    return jnp.concatenate(parts, axis=1) * w


def _gelu_tanh(x):
    return 0.5 * x * (1.0 + jnp.tanh(0.7978845608028654 * (x + 0.044715 * (x * x * x))))


def _merge_rows(hg0_ref, hg1_ref, ml0_ref, ml1_ref, lf_ref, lb_ref, hgg_ref, mlo_ref, ly_ref, h_ref,
                res_ref, gate_ref, wmg_ref, bmg_ref, hgn_ref, mln_ref, wb_ref, wo_ref, o_ref,
                r0, rows, d_model):
    rs = slice(r0, r0 + rows)
    hgg = hgg_ref[rs, :].astype(F32)
    o_hg = hg0_ref[rs, :].astype(F32) + hg1_ref[rs, :].astype(F32)
    o_ml = ml0_ref[rs, :].astype(F32) + ml1_ref[rs, :].astype(F32)
    a_out = _head_rms(o_hg, hgn_ref[...]) * (hgg * _sigmoid_t(hgg))
    b_out = _head_rms(o_ml, mln_ref[...]) * _sigmoid_t(mlo_ref[rs, :].astype(F32))
    h_lru = lf_ref[rs, :].astype(F32) + lb_ref[rs, :].astype(F32)
    c_out = h_lru * _gelu_tanh(ly_ref[rs, :].astype(F32))
    h = h_ref[rs, :]
    yield
    merged = None
    for n, br in enumerate((a_out, b_out, c_out)):
        cs = slice(n * d_model, (n + 1) * d_model)
        gate = _sigmoid_t((_dot(h, wmg_ref[:, cs]) + bmg_ref[:, cs]).astype(BF16))
        term = gate * _dot(br.astype(BF16), wb_ref[n]).astype(BF16)
        merged = term if merged is None else merged + term
        yield
    o_ref[rs, :] = res_ref[rs, :] + gate_ref[...] * _dot(merged, wo_ref[...])
    yield


def _merge_body(*refs, rows, d_model):
    tm = refs[-1].shape[0]
    gens = [_merge_rows(*refs, r0, rows, d_model) for r0 in range(0, tm, rows)]
    n_phase = 5
    for step in range(n_phase + len(gens) - 1):
        for i, g in enumerate(gens):
            if 0 <= step - i < n_phase:
                next(g)


def _merge_call(l, is_ctx, tm, o_hg, o_ml, h_f, h_b, hg_a, ml_a, lru_y, h1, res, mods4,
                w_mg, b_mg, hg_norm, ml_norm, w_branch, w_out):
    bsz, n, d_model = res.shape
    t0 = 0 if is_ctx else 1
    rows = min(tm, MERGE_ROWS)

    def tok(col):
        return lambda b, t: (b, t + t0, col)

    mix = lambda d: pl.BlockSpec((None, None, tm, MIX_W), lambda b, t: (b, d, t + t0, 0))
    in_specs = [
        mix(0), mix(1), mix(0), mix(1),
        pl.BlockSpec((None, tm, MIX_W), tok(0)),
        pl.BlockSpec((None, tm, MIX_W), tok(0)),
        pl.BlockSpec((None, tm, MIX_W), tok(2)),
        pl.BlockSpec((None, tm, MIX_W), tok(2)),
        pl.BlockSpec((None, tm, MIX_W), tok(0)),
        pl.BlockSpec((None, tm, d_model), tok(0)),
        pl.BlockSpec((None, tm, d_model), lambda b, t: (b, t, 0)),
        pl.BlockSpec((None, None, 1, d_model), lambda b, t: (l, bsz if is_ctx else b, 0, 2)),
        _resident((None, d_model, 3 * d_model), lambda b, t: (l, 0, 0)),
        pl.BlockSpec((None, 1, 3 * d_model), lambda b, t: (l, 0, 0)),
        pl.BlockSpec((None, 1, MIX_W), lambda b, t: (l, 0, 0)),
        pl.BlockSpec((None, 1, MIX_W), lambda b, t: (l, 0, 0)),
        _resident((None, 3, MIX_W, d_model), lambda b, t: (l, 0, 0, 0)),
        _resident((None, d_model, d_model), lambda b, t: (l, 0, 0)),
    ]
    return pl.pallas_call(
        functools.partial(_merge_body, rows=rows, d_model=d_model),
        grid=(bsz, n // tm),
        in_specs=in_specs,
        out_specs=pl.BlockSpec((None, tm, d_model), lambda b, t: (b, t, 0)),
        out_shape=jax.ShapeDtypeStruct(res.shape, F32),
        compiler_params=_cparams(("arbitrary", "arbitrary")),
        name="merge",
    )(o_hg, o_hg, o_ml, o_ml, h_f, h_b, hg_a, ml_a, lru_y, h1, res, mods4,
      w_mg, b_mg, hg_norm, ml_norm, w_branch, w_out)


def _ffn_rows(x_ref, sh_ref, sc_ref, gt_ref, ln_ref, wi_ref, wo_ref, fn_ref, o_ref, r0, rows, *,
              n_inner, final_norm):
    x = x_ref[r0:r0 + rows, :]
    h = _norm_mod(x, ln_ref[...], sh_ref[...], sc_ref[...]).astype(BF16)
    yield
    gu = _dot(h, wi_ref[...])
    yield
    hidden = gu.shape[-1] // 2
    gate = gu[:, :hidden]
    act = (gate * _sigmoid_t(gate) * gu[:, hidden:]).astype(BF16)
    yield
    f = _dot(act, wo_ref[...])
    yield
    y = x + gt_ref[...] * f
    if final_norm:
        y = y * lax.rsqrt(jnp.mean(y * y, axis=-1, keepdims=True) + EPS) * fn_ref[...]
    if n_inner is None:
        o_ref[r0:r0 + rows, :] = y
    else:
        for a in range(rows // n_inner):
            o_ref[:, r0 // n_inner + a, :] = y[a * n_inner:(a + 1) * n_inner, :]
    yield


def _ffn_body(x_ref, sh_ref, sc_ref, gt_ref, ln_ref, wi_ref, wo_ref, fn_ref, o_ref, *, rows, n_inner,
              final_norm):
    gens = [_ffn_rows(x_ref, sh_ref, sc_ref, gt_ref, ln_ref, wi_ref, wo_ref, fn_ref, o_ref, r0, rows,
                      n_inner=n_inner, final_norm=final_norm)
            for r0 in range(0, x_ref.shape[0], rows)]
    skew = 2
    n_phase = 5
    for step in range(n_phase + skew * (len(gens) - 1)):
        for i, g in enumerate(gens):
            if 0 <= step - skew * i < n_phase:
                next(g)


def _ffn_call(l, x, mods4, ctx_row, ln2, wi, wo, fin, *, tm, n_inner, final_norm):
    bsz, n, d_model = x.shape
    hidden = wo.shape[1]
    nt = n // tm
    rows = min(tm, FFN_ROWS)

    def mod_spec(chunk):
        return pl.BlockSpec((None, None, 1, d_model),
                            lambda b, t: (l, bsz if ctx_row else b, 0, chunk))

    if n_inner is None:
        o_spec = pl.BlockSpec((None, tm, d_model), lambda b, t: (b, t, 0))
        o_shape = jax.ShapeDtypeStruct(x.shape, F32)
    else:
        assert rows % n_inner == 0
        n_a = tm // n_inner
        o_spec = pl.BlockSpec((None, n_inner, n_a, d_model), lambda b, t: (b, 0, t, 0))
        o_shape = jax.ShapeDtypeStruct((bsz, n_inner, n // n_inner, d_model), F32)
    out = pl.pallas_call(
        functools.partial(_ffn_body, rows=rows, n_inner=n_inner, final_norm=final_norm),
        grid=(bsz, nt),
        in_specs=[
            pl.BlockSpec((None, tm, d_model), lambda b, t: (b, t, 0)),
            mod_spec(3), mod_spec(4), mod_spec(5),
            pl.BlockSpec((None, 1, d_model), lambda b, t: (l, 0, 0)),
            _resident((None, d_model, 2 * hidden), lambda b, t: (l, 0, 0)),
            _resident((None, hidden, d_model), lambda b, t: (l, 0, 0)),
            pl.BlockSpec((1, d_model), lambda b, t: (0, 0)),
        ],
        out_specs=o_spec,
        out_shape=o_shape,
        compiler_params=_cparams(("arbitrary", "arbitrary")),
        name="ffn",
    )(x, mods4, mods4, mods4, ln2, wi, wo, fin)
    return out.reshape(bsz, n, d_model)


def _prep_in_proj(w_in, b_in):
    depth, d_model, _ = w_in.shape
    hq = N_HEADS * HG_DK
    sizes = (hq, MIX_W, MIX_W, 2 * hq, N_HEADS * ML_DQK, N_HEADS * ML_DQK, MIX_W, MIX_W,
             2 * N_HEADS, 2 * N_HEADS, MIX_W, MIX_W, 3 * d_model)
    offs = np.concatenate([[0], np.cumsum(sizes)])
    assert offs[-1] == w_in.shape[-1]

    def cols(a, lo, hi):
        return a[..., int(offs[lo]):int(offs[hi])]

    def gate_block(a):
        ig = cols(a, 8, 9).reshape(a.shape[:-1] + (2, N_HEADS))
        fg = cols(a, 9, 10).reshape(a.shape[:-1] + (2, N_HEADS))
        pad = jnp.zeros(a.shape[:-1] + (2, GATE_LANES - 2 * N_HEADS), a.dtype)
        return jnp.concatenate([ig, fg, pad], axis=-1).reshape(a.shape[:-1] + (2 * GATE_LANES,))

    groups = [
        (lambda a: cols(a, 0, 3), BF16),
        (lambda a: cols(a, 3, 4), F32),
        (lambda a: cols(a, 4, 8), BF16),
        (gate_block, F32),
        (lambda a: cols(a, 10, 11), F32),
        (lambda a: cols(a, 11, 12), BF16),
    ]
    merge_gates = lambda a: cols(a, 12, 13)
    kinds = ("hgrn2_qig", "hgrn2_forget", None, "mlstm_gates", None, None)
    b3 = b_in.reshape(depth, 1, -1)
    ws = [f(w_in).astype(BF16) for f, _ in groups]
    bs = [f(b3) for f, _ in groups]
    w_mg = merge_gates(w_in).astype(BF16)
    b_mg = merge_gates(b3)
    return ws, bs, [dt for _, dt in groups], kinds, w_mg, b_mg


def _prep_lru_gates(lru_gate_w, lru_gate_b):
    depth = lru_gate_w.shape[0]
    eye = jnp.eye(LRU_BLOCKS, dtype=lru_gate_w.dtype)
    dense = jnp.einsum("lzgnde,nm->lndzgme", lru_gate_w, eye)
    dense = dense.reshape(depth, MIX_W, 4 * MIX_W).astype(BF16)
    return dense, lru_gate_b.reshape(depth, 1, 4 * MIX_W)


def kernel(x, c, ctx, c_ctx, w_ada, b_ada, ln1, w_in, b_in, hg_lb_raw, hg_norm, ml_norm, conv_w, conv_b,
           lru_gate_w, lru_gate_b, lru_lambda, w_branch, w_out, ln2, w_ffn_in, w_ffn_out, final_norm):
    bsz, seq, d_model = x.shape
    lc = ctx.shape[1]
    depth = w_ada.shape[0]
    rows = seq // GRID_W
    assert lc % CHUNK == 0 and seq % lc == 0 and seq % FFN_TM == 0 and bsz < 16

    c_all = jnp.zeros((16, d_model), F32).at[:bsz].set(c).at[bsz].set(c_ctx)
    mods4 = _ada_call(c_all, w_ada, b_ada).reshape(depth, 16, 1, 6 * d_model)
    lb_all = _lb_call(hg_lb_raw)
    lb3 = lb_all.reshape(depth, 1, 2 * MIX_W)
    lb4 = lb_all.reshape(depth, 2, 1, MIX_W)

    ws, bs, out_dtypes, kinds, w_mg, b_mg = _prep_in_proj(w_in, b_in)
    wgate, bgate = _prep_lru_gates(lru_gate_w, lru_gate_b)
    w_ffn_in_b = w_ffn_in.astype(BF16)
    w_ffn_out_b = w_ffn_out.astype(BF16)
    w_branch_b = w_branch.astype(BF16)
    w_out_b = w_out.astype(BF16)
    ln1_3 = ln1.reshape(depth, 1, d_model)
    ln2_3 = ln2.reshape(depth, 1, d_model)
    hgn = hg_norm.reshape(depth, 1, MIX_W)
    mln = ml_norm.reshape(depth, 1, MIX_W)
    conv_b3 = conv_b.reshape(depth, 1, MIX_W)
    fin = final_norm.reshape(1, d_model)

    for l in range(depth):
        last = l == depth - 1
        n_inner = GRID_W if l % 2 == 0 else rows
        hg_a, hg_f, ml_a, ml_g, lru_x, lru_y, h1 = _in_proj_call(l, ctx, x, mods4, ln1_3, lb3, ws, bs,
                                                                 out_dtypes, kinds)
        o_hg, o_ml = _mixers_call(l, hg_a, hg_f, lb4, ml_a, ml_g, lc)
        a0, u0, a1, u1 = _lru_gates_call(l, lru_x, conv_w, conv_b3, wgate, bgate, lru_lambda, lc)
        h_f, h_b = _lru_scan_call(a0, u0, a1, u1, lc)
        branches = (o_hg, o_ml, h_f, h_b, hg_a, ml_a, lru_y, h1)
        merge_w = (w_mg, b_mg, hgn, mln, w_branch_b, w_out_b)
        x_m = _merge_call(l, False, lc, *branches, x, mods4, *merge_w)
        x = _ffn_call(l, x_m, mods4, False, ln2_3, w_ffn_in_b, w_ffn_out_b, fin,
                      tm=FFN_TM, n_inner=n_inner, final_norm=last)
        if not last:
            ctx_m = _merge_call(l, True, lc, *branches, ctx, mods4, *merge_w)
            ctx = _ffn_call(l, ctx_m, mods4, True, ln2_3, w_ffn_in_b, w_ffn_out_b, fin,
                            tm=lc, n_inner=None, final_norm=False)
    return x
```

```python
import functools
import itertools

import numpy as np
import jax
import jax.numpy as jnp
from jax import lax
from jax.experimental import pallas as pl
from jax.experimental.pallas import tpu as pltpu

F32 = jnp.float32
BF16 = jnp.bfloat16

GRID_W = 64
MIX_W = 512
N_HEADS = 4
HEAD_DV = 128
HG_DK = 128
ML_DQK = 64
LRU_BLOCKS = 8
LRU_BD = 64
LRU_C = 8.0
EPS = 1e-6
NEG_BIG = -1e30
LB_TINY = 1e-30
LOG2_E = 1.4426950408889634
GATE_LANES = 128
CHUNK = 128
HG_GROUP = 1
MIX_NB = 4
MIX_SKEW = 14
FFN_TM = 512
FFN_ROWS = 256
MERGE_ROWS = 128
IN_PROJ_ROWS = 128
VMEM_LIMIT = 56 * 1024 * 1024


def _cparams(sem):
    return pltpu.CompilerParams(dimension_semantics=sem, vmem_limit_bytes=VMEM_LIMIT)


def _resident(shape, index_map):
    return pl.BlockSpec(shape, index_map, pipeline_mode=pl.Buffered(1))


def _sigmoid(x):
    return 1.0 / (1.0 + jnp.exp(-x))


def _sigmoid_t(x):
    return 0.5 * jnp.tanh(0.5 * x) + 0.5


def _log_sigmoid(x):
    return jnp.minimum(x, 0.0) - jnp.log(1.0 + jnp.exp(-jnp.abs(x)))


def _dot(a, b):
    return jnp.dot(a, b, preferred_element_type=F32)


def _dot_nt(a, b):
    return lax.dot_general(a, b, (((1,), (1,)), ((), ())), preferred_element_type=F32)


def _dot_tn(a, b):
    return lax.dot_general(a, b, (((0,), (0,)), ((), ())), preferred_element_type=F32)


def _split_hi_lo(x):
    hi = x.astype(BF16)
    lo = (x - hi.astype(F32)).astype(BF16)
    return hi, lo


def _norm_mod(x, ln, shift, scale):
    y = x * lax.rsqrt(jnp.mean(x * x, axis=-1, keepdims=True) + EPS) * ln
    return y * (1.0 + scale) + shift


def _mirror(m):
    return m[..., ::-1, ::-1].copy()


@functools.lru_cache(maxsize=None)
def _hgrn2_consts(C):
    n_lv = int(np.log2(C))
    mats, masks = [], []
    r = np.arange(C)
    for lv in range(n_lv):
        s = C >> (lv + 1)
        base = (r // (2 * s)) * (2 * s)
        mid = base + s - 1
        odd = r >= base + s
        m = np.zeros((C, C), np.float32)
        for t in range(C):
            if odd[t]:
                m[t, mid[t] + 1:t + 1] = 1.0
            else:
                m[t, t + 1:mid[t] + 1] = 1.0
        mats.append(m)
        same = base[:, None] == base[None, :]
        masks.append((same & odd[:, None] & (~odd)[None, :]).astype(np.float32))
    mats.append(np.tril(np.ones((C, C), np.float32)))
    masks.append(np.eye(C, dtype=np.float32))
    mats = np.stack(mats)
    masks = np.stack(masks)
    mats = np.stack([mats, _mirror(mats)])
    masks = np.stack([masks, _mirror(masks)])
    mst = mats.reshape(2, (n_lv + 1) * C, C)
    mst = np.concatenate([mst, mst], axis=-1)
    return mst, masks, n_lv


@functools.lru_cache(maxsize=None)
def _mlstm_consts(C):
    tri = np.tril(np.ones((C, C), np.float32))
    tri = np.stack([tri, _mirror(tri)])
    tri_cat = np.concatenate([tri, tri], axis=-1)
    tri_t = np.transpose(tri, (0, 2, 1))
    tri_t_cat = np.concatenate([tri_t, tri_t], axis=1)
    return tri_cat, tri_t_cat, tri


def _chunk_index(d, j, nctx, nch):
    bw = jnp.where(j < nctx, nctx - 1 - j, nch - 1 + nctx - j)
    return jnp.where(d == 0, j, bw)


def _ada_body(c_ref, w_ref, b_ref, o_ref):
    cc = c_ref[...]
    s = cc * _sigmoid(cc)
    o_ref[...] = jnp.dot(s, w_ref[...], preferred_element_type=F32,
                         precision=lax.Precision.HIGHEST) + b_ref[...]


def _ada_call(c_all, w_ada, b_ada):
    depth, d_model, n6 = w_ada.shape
    rows = c_all.shape[0]
    tn = 1536
    return pl.pallas_call(
        _ada_body,
        grid=(depth, n6 // tn),
        in_specs=[
            pl.BlockSpec((rows, d_model), lambda l, n: (0, 0)),
            pl.BlockSpec((None, d_model, tn), lambda l, n: (l, 0, n)),
            pl.BlockSpec((None, 1, tn), lambda l, n: (l, 0, n)),
        ],
        out_specs=pl.BlockSpec((None, rows, tn), lambda l, n: (l, 0, n)),
        out_shape=jax.ShapeDtypeStruct((depth, rows, n6), F32),
        compiler_params=_cparams(("arbitrary", "arbitrary")),
        name="ada_mod",
    )(c_all, w_ada, b_ada.reshape(depth, 1, n6))


def _lb_body(raw_ref, o_ref):
    raw = raw_ref[...]
    depth = raw.shape[0]
    e = jnp.exp(raw - jnp.max(raw, axis=0, keepdims=True))
    p = e / jnp.sum(e, axis=0, keepdims=True)
    acc = jnp.zeros_like(p[0:1])
    for l in range(depth):
        acc = acc + p[l:l + 1]
        o_ref[l:l + 1, :] = acc - p[0:1]


def _lb_call(hg_lb_raw):
    depth = hg_lb_raw.shape[0]
    raw = hg_lb_raw.reshape(depth, -1)
    return pl.pallas_call(
        _lb_body,
        out_shape=jax.ShapeDtypeStruct(raw.shape, F32),
        name="hgrn2_lower_bounds",
    )(raw)


def _log2_forget(f_pre, lb):
    e = jnp.exp(-jnp.abs(f_pre))
    inv = 1.0 / (1.0 + e)
    sig = jnp.where(f_pre >= 0.0, inv, e * inv)
    return jnp.log(jnp.maximum(lb, LB_TINY) + (1.0 - lb) * sig) * LOG2_E


def _lru_decay_input(r_pre, i_pre, lam, xc):
    z = -lam
    softplus = jnp.maximum(z, 0.0) + jnp.log(1.0 + jnp.exp(-jnp.abs(z)))
    log_a = (-LRU_C * softplus) * _sigmoid_t(r_pre)
    a = jnp.exp(log_a)
    one_m_a2 = -jnp.tanh(log_a) * (a * a + 1.0)
    return a, jnp.sqrt(jnp.maximum(one_m_a2, 0.0)) * (_sigmoid_t(i_pre) * xc)


def _in_proj_body(ctx_ref, x_ref, xp_ref, xn_ref, sh_ref, sc_ref, ln_ref, lb_ref,
                  cw_ref, cb_ref, wg_ref, bg_ref, lam_ref, *rest, kinds):
    n_grp = len(kinds)
    w_refs = rest[:n_grp]
    b_refs = rest[n_grp:2 * n_grp]
    outs = list(rest[2 * n_grp:])
    ext_ref = outs.pop()
    lru_refs = [outs.pop() for _ in range(4)][::-1]
    h_ref = outs.pop()
    o_refs = iter(outs)
    o_refs = [None if kind == "lru_x" else next(o_refs) for kind in kinds]
    t = pl.program_id(1)
    tm = x_ref.shape[0]
    rows = min(tm, IN_PROJ_ROWS)
    lru_i = kinds.index("lru_x")

    def norm(x):
        return _norm_mod(x, ln_ref[...], sh_ref[...], sc_ref[...]).astype(BF16)

    def finish(kind, y, o_ref, r0):
        rs = slice(r0, r0 + rows)
        if kind == "hgrn2_qig":
            q = y[:, :MIX_W]
            o_ref[rs, :MIX_W] = (q * _sigmoid_t(q)).astype(o_ref.dtype)
            o_ref[rs, MIX_W:] = y[:, MIX_W:].astype(o_ref.dtype)
        elif kind == "hgrn2_forget":
            o_ref[rs, :] = _log2_forget(y, lb_ref[...]).astype(o_ref.dtype)
        elif kind == "mlstm_gates":
            lane = lax.broadcasted_iota(jnp.int32, y.shape, 1) % GATE_LANES
            is_f = jnp.logical_and(lane >= N_HEADS, lane < 2 * N_HEADS)
            o_ref[rs, :] = (jnp.where(is_f, _log_sigmoid(y), y) * LOG2_E).astype(o_ref.dtype)
        elif kind == "lru_x":
            ext_ref[8 + r0:8 + r0 + rows, :] = y
        else:
            o_ref[rs, :] = y.astype(o_ref.dtype)

    def row_group(r0):
        rs = slice(r0, r0 + rows)
        h = norm(jnp.where(t == 0, ctx_ref[rs, :], x_ref[rs, :]))
        h_ref[rs, :] = h
        yield
        rest_order = sorted((i for i in range(n_grp) if i != lru_i), key=lambda i: kinds[i] is None)
        for i in [lru_i] + rest_order:
            finish(kinds[i], _dot(h, w_refs[i][...]) + b_refs[i][...], o_refs[i], r0)
            yield

    def lru_gates(wait):
        prev_ok = t >= 2
        next_ok = jnp.logical_and(t >= 1, t < pl.num_programs(1) - 1)
        w_lx, b_lx = w_refs[lru_i][...], b_refs[lru_i][...]
        ext_ref[0:8, :] = jnp.where(prev_ok, _dot(norm(xp_ref[...]), w_lx) + b_lx, 0.0)
        ext_ref[8 + tm:16 + tm, :] = jnp.where(next_ok, _dot(norm(xn_ref[...]), w_lx) + b_lx, 0.0)
        for _ in range(wait):
            yield
        cw = cw_ref[...]
        xc = (cw[0:1] * ext_ref[6:6 + tm, :] + cw[1:2] * ext_ref[7:7 + tm, :] + cw[2:3] * ext_ref[8:8 + tm, :]
              + cw[3:4] * ext_ref[9:9 + tm, :]) + cb_ref[...]
        yield
        gates = _dot(xc.astype(BF16), wg_ref[...]) + bg_ref[...]
        yield
        for d in range(2):
            for r0 in range(0, tm, rows):
                rs = slice(r0, r0 + rows)
                a, u = _lru_decay_input(gates[rs, (2 * d) * MIX_W:(2 * d + 1) * MIX_W],
                                        gates[rs, (2 * d + 1) * MIX_W:(2 * d + 2) * MIX_W],
                                        lam_ref[d:d + 1, :], xc[rs, :])
                lru_refs[2 * d][rs, :] = a
                lru_refs[2 * d + 1][rs, :] = u
                yield

    groups = [row_group(r0) for r0 in range(0, tm, rows)]
    gens = groups + [lru_gates(wait=len(groups) + 1)]
    live = list(gens)
    step = 0
    while live:
        for i, g in enumerate(gens):
            if g in live and (step >= i or g is gens[-1]):
                try:
                    next(g)
                except StopIteration:
                    live.remove(g)
        step += 1


def _in_proj_call(l, ctx, x, mods4, ln1, lb3, ws, bs, out_dtypes, kinds, conv_w, conv_b, wgate, bgate, lam):
    bsz, lc, d_model = ctx.shape
    seq = x.shape[1]
    tm = lc
    nt = 1 + seq // tm
    h8 = tm // 8

    def mod_spec(chunk):
        return pl.BlockSpec((None, None, 1, d_model),
                            lambda b, t: (l, jnp.where(t == 0, bsz, b), 0, chunk))

    def per_layer(shape):
        return pl.BlockSpec((None,) + shape, lambda b, t: (l,) + (0,) * len(shape))

    in_specs = [
        pl.BlockSpec((None, tm, d_model), lambda b, t: (b, 0, 0)),
        pl.BlockSpec((None, tm, d_model), lambda b, t: (b, jnp.maximum(t - 1, 0), 0)),
        pl.BlockSpec((None, 8, d_model), lambda b, t: (b, jnp.maximum((t - 1) * h8 - 1, 0), 0)),
        pl.BlockSpec((None, 8, d_model), lambda b, t: (b, jnp.minimum(jnp.maximum(t, 1) * h8, seq // 8 - 1), 0)),
        mod_spec(0), mod_spec(1),
        per_layer((1, d_model)),
        per_layer((1, lb3.shape[-1])),
        per_layer((4, MIX_W)), per_layer((1, MIX_W)),
        per_layer((MIX_W, 4 * MIX_W)), per_layer((1, 4 * MIX_W)), per_layer((2, MIX_W)),
    ]
    in_specs += [per_layer((d_model, w.shape[-1])) for w in ws]
    in_specs += [per_layer((1, w.shape[-1])) for w in ws]
    tok = lambda width: pl.BlockSpec((None, tm, width), lambda b, t: (b, t, 0))
    out_specs, out_shape = [], []
    for w, dt, kind in zip(ws, out_dtypes, kinds):
        if kind != "lru_x":
            out_specs.append(tok(w.shape[-1]))
            out_shape.append(jax.ShapeDtypeStruct((bsz, lc + seq, w.shape[-1]), dt))
    out_specs.append(tok(d_model))
    out_shape.append(jax.ShapeDtypeStruct((bsz, lc + seq, d_model), BF16))
    out_specs += [tok(MIX_W)] * 4
    out_shape += [jax.ShapeDtypeStruct((bsz, lc + seq, MIX_W), F32)] * 4
    return pl.pallas_call(
        functools.partial(_in_proj_body, kinds=kinds),
        grid=(bsz, nt),
        in_specs=in_specs,
        out_specs=out_specs,
        out_shape=out_shape,
        scratch_shapes=[pltpu.VMEM((tm + 16, MIX_W), F32)],
        compiler_params=_cparams(("arbitrary", "arbitrary")),
        name="in_proj",
    )(ctx, x, x, x, mods4, mods4, ln1, lb3, conv_w, conv_b, wgate, bgate, lam, *ws, *bs)


def _hgrn2_body(qv_ref, g_ref, lb_ref, mst_ref, msk_ref, o_ref, st_ref, *, C, n_lv):
    q16 = qv_ref[:, :MIX_W]
    v = qv_ref[:, MIX_W:]
    q = q16.astype(F32)
    g = g_ref[...]
    lb = lb_ref[...]
    kk = (1.0 - jnp.exp2(g)) + (jnp.maximum(lb, LB_TINY) - lb)
    g_hi, g_lo = _split_hi_lo(g)
    ex = _dot(mst_ref[...], jnp.concatenate([g_hi, g_lo], axis=0))
    tot = jnp.sum(g, axis=0, keepdims=True)
    k16 = kk.astype(BF16)

    for h0 in range(0, N_HEADS, HG_GROUP):
        heads = range(h0, h0 + HG_GROUP)
        sls = {h: slice(h * HG_DK, (h + 1) * HG_DK) for h in heads}
        p = {h: msk_ref[n_lv] * _dot_nt(q16[:, sls[h]], k16[:, sls[h]]) for h in heads}
        for lv in range(n_lv):
            for h in heads:
                w = jnp.exp2(ex[lv * C:(lv + 1) * C, sls[h]]).astype(BF16)
                p[h] = p[h] + msk_ref[lv] * _dot_nt(q16[:, sls[h]] * w, k16[:, sls[h]] * w)
            if lv % 2 == 1:
                yield
        for h in heads:
            sl = sls[h]
            st = st_ref[h]
            b_in = ex[n_lv * C:(n_lv + 1) * C, sl]
            qb = (q[:, sl] * jnp.exp2(b_in)).astype(BF16)
            vh = v[:, sl]
            o_ref[:, sl] = (_dot(p[h].astype(BF16), vh) + _dot_nt(qb, st.astype(BF16))).astype(o_ref.dtype)
            tot_h = tot[:, sl]
            kb = (kk[:, sl] * jnp.exp2(tot_h - b_in)).astype(BF16)
            st_ref[h] = st * jnp.exp2(tot_h) + _dot_tn(vh, kb)
            yield


def _mlstm_body(qkv_ref, g_ref, tri_ref, trit_ref, msk_ref, o_ref, c_ref, m_ref, *, C):
    qk_w = N_HEADS * ML_DQK
    gates = g_ref[...]
    gates_t = gates.T
    lf_hi, lf_lo = _split_hi_lo(gates)
    b_cols = _dot(tri_ref[...], jnp.concatenate([lf_hi, lf_lo], axis=0))
    lft_hi, lft_lo = _split_hi_lo(gates_t)
    b_rows = _dot(jnp.concatenate([lft_hi, lft_lo], axis=1), trit_ref[...])
    tot = jnp.sum(gates, axis=0, keepdims=True)
    allowed = msk_ref[...] > 0.0
    ones = jnp.ones((C, HEAD_DV), BF16)
    log2_scale = 0.5 * np.log2(ML_DQK)

    for h in range(N_HEADS):
        b_col = b_cols[:, N_HEADS + h:N_HEADS + h + 1]
        c_col = gates[:, h:h + 1] - b_col
        c_row = gates_t[h:h + 1, :] - b_rows[N_HEADS + h:N_HEADS + h + 1, :]
        m_prev = m_ref[h][0:1, 0:1]
        cm = jnp.where(allowed, c_row, NEG_BIG)
        m_t = jnp.maximum(jnp.max(cm, axis=-1, keepdims=True), m_prev)
        yield
        qh = qkv_ref[:, h * ML_DQK:(h + 1) * ML_DQK]
        kh = qkv_ref[:, qk_w + h * ML_DQK:qk_w + (h + 1) * ML_DQK]
        vh = qkv_ref[:, 2 * qk_w + h * HEAD_DV:2 * qk_w + (h + 1) * HEAD_DV]
        vaug = jnp.concatenate([vh, ones], axis=1)
        s = _dot_nt(qh, kh) * jnp.exp2(cm - (m_t + log2_scale))
        qs = (qh.astype(F32) * jnp.exp2(m_prev - m_t - log2_scale)).astype(BF16)
        yield
        c_aug = c_ref[h]
        r = _dot(jnp.concatenate([s.astype(BF16), qs], axis=1),
                 jnp.concatenate([vaug, c_aug.astype(BF16)], axis=0))
        num = r[:, :HEAD_DV]
        den = r[:, HEAD_DV:]
        hout = num / jnp.maximum(jnp.abs(den), jnp.exp2(-(b_col + m_t)))
        o_ref[:, h * HEAD_DV:(h + 1) * HEAD_DV] = hout.astype(o_ref.dtype)
        yield
        b_last = tot[:, N_HEADS + h:N_HEADS + h + 1]
        log_w = b_last + c_col
        m_new = jnp.maximum(b_last + m_prev, jnp.max(log_w, axis=0, keepdims=True))
        kw = (kh.astype(F32) * jnp.exp2(log_w - m_new)).astype(BF16)
        c_ref[h] = jnp.exp2(b_last + m_prev - m_new) * c_aug + _dot_tn(kw, vaug)
        m_ref[h] = jnp.broadcast_to(m_new, m_ref.shape[1:])
        yield


def _mixers_body(hqv_ref, hg_ref, lb_ref, mst_ref, hmsk_ref, mqkv_ref, mg_ref,
                 tri_ref, trit_ref, mmsk_ref, o_hg_ref, o_ml_ref, st_ref, c_ref, m_ref, *, C, n_lv):
    @pl.when(pl.program_id(2) == 0)
    def _():
        st_ref[...] = jnp.zeros_like(st_ref)
        c_ref[...] = jnp.zeros_like(c_ref)
        m_ref[...] = jnp.zeros_like(m_ref)

    def element(bb):
        gens = [_hgrn2_body(hqv_ref.at[bb], hg_ref.at[bb], lb_ref, mst_ref, hmsk_ref,
                            o_hg_ref.at[bb], st_ref.at[bb], C=C, n_lv=n_lv),
                _mlstm_body(mqkv_ref.at[bb], mg_ref.at[bb], tri_ref, trit_ref,
                            mmsk_ref, o_ml_ref.at[bb], c_ref.at[bb], m_ref.at[bb], C=C)]
        for _ in itertools.zip_longest(*gens):
            yield

    streams = [element(bb) for bb in range(hqv_ref.shape[0])]
    live = []
    while streams or live:
        if streams and (not live or live[-1][1] >= MIX_SKEW):
            live.append([streams.pop(0), 0])
        for item in list(live):
            try:
                next(item[0])
                item[1] += 1
            except StopIteration:
                live.remove(item)


def _mixers_call(l, hg_a, hg_f, lb4, ml_a, ml_g, lc):
    bsz, tt, _ = hg_a.shape
    C = CHUNK
    nch, nctx = tt // C, lc // C
    mst, hmsk, n_lv = _hgrn2_consts(C)
    mst = jnp.asarray(mst, BF16)
    hmsk = jnp.asarray(hmsk, F32)
    tri_cat, tri_t_cat, tri = _mlstm_consts(C)
    tri_cat = jnp.asarray(tri_cat, BF16)
    tri_t_cat = jnp.asarray(tri_t_cat, BF16)
    mmsk = jnp.asarray(tri, F32)
    cidx = functools.partial(_chunk_index, nctx=nctx, nch=nch)
    qk_w = N_HEADS * ML_DQK
    tok = lambda col: (lambda b, d, j: (b, cidx(d, j), col))
    tok_d = lambda b, d, j: (b, cidx(d, j), d)
    per_dir3 = lambda b, d, j: (d, 0, 0)
    nb = max(n for n in range(1, MIX_NB + 1) if bsz % n == 0)
    o_spec = pl.BlockSpec((nb, None, C, MIX_W), lambda b, d, j: (b, d, cidx(d, j), 0))
    o_shape = jax.ShapeDtypeStruct((bsz, 2, tt, MIX_W), BF16)
    return pl.pallas_call(
        functools.partial(_mixers_body, C=C, n_lv=n_lv),
        grid=(bsz // nb, 2, nch),
        in_specs=[
            pl.BlockSpec((nb, C, 2 * MIX_W), tok(0)),
            pl.BlockSpec((nb, C, MIX_W), tok_d),
            pl.BlockSpec((None, None, 1, MIX_W), lambda b, d, j: (l, d, 0, 0)),
            pl.BlockSpec((None,) + mst.shape[1:], per_dir3),
            pl.BlockSpec((None,) + hmsk.shape[1:], lambda b, d, j: (d, 0, 0, 0)),
            pl.BlockSpec((nb, C, 2 * qk_w + MIX_W), tok(0)),
            pl.BlockSpec((nb, C, GATE_LANES), tok_d),
            pl.BlockSpec((None, C, 2 * C), per_dir3),
            pl.BlockSpec((None, 2 * C, C), per_dir3),
            pl.BlockSpec((None, C, C), per_dir3),
        ],
        out_specs=[o_spec, o_spec],
        out_shape=[o_shape, o_shape],
        scratch_shapes=[pltpu.VMEM((nb, N_HEADS, HEAD_DV, HG_DK), F32),
                        pltpu.VMEM((nb, N_HEADS, ML_DQK, 2 * HEAD_DV), F32),
                        pltpu.VMEM((nb, N_HEADS, 8, 128), F32)],
        compiler_params=_cparams(("arbitrary", "arbitrary", "arbitrary")),
        name="mixers",
    )(hg_a, hg_f, lb4, mst, hmsk, ml_a, ml_g, tri_cat, tri_t_cat, mmsk)


def _lru_scan_body(a0_ref, u0_ref, a1_ref, u1_ref, hf_ref, hb_ref, sf_ref, sb_ref, tf_ref, tb_ref, *, tc):
    @pl.when(pl.program_id(0) == 0)
    def _():
        sf_ref[...] = jnp.zeros_like(sf_ref)
        sb_ref[...] = jnp.zeros_like(sb_ref)

    def step(i, carry):
        hf, hb = carry
        hf = a0_ref[:, i, :] * hf + u0_ref[:, i, :]
        tf_ref[:, i, :] = hf
        ib = tc - 1 - i
        hb = a1_ref[:, ib, :] * hb + u1_ref[:, ib, :]
        tb_ref[:, ib, :] = hb
        return hf, hb

    hf, hb = lax.fori_loop(0, tc, step, (sf_ref[...], sb_ref[...]), unroll=8)
    sf_ref[...] = hf
    sb_ref[...] = hb
    hf_ref[...] = tf_ref[...].astype(hf_ref.dtype)
    hb_ref[...] = tb_ref[...].astype(hb_ref.dtype)


def _lru_scan_call(a0, u0, a1, u1, lc):
    bsz, tt, w = a0.shape
    tc = CHUNK
    nch, nctx = tt // tc, lc // tc
    fw = pl.BlockSpec((bsz, tc, w), lambda j: (0, j, 0))
    bw = pl.BlockSpec((bsz, tc, w), lambda j: (0, _chunk_index(1, j, nctx, nch), 0))
    out = jax.ShapeDtypeStruct((bsz, tt, w), BF16)
    return pl.pallas_call(
        functools.partial(_lru_scan_body, tc=tc),
        grid=(nch,),
        in_specs=[fw, fw, bw, bw],
        out_specs=[fw, bw],
        out_shape=[out, out],
        scratch_shapes=[pltpu.VMEM((bsz, w), F32), pltpu.VMEM((bsz, w), F32),
                        pltpu.VMEM((bsz, tc, w), F32), pltpu.VMEM((bsz, tc, w), F32)],
        compiler_params=_cparams(("arbitrary",)),
        name="rglru_scan",
    )(a0, u0, a1, u1)


def _head_rms(o, w):
    parts = []
    for h in range(N_HEADS):
        oh = o[:, h * HEAD_DV:(h + 1) * HEAD_DV]
        parts.append(oh * lax.rsqrt(jnp.mean(oh * oh, axis=-1, keepdims=True) + EPS))
    return jnp.concatenate(parts, axis=1) * w


def _gelu_tanh(x):
    return 0.5 * x * (1.0 + jnp.tanh(0.7978845608028654 * (x + 0.044715 * (x * x * x))))


def _merge_rows(hg0_ref, hg1_ref, ml0_ref, ml1_ref, lf_ref, lb_ref, hgg_ref, mlo_ref, ly_ref, h_ref,
                res_ref, gate_ref, wmg_ref, bmg_ref, hgn_ref, mln_ref, wb_ref, wo_ref, o_ref,
                r0, rows, d_model):
    rs = slice(r0, r0 + rows)
    hgg = hgg_ref[rs, :].astype(F32)
    o_hg = hg0_ref[rs, :].astype(F32) + hg1_ref[rs, :].astype(F32)
    o_ml = ml0_ref[rs, :].astype(F32) + ml1_ref[rs, :].astype(F32)
    a_out = _head_rms(o_hg, hgn_ref[...]) * (hgg * _sigmoid_t(hgg))
    b_out = _head_rms(o_ml, mln_ref[...]) * _sigmoid_t(mlo_ref[rs, :].astype(F32))
    h_lru = lf_ref[rs, :].astype(F32) + lb_ref[rs, :].astype(F32)
    c_out = h_lru * _gelu_tanh(ly_ref[rs, :].astype(F32))
    h = h_ref[rs, :]
    yield
    merged = None
    for n, br in enumerate((a_out, b_out, c_out)):
        cs = slice(n * d_model, (n + 1) * d_model)
        gate = _sigmoid_t((_dot(h, wmg_ref[:, cs]) + bmg_ref[:, cs]).astype(BF16))
        term = gate * _dot(br.astype(BF16), wb_ref[n]).astype(BF16)
        merged = term if merged is None else merged + term
        yield
    o_ref[rs, :] = res_ref[rs, :] + gate_ref[...] * _dot(merged, wo_ref[...])
    yield


def _merge_body(*refs, rows, d_model):
    tm = refs[-1].shape[0]
    gens = [_merge_rows(*refs, r0, rows, d_model) for r0 in range(0, tm, rows)]
    n_phase = 5
    for step in range(n_phase + len(gens) - 1):
        for i, g in enumerate(gens):
            if 0 <= step - i < n_phase:
                next(g)


def _merge_call(l, is_ctx, tm, o_hg, o_ml, h_f, h_b, hg_a, ml_a, lru_y, h1, res, mods4,
                w_mg, b_mg, hg_norm, ml_norm, w_branch, w_out):
    bsz, n, d_model = res.shape
    t0 = 0 if is_ctx else 1
    rows = min(tm, MERGE_ROWS)

    def tok(col):
        return lambda b, t: (b, t + t0, col)

    mix = lambda d: pl.BlockSpec((None, None, tm, MIX_W), lambda b, t: (b, d, t + t0, 0))
    in_specs = [
        mix(0), mix(1), mix(0), mix(1),
        pl.BlockSpec((None, tm, MIX_W), tok(0)),
        pl.BlockSpec((None, tm, MIX_W), tok(0)),
        pl.BlockSpec((None, tm, MIX_W), tok(2)),
        pl.BlockSpec((None, tm, MIX_W), tok(2)),
        pl.BlockSpec((None, tm, MIX_W), tok(0)),
        pl.BlockSpec((None, tm, d_model), tok(0)),
        pl.BlockSpec((None, tm, d_model), lambda b, t: (b, t, 0)),
        pl.BlockSpec((None, None, 1, d_model), lambda b, t: (l, bsz if is_ctx else b, 0, 2)),
        _resident((None, d_model, 3 * d_model), lambda b, t: (l, 0, 0)),
        pl.BlockSpec((None, 1, 3 * d_model), lambda b, t: (l, 0, 0)),
        pl.BlockSpec((None, 1, MIX_W), lambda b, t: (l, 0, 0)),
        pl.BlockSpec((None, 1, MIX_W), lambda b, t: (l, 0, 0)),
        _resident((None, 3, MIX_W, d_model), lambda b, t: (l, 0, 0, 0)),
        _resident((None, d_model, d_model), lambda b, t: (l, 0, 0)),
    ]
    return pl.pallas_call(
        functools.partial(_merge_body, rows=rows, d_model=d_model),
        grid=(bsz, n // tm),
        in_specs=in_specs,
        out_specs=pl.BlockSpec((None, tm, d_model), lambda b, t: (b, t, 0)),
        out_shape=jax.ShapeDtypeStruct(res.shape, F32),
        compiler_params=_cparams(("arbitrary", "arbitrary")),
        name="merge",
    )(o_hg, o_hg, o_ml, o_ml, h_f, h_b, hg_a, ml_a, lru_y, h1, res, mods4,
      w_mg, b_mg, hg_norm, ml_norm, w_branch, w_out)


def _ffn_rows(x_ref, sh_ref, sc_ref, gt_ref, ln_ref, wi_ref, wo_ref, fn_ref, o_ref, r0, rows, *,
              n_inner, final_norm):
    x = x_ref[r0:r0 + rows, :]
    h = _norm_mod(x, ln_ref[...], sh_ref[...], sc_ref[...]).astype(BF16)
    yield
    gu = _dot(h, wi_ref[...])
    yield
    hidden = gu.shape[-1] // 2
    gate = gu[:, :hidden]
    act = (gate * _sigmoid_t(gate) * gu[:, hidden:]).astype(BF16)
    yield
    f = _dot(act, wo_ref[...])
    yield
    y = x + gt_ref[...] * f
    if final_norm:
        y = y * lax.rsqrt(jnp.mean(y * y, axis=-1, keepdims=True) + EPS) * fn_ref[...]
    if n_inner is None:
        o_ref[r0:r0 + rows, :] = y
    else:
        for a in range(rows // n_inner):
            o_ref[:, r0 // n_inner + a, :] = y[a * n_inner:(a + 1) * n_inner, :]
    yield


def _ffn_body(x_ref, sh_ref, sc_ref, gt_ref, ln_ref, wi_ref, wo_ref, fn_ref, o_ref, *, rows, n_inner,
              final_norm):
    gens = [_ffn_rows(x_ref, sh_ref, sc_ref, gt_ref, ln_ref, wi_ref, wo_ref, fn_ref, o_ref, r0, rows,
                      n_inner=n_inner, final_norm=final_norm)
            for r0 in range(0, x_ref.shape[0], rows)]
    skew = 2
    n_phase = 5
    for step in range(n_phase + skew * (len(gens) - 1)):
        for i, g in enumerate(gens):
            if 0 <= step - skew * i < n_phase:
                next(g)


def _ffn_call(l, x, mods4, ctx_row, ln2, wi, wo, fin, *, tm, n_inner, final_norm):
    bsz, n, d_model = x.shape
    hidden = wo.shape[1]
    nt = n // tm
    rows = min(tm, FFN_ROWS)

    def mod_spec(chunk):
        return pl.BlockSpec((None, None, 1, d_model),
                            lambda b, t: (l, bsz if ctx_row else b, 0, chunk))

    if n_inner is None:
        o_spec = pl.BlockSpec((None, tm, d_model), lambda b, t: (b, t, 0))
        o_shape = jax.ShapeDtypeStruct(x.shape, F32)
    else:
        assert rows % n_inner == 0
        n_a = tm // n_inner
        o_spec = pl.BlockSpec((None, n_inner, n_a, d_model), lambda b, t: (b, 0, t, 0))
        o_shape = jax.ShapeDtypeStruct((bsz, n_inner, n // n_inner, d_model), F32)
    out = pl.pallas_call(
        functools.partial(_ffn_body, rows=rows, n_inner=n_inner, final_norm=final_norm),
        grid=(bsz, nt),
        in_specs=[
            pl.BlockSpec((None, tm, d_model), lambda b, t: (b, t, 0)),
            mod_spec(3), mod_spec(4), mod_spec(5),
            pl.BlockSpec((None, 1, d_model), lambda b, t: (l, 0, 0)),
            _resident((None, d_model, 2 * hidden), lambda b, t: (l, 0, 0)),
            _resident((None, hidden, d_model), lambda b, t: (l, 0, 0)),
            pl.BlockSpec((1, d_model), lambda b, t: (0, 0)),
        ],
        out_specs=o_spec,
        out_shape=o_shape,
        compiler_params=_cparams(("arbitrary", "arbitrary")),
        name="ffn",
    )(x, mods4, mods4, mods4, ln2, wi, wo, fin)
    return out.reshape(bsz, n, d_model)


def _prep_in_proj(w_in, b_in):
    depth, d_model, _ = w_in.shape
    hq = N_HEADS * HG_DK
    sizes = (hq, MIX_W, MIX_W, 2 * hq, N_HEADS * ML_DQK, N_HEADS * ML_DQK, MIX_W, MIX_W,
             2 * N_HEADS, 2 * N_HEADS, MIX_W, MIX_W, 3 * d_model)
    offs = np.concatenate([[0], np.cumsum(sizes)])
    assert offs[-1] == w_in.shape[-1]

    def cols(a, lo, hi):
        return a[..., int(offs[lo]):int(offs[hi])]

    def gate_block(a):
        ig = cols(a, 8, 9).reshape(a.shape[:-1] + (2, N_HEADS))
        fg = cols(a, 9, 10).reshape(a.shape[:-1] + (2, N_HEADS))
        pad = jnp.zeros(a.shape[:-1] + (2, GATE_LANES - 2 * N_HEADS), a.dtype)
        return jnp.concatenate([ig, fg, pad], axis=-1).reshape(a.shape[:-1] + (2 * GATE_LANES,))

    groups = [
        (lambda a: cols(a, 0, 3), BF16),
        (lambda a: cols(a, 3, 4), F32),
        (lambda a: cols(a, 4, 8), BF16),
        (gate_block, F32),
        (lambda a: cols(a, 10, 11), F32),
        (lambda a: cols(a, 11, 12), BF16),
    ]
    merge_gates = lambda a: cols(a, 12, 13)
    kinds = ("hgrn2_qig", "hgrn2_forget", None, "mlstm_gates", "lru_x", None)
    b3 = b_in.reshape(depth, 1, -1)
    ws = [f(w_in).astype(BF16) for f, _ in groups]
    bs = [f(b3) for f, _ in groups]
    w_mg = merge_gates(w_in).astype(BF16)
    b_mg = merge_gates(b3)
    return ws, bs, [dt for _, dt in groups], kinds, w_mg, b_mg


def _prep_lru_gates(lru_gate_w, lru_gate_b):
    depth = lru_gate_w.shape[0]
    eye = jnp.eye(LRU_BLOCKS, dtype=lru_gate_w.dtype)
    dense = jnp.einsum("lzgnde,nm->lndzgme", lru_gate_w, eye)
    dense = dense.reshape(depth, MIX_W, 4 * MIX_W).astype(BF16)
    return dense, lru_gate_b.reshape(depth, 1, 4 * MIX_W)


def kernel(x, c, ctx, c_ctx, w_ada, b_ada, ln1, w_in, b_in, hg_lb_raw, hg_norm, ml_norm, conv_w, conv_b,
           lru_gate_w, lru_gate_b, lru_lambda, w_branch, w_out, ln2, w_ffn_in, w_ffn_out, final_norm):
    bsz, seq, d_model = x.shape
    lc = ctx.shape[1]
    depth = w_ada.shape[0]
    rows = seq // GRID_W
    assert lc % CHUNK == 0 and seq % lc == 0 and seq % FFN_TM == 0 and bsz < 16

    c_all = jnp.zeros((16, d_model), F32).at[:bsz].set(c).at[bsz].set(c_ctx)
    mods4 = _ada_call(c_all, w_ada, b_ada).reshape(depth, 16, 1, 6 * d_model)
    lb_all = _lb_call(hg_lb_raw)
    lb3 = lb_all.reshape(depth, 1, 2 * MIX_W)
    lb4 = lb_all.reshape(depth, 2, 1, MIX_W)

    ws, bs, out_dtypes, kinds, w_mg, b_mg = _prep_in_proj(w_in, b_in)
    wgate, bgate = _prep_lru_gates(lru_gate_w, lru_gate_b)
    w_ffn_in_b = w_ffn_in.astype(BF16)
    w_ffn_out_b = w_ffn_out.astype(BF16)
    w_branch_b = w_branch.astype(BF16)
    w_out_b = w_out.astype(BF16)
    ln1_3 = ln1.reshape(depth, 1, d_model)
    ln2_3 = ln2.reshape(depth, 1, d_model)
    hgn = hg_norm.reshape(depth, 1, MIX_W)
    mln = ml_norm.reshape(depth, 1, MIX_W)
    conv_b3 = conv_b.reshape(depth, 1, MIX_W)
    fin = final_norm.reshape(1, d_model)

    for l in range(depth):
        last = l == depth - 1
        n_inner = GRID_W if l % 2 == 0 else rows
        hg_a, hg_f, ml_a, ml_g, lru_y, h1, a0, u0, a1, u1 = _in_proj_call(
            l, ctx, x, mods4, ln1_3, lb3, ws, bs, out_dtypes, kinds, conv_w, conv_b3, wgate, bgate, lru_lambda)
        o_hg, o_ml = _mixers_call(l, hg_a, hg_f, lb4, ml_a, ml_g, lc)
        h_f, h_b = _lru_scan_call(a0, u0, a1, u1, lc)
        branches = (o_hg, o_ml, h_f, h_b, hg_a, ml_a, lru_y, h1)
        merge_w = (w_mg, b_mg, hgn, mln, w_branch_b, w_out_b)
        x_m = _merge_call(l, False, lc, *branches, x, mods4, *merge_w)
        x = _ffn_call(l, x_m, mods4, False, ln2_3, w_ffn_in_b, w_ffn_out_b, fin,
                      tm=FFN_TM, n_inner=n_inner, final_norm=last)
        if not last:
            ctx_m = _merge_call(l, True, lc, *branches, ctx, mods4, *merge_w)
            ctx = _ffn_call(l, ctx_m, mods4, True, ln2_3, w_ffn_in_b, w_ffn_out_b, fin,
                            tm=lc, n_inner=None, final_norm=False)
    return x
```

```python
import functools
import itertools

import numpy as np
import jax
import jax.numpy as jnp
from jax import lax
from jax.experimental import pallas as pl
from jax.experimental.pallas import tpu as pltpu

F32 = jnp.float32
BF16 = jnp.bfloat16

GRID_W = 64
MIX_W = 512
N_HEADS = 4
HEAD_DV = 128
HG_DK = 128
ML_DQK = 64
LRU_BLOCKS = 8
LRU_BD = 64
LRU_C = 8.0
EPS = 1e-6
NEG_BIG = -1e30
LB_TINY = 1e-30
LOG2_E = 1.4426950408889634
GATE_LANES = 128
CHUNK = 128
HG_GROUP = 1
MIX_NB = 4
MIX_SKEW = 14
FFN_TM = 512
FFN_ROWS = 256
MERGE_ROWS = 128
IN_PROJ_ROWS = 128
VMEM_LIMIT = 56 * 1024 * 1024


def _cparams(sem):
    return pltpu.CompilerParams(dimension_semantics=sem, vmem_limit_bytes=VMEM_LIMIT)


def _resident(shape, index_map):
    return pl.BlockSpec(shape, index_map, pipeline_mode=pl.Buffered(1))


def _sigmoid(x):
    return 1.0 / (1.0 + jnp.exp(-x))


def _sigmoid_t(x):
    return 0.5 * jnp.tanh(0.5 * x) + 0.5


def _log_sigmoid(x):
    return jnp.minimum(x, 0.0) - jnp.log(1.0 + jnp.exp(-jnp.abs(x)))


def _dot(a, b):
    return jnp.dot(a, b, preferred_element_type=F32)


def _dot_nt(a, b):
    return lax.dot_general(a, b, (((1,), (1,)), ((), ())), preferred_element_type=F32)


def _dot_tn(a, b):
    return lax.dot_general(a, b, (((0,), (0,)), ((), ())), preferred_element_type=F32)


def _split_hi_lo(x):
    hi = x.astype(BF16)
    lo = (x - hi.astype(F32)).astype(BF16)
    return hi, lo


def _norm_mod(x, ln, shift, scale):
    gain = ln * (1.0 + scale)
    return (x * lax.rsqrt(jnp.mean(x * x, axis=-1, keepdims=True) + EPS)) * gain + shift


def _mirror(m):
    return m[..., ::-1, ::-1].copy()


@functools.lru_cache(maxsize=None)
def _hgrn2_consts(C):
    n_lv = int(np.log2(C))
    mats, masks = [], []
    r = np.arange(C)
    for lv in range(n_lv):
        s = C >> (lv + 1)
        base = (r // (2 * s)) * (2 * s)
        mid = base + s - 1
        odd = r >= base + s
        m = np.zeros((C, C), np.float32)
        for t in range(C):
            if odd[t]:
                m[t, mid[t] + 1:t + 1] = 1.0
            else:
                m[t, t + 1:mid[t] + 1] = 1.0
        mats.append(m)
        same = base[:, None] == base[None, :]
        masks.append((same & odd[:, None] & (~odd)[None, :]).astype(np.float32))
    mats.append(np.tril(np.ones((C, C), np.float32)))
    masks.append(np.eye(C, dtype=np.float32))
    mats = np.stack(mats)
    masks = np.stack(masks)
    mats = np.stack([mats, _mirror(mats)])
    masks = np.stack([masks, _mirror(masks)])
    mst = mats.reshape(2, (n_lv + 1) * C, C)
    mst = np.concatenate([mst, mst], axis=-1)
    return mst, masks, n_lv


@functools.lru_cache(maxsize=None)
def _mlstm_consts(C):
    tri = np.tril(np.ones((C, C), np.float32))
    tri = np.stack([tri, _mirror(tri)])
    tri_cat = np.concatenate([tri, tri], axis=-1)
    tri_t = np.transpose(tri, (0, 2, 1))
    tri_t_cat = np.concatenate([tri_t, tri_t], axis=1)
    return tri_cat, tri_t_cat, tri


def _chunk_index(d, j, nctx, nch):
    bw = jnp.where(j < nctx, nctx - 1 - j, nch - 1 + nctx - j)
    return jnp.where(d == 0, j, bw)


def _ada_body(c_ref, w_ref, b_ref, o_ref):
    cc = c_ref[...]
    s = cc * _sigmoid(cc)
    o_ref[...] = jnp.dot(s, w_ref[...], preferred_element_type=F32,
                         precision=lax.Precision.HIGHEST) + b_ref[...]


def _ada_call(c_all, w_ada, b_ada):
    depth, d_model, n6 = w_ada.shape
    rows = c_all.shape[0]
    tn = 1536
    return pl.pallas_call(
        _ada_body,
        grid=(depth, n6 // tn),
        in_specs=[
            pl.BlockSpec((rows, d_model), lambda l, n: (0, 0)),
            pl.BlockSpec((None, d_model, tn), lambda l, n: (l, 0, n)),
            pl.BlockSpec((None, 1, tn), lambda l, n: (l, 0, n)),
        ],
        out_specs=pl.BlockSpec((None, rows, tn), lambda l, n: (l, 0, n)),
        out_shape=jax.ShapeDtypeStruct((depth, rows, n6), F32),
        compiler_params=_cparams(("arbitrary", "arbitrary")),
        name="ada_mod",
    )(c_all, w_ada, b_ada.reshape(depth, 1, n6))


def _lb_body(raw_ref, o_ref):
    raw = raw_ref[...]
    depth = raw.shape[0]
    e = jnp.exp(raw - jnp.max(raw, axis=0, keepdims=True))
    p = e / jnp.sum(e, axis=0, keepdims=True)
    acc = jnp.zeros_like(p[0:1])
    for l in range(depth):
        acc = acc + p[l:l + 1]
        o_ref[l:l + 1, :] = acc - p[0:1]


def _lb_call(hg_lb_raw):
    depth = hg_lb_raw.shape[0]
    raw = hg_lb_raw.reshape(depth, -1)
    return pl.pallas_call(
        _lb_body,
        out_shape=jax.ShapeDtypeStruct(raw.shape, F32),
        name="hgrn2_lower_bounds",
    )(raw)


def _log2_forget(f_pre, lb):
    half = 0.5 * (1.0 - lb)
    floor = jnp.maximum(lb, LB_TINY)
    return jnp.log2(jnp.maximum((floor + half) + half * jnp.tanh(0.5 * f_pre), floor))


def _lru_decay_input(r_pre, i_pre, lam, xc):
    z = -lam
    softplus = jnp.maximum(z, 0.0) + jnp.log(1.0 + jnp.exp(-jnp.abs(z)))
    log_a = (-LRU_C * softplus) * _sigmoid_t(r_pre)
    a = jnp.exp(log_a)
    one_m_a2 = -jnp.tanh(log_a) * (a * a + 1.0)
    return a, jnp.sqrt(jnp.maximum(one_m_a2, 0.0)) * (_sigmoid_t(i_pre) * xc)


def _in_proj_body(ctx_ref, x_ref, xp_ref, xn_ref, sh_ref, sc_ref, ln_ref, lb_ref,
                  cw_ref, cb_ref, wg_ref, bg_ref, lam_ref, *rest, kinds):
    n_grp = len(kinds)
    w_refs = rest[:n_grp]
    b_refs = rest[n_grp:2 * n_grp]
    outs = list(rest[2 * n_grp:])
    ext_ref = outs.pop()
    lru_refs = [outs.pop() for _ in range(4)][::-1]
    h_ref = outs.pop()
    o_refs = iter(outs)
    o_refs = [None if kind == "lru_x" else next(o_refs) for kind in kinds]
    t = pl.program_id(1)
    tm = x_ref.shape[0]
    rows = min(tm, IN_PROJ_ROWS)
    lru_i = kinds.index("lru_x")

    def norm(x):
        return _norm_mod(x, ln_ref[...], sh_ref[...], sc_ref[...]).astype(BF16)

    def finish(kind, y, o_ref, r0):
        rs = slice(r0, r0 + rows)
        if kind == "hgrn2_qig":
            q = y[:, :MIX_W]
            o_ref[rs, :MIX_W] = (q * _sigmoid_t(q)).astype(o_ref.dtype)
            o_ref[rs, MIX_W:] = y[:, MIX_W:].astype(o_ref.dtype)
        elif kind == "hgrn2_forget":
            o_ref[rs, :] = _log2_forget(y, lb_ref[...]).astype(o_ref.dtype)
        elif kind == "mlstm_gates":
            lane = lax.broadcasted_iota(jnp.int32, y.shape, 1) % GATE_LANES
            is_f = jnp.logical_and(lane >= N_HEADS, lane < 2 * N_HEADS)
            o_ref[rs, :] = (jnp.where(is_f, _log_sigmoid(y), y) * LOG2_E).astype(o_ref.dtype)
        elif kind == "lru_x":
            ext_ref[8 + r0:8 + r0 + rows, :] = y
        else:
            o_ref[rs, :] = y.astype(o_ref.dtype)

    def row_group(r0):
        rs = slice(r0, r0 + rows)
        h = norm(jnp.where(t == 0, ctx_ref[rs, :], x_ref[rs, :]))
        h_ref[rs, :] = h
        yield
        rest_order = sorted((i for i in range(n_grp) if i != lru_i), key=lambda i: kinds[i] is None)
        for i in [lru_i] + rest_order:
            finish(kinds[i], _dot(h, w_refs[i][...]) + b_refs[i][...], o_refs[i], r0)
            yield

    def lru_gates(wait):
        prev_ok = t >= 2
        next_ok = jnp.logical_and(t >= 1, t < pl.num_programs(1) - 1)
        w_lx, b_lx = w_refs[lru_i][...], b_refs[lru_i][...]
        ext_ref[0:8, :] = jnp.where(prev_ok, _dot(norm(xp_ref[...]), w_lx) + b_lx, 0.0)
        ext_ref[8 + tm:16 + tm, :] = jnp.where(next_ok, _dot(norm(xn_ref[...]), w_lx) + b_lx, 0.0)
        for _ in range(wait):
            yield
        cw = cw_ref[...]
        xc = (cw[0:1] * ext_ref[6:6 + tm, :] + cw[1:2] * ext_ref[7:7 + tm, :] + cw[2:3] * ext_ref[8:8 + tm, :]
              + cw[3:4] * ext_ref[9:9 + tm, :]) + cb_ref[...]
        yield
        gates = _dot(xc.astype(BF16), wg_ref[...]) + bg_ref[...]
        yield
        for d in range(2):
            for r0 in range(0, tm, rows):
                rs = slice(r0, r0 + rows)
                a, u = _lru_decay_input(gates[rs, (2 * d) * MIX_W:(2 * d + 1) * MIX_W],
                                        gates[rs, (2 * d + 1) * MIX_W:(2 * d + 2) * MIX_W],
                                        lam_ref[d:d + 1, :], xc[rs, :])
                lru_refs[2 * d][rs, :] = a
                lru_refs[2 * d + 1][rs, :] = u
                yield

    groups = [row_group(r0) for r0 in range(0, tm, rows)]
    gens = groups + [lru_gates(wait=len(groups) + 1)]
    live = list(gens)
    step = 0
    while live:
        for i, g in enumerate(gens):
            if g in live and (step >= i or g is gens[-1]):
                try:
                    next(g)
                except StopIteration:
                    live.remove(g)
        step += 1


def _in_proj_call(l, ctx, x, mods4, ln1, lb3, ws, bs, out_dtypes, kinds, conv_w, conv_b, wgate, bgate, lam):
    bsz, lc, d_model = ctx.shape
    seq = x.shape[1]
    tm = lc
    nt = 1 + seq // tm
    h8 = tm // 8

    def mod_spec(chunk):
        return pl.BlockSpec((None, None, 1, d_model),
                            lambda b, t: (l, jnp.where(t == 0, bsz, b), 0, chunk))

    def per_layer(shape):
        return pl.BlockSpec((None,) + shape, lambda b, t: (l,) + (0,) * len(shape))

    in_specs = [
        pl.BlockSpec((None, tm, d_model), lambda b, t: (b, 0, 0)),
        pl.BlockSpec((None, tm, d_model), lambda b, t: (b, jnp.maximum(t - 1, 0), 0)),
        pl.BlockSpec((None, 8, d_model), lambda b, t: (b, jnp.maximum((t - 1) * h8 - 1, 0), 0)),
        pl.BlockSpec((None, 8, d_model), lambda b, t: (b, jnp.minimum(jnp.maximum(t, 1) * h8, seq // 8 - 1), 0)),
        mod_spec(0), mod_spec(1),
        per_layer((1, d_model)),
        per_layer((1, lb3.shape[-1])),
        per_layer((4, MIX_W)), per_layer((1, MIX_W)),
        per_layer((MIX_W, 4 * MIX_W)), per_layer((1, 4 * MIX_W)), per_layer((2, MIX_W)),
    ]
    in_specs += [per_layer((d_model, w.shape[-1])) for w in ws]
    in_specs += [per_layer((1, w.shape[-1])) for w in ws]
    tok = lambda width: pl.BlockSpec((None, tm, width), lambda b, t: (b, t, 0))
    out_specs, out_shape = [], []
    for w, dt, kind in zip(ws, out_dtypes, kinds):
        if kind != "lru_x":
            out_specs.append(tok(w.shape[-1]))
            out_shape.append(jax.ShapeDtypeStruct((bsz, lc + seq, w.shape[-1]), dt))
    out_specs.append(tok(d_model))
    out_shape.append(jax.ShapeDtypeStruct((bsz, lc + seq, d_model), BF16))
    out_specs += [tok(MIX_W)] * 4
    out_shape += [jax.ShapeDtypeStruct((bsz, lc + seq, MIX_W), F32)] * 4
    return pl.pallas_call(
        functools.partial(_in_proj_body, kinds=kinds),
        grid=(bsz, nt),
        in_specs=in_specs,
        out_specs=out_specs,
        out_shape=out_shape,
        scratch_shapes=[pltpu.VMEM((tm + 16, MIX_W), F32)],
        compiler_params=_cparams(("arbitrary", "arbitrary")),
        name="in_proj",
    )(ctx, x, x, x, mods4, mods4, ln1, lb3, conv_w, conv_b, wgate, bgate, lam, *ws, *bs)


def _hgrn2_body(qv_ref, g_ref, lb_ref, mst_ref, msk_ref, o_ref, st_ref, *, C, n_lv):
    q16 = qv_ref[:, :MIX_W]
    v = qv_ref[:, MIX_W:]
    q = q16.astype(F32)
    g = g_ref[...]
    lb = lb_ref[...]
    kk = (1.0 - jnp.exp2(g)) + (jnp.maximum(lb, LB_TINY) - lb)
    g_hi, g_lo = _split_hi_lo(g)
    ex = _dot(mst_ref[...], jnp.concatenate([g_hi, g_lo], axis=0))
    tot = jnp.sum(g, axis=0, keepdims=True)
    k16 = kk.astype(BF16)

    for h0 in range(0, N_HEADS, HG_GROUP):
        heads = range(h0, h0 + HG_GROUP)
        sls = {h: slice(h * HG_DK, (h + 1) * HG_DK) for h in heads}
        p = {h: msk_ref[n_lv] * _dot_nt(q16[:, sls[h]], k16[:, sls[h]]) for h in heads}
        for lv in range(n_lv):
            for h in heads:
                w = jnp.exp2(ex[lv * C:(lv + 1) * C, sls[h]]).astype(BF16)
                p[h] = p[h] + msk_ref[lv] * _dot_nt(q16[:, sls[h]] * w, k16[:, sls[h]] * w)
            if lv % 2 == 1:
                yield
        for h in heads:
            sl = sls[h]
            st = st_ref[h]
            b_in = ex[n_lv * C:(n_lv + 1) * C, sl]
            qb = (q[:, sl] * jnp.exp2(b_in)).astype(BF16)
            vh = v[:, sl]
            o_ref[:, sl] = (_dot(p[h].astype(BF16), vh) + _dot_nt(qb, st.astype(BF16))).astype(o_ref.dtype)
            tot_h = tot[:, sl]
            kb = (kk[:, sl] * jnp.exp2(tot_h - b_in)).astype(BF16)
            st_ref[h] = st * jnp.exp2(tot_h) + _dot_tn(vh, kb)
            yield


def _mlstm_body(qkv_ref, g_ref, tri_ref, trit_ref, msk_ref, o_ref, c_ref, m_ref, *, C):
    qk_w = N_HEADS * ML_DQK
    gates = g_ref[...]
    gates_t = gates.T
    lf_hi, lf_lo = _split_hi_lo(gates)
    b_cols = _dot(tri_ref[...], jnp.concatenate([lf_hi, lf_lo], axis=0))
    lft_hi, lft_lo = _split_hi_lo(gates_t)
    b_rows = _dot(jnp.concatenate([lft_hi, lft_lo], axis=1), trit_ref[...])
    tot = jnp.sum(gates, axis=0, keepdims=True)
    allowed = msk_ref[...] > 0.0
    ones = jnp.ones((C, HEAD_DV), BF16)
    log2_scale = 0.5 * np.log2(ML_DQK)

    for h in range(N_HEADS):
        b_col = b_cols[:, N_HEADS + h:N_HEADS + h + 1]
        c_col = gates[:, h:h + 1] - b_col
        c_row = gates_t[h:h + 1, :] - b_rows[N_HEADS + h:N_HEADS + h + 1, :]
        m_prev = m_ref[h][0:1, 0:1]
        cm = jnp.where(allowed, c_row, NEG_BIG)
        m_t = jnp.maximum(jnp.max(cm, axis=-1, keepdims=True), m_prev)
        yield
        qh = qkv_ref[:, h * ML_DQK:(h + 1) * ML_DQK]
        kh = qkv_ref[:, qk_w + h * ML_DQK:qk_w + (h + 1) * ML_DQK]
        vh = qkv_ref[:, 2 * qk_w + h * HEAD_DV:2 * qk_w + (h + 1) * HEAD_DV]
        vaug = jnp.concatenate([vh, ones], axis=1)
        s = _dot_nt(qh, kh) * jnp.exp2(cm - (m_t + log2_scale))
        qs = (qh.astype(F32) * jnp.exp2(m_prev - m_t - log2_scale)).astype(BF16)
        yield
        c_aug = c_ref[h]
        r = _dot(jnp.concatenate([s.astype(BF16), qs], axis=1),
                 jnp.concatenate([vaug, c_aug.astype(BF16)], axis=0))
        num = r[:, :HEAD_DV]
        den = r[:, HEAD_DV:]
        hout = num / jnp.maximum(jnp.abs(den), jnp.exp2(-(b_col + m_t)))
        o_ref[:, h * HEAD_DV:(h + 1) * HEAD_DV] = hout.astype(o_ref.dtype)
        yield
        b_last = tot[:, N_HEADS + h:N_HEADS + h + 1]
        log_w = b_last + c_col
        m_new = jnp.maximum(b_last + m_prev, jnp.max(log_w, axis=0, keepdims=True))
        kw = (kh.astype(F32) * jnp.exp2(log_w - m_new)).astype(BF16)
        c_ref[h] = jnp.exp2(b_last + m_prev - m_new) * c_aug + _dot_tn(kw, vaug)
        m_ref[h] = jnp.broadcast_to(m_new, m_ref.shape[1:])
        yield


def _mixers_body(hqv_ref, hg_ref, lb_ref, mst_ref, hmsk_ref, mqkv_ref, mg_ref,
                 tri_ref, trit_ref, mmsk_ref, o_hg_ref, o_ml_ref, st_ref, c_ref, m_ref, *, C, n_lv):
    @pl.when(pl.program_id(2) == 0)
    def _():
        st_ref[...] = jnp.zeros_like(st_ref)
        c_ref[...] = jnp.zeros_like(c_ref)
        m_ref[...] = jnp.zeros_like(m_ref)

    def element(bb):
        gens = [_hgrn2_body(hqv_ref.at[bb], hg_ref.at[bb], lb_ref, mst_ref, hmsk_ref,
                            o_hg_ref.at[bb], st_ref.at[bb], C=C, n_lv=n_lv),
                _mlstm_body(mqkv_ref.at[bb], mg_ref.at[bb], tri_ref, trit_ref,
                            mmsk_ref, o_ml_ref.at[bb], c_ref.at[bb], m_ref.at[bb], C=C)]
        for _ in itertools.zip_longest(*gens):
            yield

    streams = [element(bb) for bb in range(hqv_ref.shape[0])]
    live = []
    while streams or live:
        if streams and (not live or live[-1][1] >= MIX_SKEW):
            live.append([streams.pop(0), 0])
        for item in list(live):
            try:
                next(item[0])
                item[1] += 1
            except StopIteration:
                live.remove(item)


def _mixers_call(l, hg_a, hg_f, lb4, ml_a, ml_g, lc):
    bsz, tt, _ = hg_a.shape
    C = CHUNK
    nch, nctx = tt // C, lc // C
    mst, hmsk, n_lv = _hgrn2_consts(C)
    mst = jnp.asarray(mst, BF16)
    hmsk = jnp.asarray(hmsk, F32)
    tri_cat, tri_t_cat, tri = _mlstm_consts(C)
    tri_cat = jnp.asarray(tri_cat, BF16)
    tri_t_cat = jnp.asarray(tri_t_cat, BF16)
    mmsk = jnp.asarray(tri, F32)
    cidx = functools.partial(_chunk_index, nctx=nctx, nch=nch)
    qk_w = N_HEADS * ML_DQK
    tok = lambda col: (lambda b, d, j: (b, cidx(d, j), col))
    tok_d = lambda b, d, j: (b, cidx(d, j), d)
    per_dir3 = lambda b, d, j: (d, 0, 0)
    nb = max(n for n in range(1, MIX_NB + 1) if bsz % n == 0)
    o_spec = pl.BlockSpec((nb, None, C, MIX_W), lambda b, d, j: (b, d, cidx(d, j), 0))
    o_shape = jax.ShapeDtypeStruct((bsz, 2, tt, MIX_W), BF16)
    return pl.pallas_call(
        functools.partial(_mixers_body, C=C, n_lv=n_lv),
        grid=(bsz // nb, 2, nch),
        in_specs=[
            pl.BlockSpec((nb, C, 2 * MIX_W), tok(0)),
            pl.BlockSpec((nb, C, MIX_W), tok_d),
            pl.BlockSpec((None, None, 1, MIX_W), lambda b, d, j: (l, d, 0, 0)),
            pl.BlockSpec((None,) + mst.shape[1:], per_dir3),
            pl.BlockSpec((None,) + hmsk.shape[1:], lambda b, d, j: (d, 0, 0, 0)),
            pl.BlockSpec((nb, C, 2 * qk_w + MIX_W), tok(0)),
            pl.BlockSpec((nb, C, GATE_LANES), tok_d),
            pl.BlockSpec((None, C, 2 * C), per_dir3),
            pl.BlockSpec((None, 2 * C, C), per_dir3),
            pl.BlockSpec((None, C, C), per_dir3),
        ],
        out_specs=[o_spec, o_spec],
        out_shape=[o_shape, o_shape],
        scratch_shapes=[pltpu.VMEM((nb, N_HEADS, HEAD_DV, HG_DK), F32),
                        pltpu.VMEM((nb, N_HEADS, ML_DQK, 2 * HEAD_DV), F32),
                        pltpu.VMEM((nb, N_HEADS, 8, 128), F32)],
        compiler_params=_cparams(("arbitrary", "arbitrary", "arbitrary")),
        name="mixers",
    )(hg_a, hg_f, lb4, mst, hmsk, ml_a, ml_g, tri_cat, tri_t_cat, mmsk)


def _lru_scan_body(a0_ref, u0_ref, a1_ref, u1_ref, hf_ref, hb_ref, sf_ref, sb_ref, tf_ref, tb_ref, *, tc):
    @pl.when(pl.program_id(0) == 0)
    def _():
        sf_ref[...] = jnp.zeros_like(sf_ref)
        sb_ref[...] = jnp.zeros_like(sb_ref)

    def step(i, carry):
        hf, hb = carry
        hf = a0_ref[:, i, :] * hf + u0_ref[:, i, :]
        tf_ref[:, i, :] = hf
        ib = tc - 1 - i
        hb = a1_ref[:, ib, :] * hb + u1_ref[:, ib, :]
        tb_ref[:, ib, :] = hb
        return hf, hb

    hf, hb = lax.fori_loop(0, tc, step, (sf_ref[...], sb_ref[...]), unroll=8)
    sf_ref[...] = hf
    sb_ref[...] = hb
    hf_ref[...] = tf_ref[...].astype(hf_ref.dtype)
    hb_ref[...] = tb_ref[...].astype(hb_ref.dtype)


def _lru_scan_call(a0, u0, a1, u1, lc):
    bsz, tt, w = a0.shape
    tc = CHUNK
    nch, nctx = tt // tc, lc // tc
    fw = pl.BlockSpec((bsz, tc, w), lambda j: (0, j, 0))
    bw = pl.BlockSpec((bsz, tc, w), lambda j: (0, _chunk_index(1, j, nctx, nch), 0))
    out = jax.ShapeDtypeStruct((bsz, tt, w), BF16)
    return pl.pallas_call(
        functools.partial(_lru_scan_body, tc=tc),
        grid=(nch,),
        in_specs=[fw, fw, bw, bw],
        out_specs=[fw, bw],
        out_shape=[out, out],
        scratch_shapes=[pltpu.VMEM((bsz, w), F32), pltpu.VMEM((bsz, w), F32),
                        pltpu.VMEM((bsz, tc, w), F32), pltpu.VMEM((bsz, tc, w), F32)],
        compiler_params=_cparams(("arbitrary",)),
        name="rglru_scan",
    )(a0, u0, a1, u1)


def _head_rms(o, w):
    parts = []
    for h in range(N_HEADS):
        oh = o[:, h * HEAD_DV:(h + 1) * HEAD_DV]
        parts.append(oh * lax.rsqrt(jnp.mean(oh * oh, axis=-1, keepdims=True) + EPS))
    return jnp.concatenate(parts, axis=1) * w


def _gelu_tanh(x):
    return 0.5 * x * (1.0 + jnp.tanh(0.7978845608028654 * (x + 0.044715 * (x * x * x))))


def _merge_rows(hg0_ref, hg1_ref, ml0_ref, ml1_ref, lf_ref, lb_ref, hgg_ref, mlo_ref, ly_ref, h_ref,
                res_ref, gate_ref, wmg_ref, bmg_ref, hgn_ref, mln_ref, wb_ref, wo_ref, o_ref,
                r0, rows, d_model):
    rs = slice(r0, r0 + rows)
    hgg = hgg_ref[rs, :].astype(F32)
    o_hg = hg0_ref[rs, :].astype(F32) + hg1_ref[rs, :].astype(F32)
    o_ml = ml0_ref[rs, :].astype(F32) + ml1_ref[rs, :].astype(F32)
    a_out = _head_rms(o_hg, hgn_ref[...]) * (hgg * _sigmoid_t(hgg))
    b_out = _head_rms(o_ml, mln_ref[...]) * _sigmoid_t(mlo_ref[rs, :].astype(F32))
    h_lru = lf_ref[rs, :].astype(F32) + lb_ref[rs, :].astype(F32)
    c_out = h_lru * _gelu_tanh(ly_ref[rs, :].astype(F32))
    h = h_ref[rs, :]
    yield
    merged = None
    for n, br in enumerate((a_out, b_out, c_out)):
        cs = slice(n * d_model, (n + 1) * d_model)
        gate = _sigmoid_t((_dot(h, wmg_ref[:, cs]) + bmg_ref[:, cs]).astype(BF16))
        term = gate * _dot(br.astype(BF16), wb_ref[n]).astype(BF16)
        merged = term if merged is None else merged + term
        yield
    o_ref[rs, :] = res_ref[rs, :] + gate_ref[...] * _dot(merged, wo_ref[...])
    yield


def _merge_body(*refs, rows, d_model):
    tm = refs[-1].shape[0]
    gens = [_merge_rows(*refs, r0, rows, d_model) for r0 in range(0, tm, rows)]
    n_phase = 5
    for step in range(n_phase + len(gens) - 1):
        for i, g in enumerate(gens):
            if 0 <= step - i < n_phase:
                next(g)


def _merge_call(l, is_ctx, tm, o_hg, o_ml, h_f, h_b, hg_a, ml_a, lru_y, h1, res, mods4,
                w_mg, b_mg, hg_norm, ml_norm, w_branch, w_out):
    bsz, n, d_model = res.shape
    t0 = 0 if is_ctx else 1
    rows = min(tm, MERGE_ROWS)

    def tok(col):
        return lambda b, t: (b, t + t0, col)

    mix = lambda d: pl.BlockSpec((None, None, tm, MIX_W), lambda b, t: (b, d, t + t0, 0))
    in_specs = [
        mix(0), mix(1), mix(0), mix(1),
        pl.BlockSpec((None, tm, MIX_W), tok(0)),
        pl.BlockSpec((None, tm, MIX_W), tok(0)),
        pl.BlockSpec((None, tm, MIX_W), tok(2)),
        pl.BlockSpec((None, tm, MIX_W), tok(2)),
        pl.BlockSpec((None, tm, MIX_W), tok(0)),
        pl.BlockSpec((None, tm, d_model), tok(0)),
        pl.BlockSpec((None, tm, d_model), lambda b, t: (b, t, 0)),
        pl.BlockSpec((None, None, 1, d_model), lambda b, t: (l, bsz if is_ctx else b, 0, 2)),
        _resident((None, d_model, 3 * d_model), lambda b, t: (l, 0, 0)),
        pl.BlockSpec((None, 1, 3 * d_model), lambda b, t: (l, 0, 0)),
        pl.BlockSpec((None, 1, MIX_W), lambda b, t: (l, 0, 0)),
        pl.BlockSpec((None, 1, MIX_W), lambda b, t: (l, 0, 0)),
        _resident((None, 3, MIX_W, d_model), lambda b, t: (l, 0, 0, 0)),
        _resident((None, d_model, d_model), lambda b, t: (l, 0, 0)),
    ]
    return pl.pallas_call(
        functools.partial(_merge_body, rows=rows, d_model=d_model),
        grid=(bsz, n // tm),
        in_specs=in_specs,
        out_specs=pl.BlockSpec((None, tm, d_model), lambda b, t: (b, t, 0)),
        out_shape=jax.ShapeDtypeStruct(res.shape, F32),
        compiler_params=_cparams(("arbitrary", "arbitrary")),
        name="merge",
    )(o_hg, o_hg, o_ml, o_ml, h_f, h_b, hg_a, ml_a, lru_y, h1, res, mods4,
      w_mg, b_mg, hg_norm, ml_norm, w_branch, w_out)


def _ffn_rows(x_ref, sh_ref, sc_ref, gt_ref, ln_ref, wi_ref, wo_ref, fn_ref, o_ref, r0, rows, *,
              n_inner, final_norm):
    x = x_ref[r0:r0 + rows, :]
    h = _norm_mod(x, ln_ref[...], sh_ref[...], sc_ref[...]).astype(BF16)
    yield
    gu = _dot(h, wi_ref[...])
    yield
    hidden = gu.shape[-1] // 2
    gate = gu[:, :hidden]
    act = (gate * _sigmoid_t(gate) * gu[:, hidden:]).astype(BF16)
    yield
    f = _dot(act, wo_ref[...])
    yield
    y = x + gt_ref[...] * f
    if final_norm:
        y = y * lax.rsqrt(jnp.mean(y * y, axis=-1, keepdims=True) + EPS) * fn_ref[...]
    if n_inner is None:
        o_ref[r0:r0 + rows, :] = y
    else:
        for a in range(rows // n_inner):
            o_ref[:, r0 // n_inner + a, :] = y[a * n_inner:(a + 1) * n_inner, :]
    yield


def _ffn_body(x_ref, sh_ref, sc_ref, gt_ref, ln_ref, wi_ref, wo_ref, fn_ref, o_ref, *, rows, n_inner,
              final_norm):
    gens = [_ffn_rows(x_ref, sh_ref, sc_ref, gt_ref, ln_ref, wi_ref, wo_ref, fn_ref, o_ref, r0, rows,
                      n_inner=n_inner, final_norm=final_norm)
            for r0 in range(0, x_ref.shape[0], rows)]
    skew = 2
    n_phase = 5
    for step in range(n_phase + skew * (len(gens) - 1)):
        for i, g in enumerate(gens):
            if 0 <= step - skew * i < n_phase:
                next(g)


def _ffn_call(l, x, mods4, ctx_row, ln2, wi, wo, fin, *, tm, n_inner, final_norm):
    bsz, n, d_model = x.shape
    hidden = wo.shape[1]
    nt = n // tm
    rows = min(tm, FFN_ROWS)

    def mod_spec(chunk):
        return pl.BlockSpec((None, None, 1, d_model),
                            lambda b, t: (l, bsz if ctx_row else b, 0, chunk))

    if n_inner is None:
        o_spec = pl.BlockSpec((None, tm, d_model), lambda b, t: (b, t, 0))
        o_shape = jax.ShapeDtypeStruct(x.shape, F32)
    else:
        assert rows % n_inner == 0
        n_a = tm // n_inner
        o_spec = pl.BlockSpec((None, n_inner, n_a, d_model), lambda b, t: (b, 0, t, 0))
        o_shape = jax.ShapeDtypeStruct((bsz, n_inner, n // n_inner, d_model), F32)
    out = pl.pallas_call(
        functools.partial(_ffn_body, rows=rows, n_inner=n_inner, final_norm=final_norm),
        grid=(bsz, nt),
        in_specs=[
            pl.BlockSpec((None, tm, d_model), lambda b, t: (b, t, 0)),
            mod_spec(3), mod_spec(4), mod_spec(5),
            pl.BlockSpec((None, 1, d_model), lambda b, t: (l, 0, 0)),
            _resident((None, d_model, 2 * hidden), lambda b, t: (l, 0, 0)),
            _resident((None, hidden, d_model), lambda b, t: (l, 0, 0)),
            pl.BlockSpec((1, d_model), lambda b, t: (0, 0)),
        ],
        out_specs=o_spec,
        out_shape=o_shape,
        compiler_params=_cparams(("arbitrary", "arbitrary")),
        name="ffn",
    )(x, mods4, mods4, mods4, ln2, wi, wo, fin)
    return out.reshape(bsz, n, d_model)


def _prep_in_proj(w_in, b_in):
    depth, d_model, _ = w_in.shape
    hq = N_HEADS * HG_DK
    sizes = (hq, MIX_W, MIX_W, 2 * hq, N_HEADS * ML_DQK, N_HEADS * ML_DQK, MIX_W, MIX_W,
             2 * N_HEADS, 2 * N_HEADS, MIX_W, MIX_W, 3 * d_model)
    offs = np.concatenate([[0], np.cumsum(sizes)])
    assert offs[-1] == w_in.shape[-1]

    def cols(a, lo, hi):
        return a[..., int(offs[lo]):int(offs[hi])]

    def gate_block(a):
        ig = cols(a, 8, 9).reshape(a.shape[:-1] + (2, N_HEADS))
        fg = cols(a, 9, 10).reshape(a.shape[:-1] + (2, N_HEADS))
        pad = jnp.zeros(a.shape[:-1] + (2, GATE_LANES - 2 * N_HEADS), a.dtype)
        return jnp.concatenate([ig, fg, pad], axis=-1).reshape(a.shape[:-1] + (2 * GATE_LANES,))

    groups = [
        (lambda a: cols(a, 0, 3), BF16),
        (lambda a: cols(a, 3, 4), F32),
        (lambda a: cols(a, 4, 8), BF16),
        (gate_block, F32),
        (lambda a: cols(a, 10, 11), F32),
        (lambda a: cols(a, 11, 12), BF16),
    ]
    merge_gates = lambda a: cols(a, 12, 13)
    kinds = ("hgrn2_qig", "hgrn2_forget", None, "mlstm_gates", "lru_x", None)
    b3 = b_in.reshape(depth, 1, -1)
    ws = [f(w_in).astype(BF16) for f, _ in groups]
    bs = [f(b3) for f, _ in groups]
    w_mg = merge_gates(w_in).astype(BF16)
    b_mg = merge_gates(b3)
    return ws, bs, [dt for _, dt in groups], kinds, w_mg, b_mg


def _prep_lru_gates(lru_gate_w, lru_gate_b):
    depth = lru_gate_w.shape[0]
    eye = jnp.eye(LRU_BLOCKS, dtype=lru_gate_w.dtype)
    dense = jnp.einsum("lzgnde,nm->lndzgme", lru_gate_w, eye)
    dense = dense.reshape(depth, MIX_W, 4 * MIX_W).astype(BF16)
    return dense, lru_gate_b.reshape(depth, 1, 4 * MIX_W)


def kernel(x, c, ctx, c_ctx, w_ada, b_ada, ln1, w_in, b_in, hg_lb_raw, hg_norm, ml_norm, conv_w, conv_b,
           lru_gate_w, lru_gate_b, lru_lambda, w_branch, w_out, ln2, w_ffn_in, w_ffn_out, final_norm):
    bsz, seq, d_model = x.shape
    lc = ctx.shape[1]
    depth = w_ada.shape[0]
    rows = seq // GRID_W
    assert lc % CHUNK == 0 and seq % lc == 0 and seq % FFN_TM == 0 and bsz < 16

    c_all = jnp.zeros((16, d_model), F32).at[:bsz].set(c).at[bsz].set(c_ctx)
    mods4 = _ada_call(c_all, w_ada, b_ada).reshape(depth, 16, 1, 6 * d_model)
    lb_all = _lb_call(hg_lb_raw)
    lb3 = lb_all.reshape(depth, 1, 2 * MIX_W)
    lb4 = lb_all.reshape(depth, 2, 1, MIX_W)

    ws, bs, out_dtypes, kinds, w_mg, b_mg = _prep_in_proj(w_in, b_in)
    wgate, bgate = _prep_lru_gates(lru_gate_w, lru_gate_b)
    w_ffn_in_b = w_ffn_in.astype(BF16)
    w_ffn_out_b = w_ffn_out.astype(BF16)
    w_branch_b = w_branch.astype(BF16)
    w_out_b = w_out.astype(BF16)
    ln1_3 = ln1.reshape(depth, 1, d_model)
    ln2_3 = ln2.reshape(depth, 1, d_model)
    hgn = hg_norm.reshape(depth, 1, MIX_W)
    mln = ml_norm.reshape(depth, 1, MIX_W)
    conv_b3 = conv_b.reshape(depth, 1, MIX_W)
    fin = final_norm.reshape(1, d_model)

    for l in range(depth):
        last = l == depth - 1
        n_inner = GRID_W if l % 2 == 0 else rows
        hg_a, hg_f, ml_a, ml_g, lru_y, h1, a0, u0, a1, u1 = _in_proj_call(
            l, ctx, x, mods4, ln1_3, lb3, ws, bs, out_dtypes, kinds, conv_w, conv_b3, wgate, bgate, lru_lambda)
        o_hg, o_ml = _mixers_call(l, hg_a, hg_f, lb4, ml_a, ml_g, lc)
        h_f, h_b = _lru_scan_call(a0, u0, a1, u1, lc)
        branches = (o_hg, o_ml, h_f, h_b, hg_a, ml_a, lru_y, h1)
        merge_w = (w_mg, b_mg, hgn, mln, w_branch_b, w_out_b)
        x_m = _merge_call(l, False, lc, *branches, x, mods4, *merge_w)
        x = _ffn_call(l, x_m, mods4, False, ln2_3, w_ffn_in_b, w_ffn_out_b, fin,
                      tm=FFN_TM, n_inner=n_inner, final_norm=last)
        if not last:
            ctx_m = _merge_call(l, True, lc, *branches, ctx, mods4, *merge_w)
            ctx = _ffn_call(l, ctx_m, mods4, True, ln2_3, w_ffn_in_b, w_ffn_out_b, fin,
                            tm=lc, n_inner=None, final_norm=False)
    return x
```

```python
import functools
import itertools

import numpy as np
import jax
import jax.numpy as jnp
from jax import lax
from jax.experimental import pallas as pl
from jax.experimental.pallas import tpu as pltpu

F32 = jnp.float32
BF16 = jnp.bfloat16

GRID_W = 64
MIX_W = 512
N_HEADS = 4
HEAD_DV = 128
HG_DK = 128
ML_DQK = 64
LRU_BLOCKS = 8
LRU_BD = 64
LRU_C = 8.0
EPS = 1e-6
NEG_BIG = -1e30
LB_TINY = 1e-30
LOG2_E = 1.4426950408889634
GATE_LANES = 128
CHUNK = 128
HG_GROUP = 1
MIX_NB = 4
MIX_SKEW = 14
FFN_TM = 512
FFN_ROWS = 256
MERGE_ROWS = 128
IN_PROJ_ROWS = 128
LRU_PITCH_PAD = 8
VMEM_LIMIT = 56 * 1024 * 1024


def _cparams(sem):
    return pltpu.CompilerParams(dimension_semantics=sem, vmem_limit_bytes=VMEM_LIMIT)


def _resident(shape, index_map):
    return pl.BlockSpec(shape, index_map, pipeline_mode=pl.Buffered(1))


def _sigmoid(x):
    return 1.0 / (1.0 + jnp.exp(-x))


def _sigmoid_t(x):
    return 0.5 * jnp.tanh(0.5 * x) + 0.5


def _log_sigmoid(x):
    return jnp.minimum(x, 0.0) - jnp.log(1.0 + jnp.exp(-jnp.abs(x)))


def _dot(a, b):
    return jnp.dot(a, b, preferred_element_type=F32)


def _dot_nt(a, b):
    return lax.dot_general(a, b, (((1,), (1,)), ((), ())), preferred_element_type=F32)


def _dot_tn(a, b):
    return lax.dot_general(a, b, (((0,), (0,)), ((), ())), preferred_element_type=F32)


def _split_hi_lo(x):
    hi = x.astype(BF16)
    lo = (x - hi.astype(F32)).astype(BF16)
    return hi, lo


def _norm_mod(x, ln, shift, scale):
    gain = ln * (1.0 + scale)
    return (x * lax.rsqrt(jnp.mean(x * x, axis=-1, keepdims=True) + EPS)) * gain + shift


def _mirror(m):
    return m[..., ::-1, ::-1].copy()


@functools.lru_cache(maxsize=None)
def _hgrn2_consts(C):
    n_lv = int(np.log2(C))
    mats, masks = [], []
    r = np.arange(C)
    for lv in range(n_lv):
        s = C >> (lv + 1)
        base = (r // (2 * s)) * (2 * s)
        mid = base + s - 1
        odd = r >= base + s
        m = np.zeros((C, C), np.float32)
        for t in range(C):
            if odd[t]:
                m[t, mid[t] + 1:t + 1] = 1.0
            else:
                m[t, t + 1:mid[t] + 1] = 1.0
        mats.append(m)
        same = base[:, None] == base[None, :]
        masks.append((same & odd[:, None] & (~odd)[None, :]).astype(np.float32))
    mats.append(np.tril(np.ones((C, C), np.float32)))
    masks.append(np.eye(C, dtype=np.float32))
    mats = np.stack(mats)
    masks = np.stack(masks)
    mats = np.stack([mats, _mirror(mats)])
    masks = np.stack([masks, _mirror(masks)])
    mst = mats.reshape(2, (n_lv + 1) * C, C)
    mst = np.concatenate([mst, mst], axis=-1)
    return mst, masks, n_lv


@functools.lru_cache(maxsize=None)
def _mlstm_consts(C):
    tri = np.tril(np.ones((C, C), np.float32))
    tri = np.stack([tri, _mirror(tri)])
    tri_cat = np.concatenate([tri, tri], axis=-1)
    tri_t = np.transpose(tri, (0, 2, 1))
    tri_t_cat = np.concatenate([tri_t, tri_t], axis=1)
    return tri_cat, tri_t_cat, tri


def _chunk_index(d, j, nctx, nch):
    bw = jnp.where(j < nctx, nctx - 1 - j, nch - 1 + nctx - j)
    return jnp.where(d == 0, j, bw)


def _ada_body(c_ref, w_ref, b_ref, o_ref):
    cc = c_ref[...]
    s = cc * _sigmoid(cc)
    o_ref[...] = jnp.dot(s, w_ref[...], preferred_element_type=F32,
                         precision=lax.Precision.HIGHEST) + b_ref[...]


def _ada_call(c_all, w_ada, b_ada):
    depth, d_model, n6 = w_ada.shape
    rows = c_all.shape[0]
    tn = 1536
    return pl.pallas_call(
        _ada_body,
        grid=(depth, n6 // tn),
        in_specs=[
            pl.BlockSpec((rows, d_model), lambda l, n: (0, 0)),
            pl.BlockSpec((None, d_model, tn), lambda l, n: (l, 0, n)),
            pl.BlockSpec((None, 1, tn), lambda l, n: (l, 0, n)),
        ],
        out_specs=pl.BlockSpec((None, rows, tn), lambda l, n: (l, 0, n)),
        out_shape=jax.ShapeDtypeStruct((depth, rows, n6), F32),
        compiler_params=_cparams(("arbitrary", "arbitrary")),
        name="ada_mod",
    )(c_all, w_ada, b_ada.reshape(depth, 1, n6))


def _lb_body(raw_ref, o_ref):
    raw = raw_ref[...]
    depth = raw.shape[0]
    e = jnp.exp(raw - jnp.max(raw, axis=0, keepdims=True))
    p = e / jnp.sum(e, axis=0, keepdims=True)
    acc = jnp.zeros_like(p[0:1])
    for l in range(depth):
        acc = acc + p[l:l + 1]
        o_ref[l:l + 1, :] = acc - p[0:1]


def _lb_call(hg_lb_raw):
    depth = hg_lb_raw.shape[0]
    raw = hg_lb_raw.reshape(depth, -1)
    return pl.pallas_call(
        _lb_body,
        out_shape=jax.ShapeDtypeStruct(raw.shape, F32),
        name="hgrn2_lower_bounds",
    )(raw)


def _log2_forget(f_pre, lb):
    half = 0.5 * (1.0 - lb)
    floor = jnp.maximum(lb, LB_TINY)
    return jnp.log2(jnp.maximum((floor + half) + half * jnp.tanh(0.5 * f_pre), floor))


def _lru_decay_input(r_pre, i_pre, lam, xc):
    z = -lam
    softplus = jnp.maximum(z, 0.0) + jnp.log(1.0 + jnp.exp(-jnp.abs(z)))
    log_a = (-LRU_C * softplus) * _sigmoid_t(r_pre)
    a = jnp.exp(log_a)
    one_m_a2 = -jnp.tanh(log_a) * (a * a + 1.0)
    return a, jnp.sqrt(jnp.maximum(one_m_a2, 0.0)) * (_sigmoid_t(i_pre) * xc)


def _in_proj_body(ctx_ref, x_ref, xp_ref, xn_ref, sh_ref, sc_ref, ln_ref, lb_ref,
                  cw_ref, cb_ref, wg_ref, bg_ref, lam_ref, *rest, kinds):
    n_grp = len(kinds)
    w_refs = rest[:n_grp]
    b_refs = rest[n_grp:2 * n_grp]
    outs = list(rest[2 * n_grp:])
    ext_ref = outs.pop()
    lru_refs = [outs.pop() for _ in range(4)][::-1]
    h_ref = outs.pop()
    o_refs = iter(outs)
    o_refs = [None if kind == "lru_x" else next(o_refs) for kind in kinds]
    t = pl.program_id(1)
    tm = x_ref.shape[0]
    rows = min(tm, IN_PROJ_ROWS)
    lru_i = kinds.index("lru_x")

    def norm(x):
        return _norm_mod(x, ln_ref[...], sh_ref[...], sc_ref[...]).astype(BF16)

    def finish(kind, y, o_ref, r0):
        rs = slice(r0, r0 + rows)
        if kind == "hgrn2_qig":
            q = y[:, :MIX_W]
            o_ref[rs, :MIX_W] = (q * _sigmoid_t(q)).astype(o_ref.dtype)
            o_ref[rs, MIX_W:] = y[:, MIX_W:].astype(o_ref.dtype)
        elif kind == "hgrn2_forget":
            o_ref[rs, :] = _log2_forget(y, lb_ref[...]).astype(o_ref.dtype)
        elif kind == "mlstm_gates":
            lane = lax.broadcasted_iota(jnp.int32, y.shape, 1) % GATE_LANES
            is_f = jnp.logical_and(lane >= N_HEADS, lane < 2 * N_HEADS)
            o_ref[rs, :] = (jnp.where(is_f, _log_sigmoid(y), y) * LOG2_E).astype(o_ref.dtype)
        elif kind == "lru_x":
            ext_ref[8 + r0:8 + r0 + rows, :] = y
        else:
            o_ref[rs, :] = y.astype(o_ref.dtype)

    def row_group(r0):
        rs = slice(r0, r0 + rows)
        h = norm(jnp.where(t == 0, ctx_ref[rs, :], x_ref[rs, :]))
        h_ref[rs, :] = h
        yield
        rest_order = sorted((i for i in range(n_grp) if i != lru_i), key=lambda i: kinds[i] is None)
        for i in [lru_i] + rest_order:
            finish(kinds[i], _dot(h, w_refs[i][...]) + b_refs[i][...], o_refs[i], r0)
            yield

    def lru_gates(wait):
        prev_ok = t >= 2
        next_ok = jnp.logical_and(t >= 1, t < pl.num_programs(1) - 1)
        w_lx, b_lx = w_refs[lru_i][...], b_refs[lru_i][...]
        ext_ref[0:8, :] = jnp.where(prev_ok, _dot(norm(xp_ref[...]), w_lx) + b_lx, 0.0)
        ext_ref[8 + tm:16 + tm, :] = jnp.where(next_ok, _dot(norm(xn_ref[...]), w_lx) + b_lx, 0.0)
        for _ in range(wait):
            yield
        cw = cw_ref[...]
        xc = (cw[0:1] * ext_ref[6:6 + tm, :] + cw[1:2] * ext_ref[7:7 + tm, :] + cw[2:3] * ext_ref[8:8 + tm, :]
              + cw[3:4] * ext_ref[9:9 + tm, :]) + cb_ref[...]
        yield
        xcb = xc.astype(BF16)
        gates = [_dot(xcb, wg_ref[n]) + bg_ref[:, n * MIX_W:(n + 1) * MIX_W] for n in range(4)]
        yield
        for d in range(2):
            for r0 in range(0, tm, rows):
                rs = slice(r0, r0 + rows)
                a, u = _lru_decay_input(gates[2 * d][rs, :], gates[2 * d + 1][rs, :],
                                        lam_ref[d:d + 1, :], xc[rs, :])
                lru_refs[2 * d][rs, :] = a
                lru_refs[2 * d + 1][rs, :] = u
                yield

    groups = [row_group(r0) for r0 in range(0, tm, rows)]
    gens = groups + [lru_gates(wait=len(groups) + 1)]
    live = list(gens)
    step = 0
    while live:
        for i, g in enumerate(gens):
            if g in live and (step >= i or g is gens[-1]):
                try:
                    next(g)
                except StopIteration:
                    live.remove(g)
        step += 1


def _in_proj_call(l, ctx, x, mods4, ln1, lb3, ws, bs, out_dtypes, kinds, conv_w, conv_b, wgate, bgate, lam):
    bsz, lc, d_model = ctx.shape
    seq = x.shape[1]
    tm = lc
    nt = 1 + seq // tm
    h8 = tm // 8

    def mod_spec(chunk):
        return pl.BlockSpec((None, None, 1, d_model),
                            lambda b, t: (l, jnp.where(t == 0, bsz, b), 0, chunk))

    def per_layer(shape):
        return pl.BlockSpec((None,) + shape, lambda b, t: (l,) + (0,) * len(shape))

    in_specs = [
        pl.BlockSpec((None, tm, d_model), lambda b, t: (b, 0, 0)),
        pl.BlockSpec((None, tm, d_model), lambda b, t: (b, jnp.maximum(t - 1, 0), 0)),
        pl.BlockSpec((None, 8, d_model), lambda b, t: (b, jnp.maximum((t - 1) * h8 - 1, 0), 0)),
        pl.BlockSpec((None, 8, d_model), lambda b, t: (b, jnp.minimum(jnp.maximum(t, 1) * h8, seq // 8 - 1), 0)),
        mod_spec(0), mod_spec(1),
        per_layer((1, d_model)),
        per_layer((1, lb3.shape[-1])),
        per_layer((4, MIX_W)), per_layer((1, MIX_W)),
        per_layer((4, MIX_W, MIX_W)), per_layer((1, 4 * MIX_W)), per_layer((2, MIX_W)),
    ]
    in_specs += [per_layer((d_model, w.shape[-1])) for w in ws]
    in_specs += [per_layer((1, w.shape[-1])) for w in ws]
    tok = lambda width: pl.BlockSpec((None, tm, width), lambda b, t: (b, t, 0))
    out_specs, out_shape = [], []
    for w, dt, kind in zip(ws, out_dtypes, kinds):
        if kind != "lru_x":
            out_specs.append(tok(w.shape[-1]))
            out_shape.append(jax.ShapeDtypeStruct((bsz, lc + seq, w.shape[-1]), dt))
    out_specs.append(tok(d_model))
    out_shape.append(jax.ShapeDtypeStruct((bsz, lc + seq, d_model), BF16))
    out_specs += [tok(MIX_W)] * 4
    out_shape += [jax.ShapeDtypeStruct((bsz, lc + seq, MIX_W), F32)] * 4
    return pl.pallas_call(
        functools.partial(_in_proj_body, kinds=kinds),
        grid=(bsz, nt),
        in_specs=in_specs,
        out_specs=out_specs,
        out_shape=out_shape,
        scratch_shapes=[pltpu.VMEM((tm + 16, MIX_W), F32)],
        compiler_params=_cparams(("arbitrary", "arbitrary")),
        name="in_proj",
    )(ctx, x, x, x, mods4, mods4, ln1, lb3, conv_w, conv_b, wgate, bgate, lam, *ws, *bs)


def _hgrn2_body(qv_ref, g_ref, lb_ref, mst_ref, msk_ref, o_ref, st_ref, *, C, n_lv):
    q16 = qv_ref[:, :MIX_W]
    v = qv_ref[:, MIX_W:]
    q = q16.astype(F32)
    g = g_ref[...]
    lb = lb_ref[...]
    kk = (1.0 - jnp.exp2(g)) + (jnp.maximum(lb, LB_TINY) - lb)
    g_hi, g_lo = _split_hi_lo(g)
    ex = _dot(mst_ref[...], jnp.concatenate([g_hi, g_lo], axis=0))
    tot = jnp.sum(g, axis=0, keepdims=True)
    k16 = kk.astype(BF16)

    for h0 in range(0, N_HEADS, HG_GROUP):
        heads = range(h0, h0 + HG_GROUP)
        sls = {h: slice(h * HG_DK, (h + 1) * HG_DK) for h in heads}
        p = {h: msk_ref[n_lv] * _dot_nt(q16[:, sls[h]], k16[:, sls[h]]) for h in heads}
        for lv in range(n_lv):
            for h in heads:
                w = jnp.exp2(ex[lv * C:(lv + 1) * C, sls[h]]).astype(BF16)
                p[h] = p[h] + msk_ref[lv] * _dot_nt(q16[:, sls[h]] * w, k16[:, sls[h]] * w)
            if lv % 2 == 1:
                yield
        for h in heads:
            sl = sls[h]
            st = st_ref[h]
            b_in = ex[n_lv * C:(n_lv + 1) * C, sl]
            qb = (q[:, sl] * jnp.exp2(b_in)).astype(BF16)
            vh = v[:, sl]
            o_ref[:, sl] = (_dot(p[h].astype(BF16), vh) + _dot_nt(qb, st.astype(BF16))).astype(o_ref.dtype)
            tot_h = tot[:, sl]
            kb = (kk[:, sl] * jnp.exp2(tot_h - b_in)).astype(BF16)
            st_ref[h] = st * jnp.exp2(tot_h) + _dot_tn(vh, kb)
            yield


def _mlstm_body(qkv_ref, g_ref, tri_ref, trit_ref, msk_ref, o_ref, c_ref, m_ref, *, C):
    qk_w = N_HEADS * ML_DQK
    gates = g_ref[...]
    gates_t = gates.T
    lf_hi, lf_lo = _split_hi_lo(gates)
    b_cols = _dot(tri_ref[...], jnp.concatenate([lf_hi, lf_lo], axis=0))
    lft_hi, lft_lo = _split_hi_lo(gates_t)
    b_rows = _dot(jnp.concatenate([lft_hi, lft_lo], axis=1), trit_ref[...])
    tot = jnp.sum(gates, axis=0, keepdims=True)
    allowed = msk_ref[...] > 0.0
    ones = jnp.ones((C, HEAD_DV), BF16)
    log2_scale = 0.5 * np.log2(ML_DQK)

    for h in range(N_HEADS):
        b_col = b_cols[:, N_HEADS + h:N_HEADS + h + 1]
        c_col = gates[:, h:h + 1] - b_col
        c_row = gates_t[h:h + 1, :] - b_rows[N_HEADS + h:N_HEADS + h + 1, :]
        m_prev = m_ref[h][0:1, 0:1]
        cm = jnp.where(allowed, c_row, NEG_BIG)
        m_t = jnp.maximum(jnp.max(cm, axis=-1, keepdims=True), m_prev)
        yield
        qh = qkv_ref[:, h * ML_DQK:(h + 1) * ML_DQK]
        kh = qkv_ref[:, qk_w + h * ML_DQK:qk_w + (h + 1) * ML_DQK]
        vh = qkv_ref[:, 2 * qk_w + h * HEAD_DV:2 * qk_w + (h + 1) * HEAD_DV]
        vaug = jnp.concatenate([vh, ones], axis=1)
        s = _dot_nt(qh, kh) * jnp.exp2(cm - (m_t + log2_scale))
        qs = (qh.astype(F32) * jnp.exp2(m_prev - m_t - log2_scale)).astype(BF16)
        yield
        c_aug = c_ref[h]
        r = _dot(jnp.concatenate([s.astype(BF16), qs], axis=1),
                 jnp.concatenate([vaug, c_aug.astype(BF16)], axis=0))
        num = r[:, :HEAD_DV]
        den = r[:, HEAD_DV:]
        hout = num / jnp.maximum(jnp.abs(den), jnp.exp2(-(b_col + m_t)))
        o_ref[:, h * HEAD_DV:(h + 1) * HEAD_DV] = hout.astype(o_ref.dtype)
        yield
        b_last = tot[:, N_HEADS + h:N_HEADS + h + 1]
        log_w = b_last + c_col
        m_new = jnp.maximum(b_last + m_prev, jnp.max(log_w, axis=0, keepdims=True))
        kw = (kh.astype(F32) * jnp.exp2(log_w - m_new)).astype(BF16)
        c_ref[h] = jnp.exp2(b_last + m_prev - m_new) * c_aug + _dot_tn(kw, vaug)
        m_ref[h] = jnp.broadcast_to(m_new, m_ref.shape[1:])
        yield


def _mixers_body(hqv_ref, hg_ref, lb_ref, mst_ref, hmsk_ref, mqkv_ref, mg_ref,
                 tri_ref, trit_ref, mmsk_ref, o_hg_ref, o_ml_ref, st_ref, c_ref, m_ref, *, C, n_lv):
    @pl.when(pl.program_id(2) == 0)
    def _():
        st_ref[...] = jnp.zeros_like(st_ref)
        c_ref[...] = jnp.zeros_like(c_ref)
        m_ref[...] = jnp.zeros_like(m_ref)

    def element(bb):
        gens = [_hgrn2_body(hqv_ref.at[bb], hg_ref.at[bb], lb_ref, mst_ref, hmsk_ref,
                            o_hg_ref.at[bb], st_ref.at[bb], C=C, n_lv=n_lv),
                _mlstm_body(mqkv_ref.at[bb], mg_ref.at[bb], tri_ref, trit_ref,
                            mmsk_ref, o_ml_ref.at[bb], c_ref.at[bb], m_ref.at[bb], C=C)]
        for _ in itertools.zip_longest(*gens):
            yield

    streams = [element(bb) for bb in range(hqv_ref.shape[0])]
    live = []
    while streams or live:
        if streams and (not live or live[-1][1] >= MIX_SKEW):
            live.append([streams.pop(0), 0])
        for item in list(live):
            try:
                next(item[0])
                item[1] += 1
            except StopIteration:
                live.remove(item)


def _mixers_call(l, hg_a, hg_f, lb4, ml_a, ml_g, lc):
    bsz, tt, _ = hg_a.shape
    C = CHUNK
    nch, nctx = tt // C, lc // C
    mst, hmsk, n_lv = _hgrn2_consts(C)
    mst = jnp.asarray(mst, BF16)
    hmsk = jnp.asarray(hmsk, F32)
    tri_cat, tri_t_cat, tri = _mlstm_consts(C)
    tri_cat = jnp.asarray(tri_cat, BF16)
    tri_t_cat = jnp.asarray(tri_t_cat, BF16)
    mmsk = jnp.asarray(tri, F32)
    cidx = functools.partial(_chunk_index, nctx=nctx, nch=nch)
    qk_w = N_HEADS * ML_DQK
    tok = lambda col: (lambda b, d, j: (b, cidx(d, j), col))
    tok_d = lambda b, d, j: (b, cidx(d, j), d)
    per_dir3 = lambda b, d, j: (d, 0, 0)
    nb = max(n for n in range(1, MIX_NB + 1) if bsz % n == 0)
    o_spec = pl.BlockSpec((nb, None, C, MIX_W), lambda b, d, j: (b, d, cidx(d, j), 0))
    o_shape = jax.ShapeDtypeStruct((bsz, 2, tt, MIX_W), BF16)
    return pl.pallas_call(
        functools.partial(_mixers_body, C=C, n_lv=n_lv),
        grid=(bsz // nb, 2, nch),
        in_specs=[
            pl.BlockSpec((nb, C, 2 * MIX_W), tok(0)),
            pl.BlockSpec((nb, C, MIX_W), tok_d),
            pl.BlockSpec((None, None, 1, MIX_W), lambda b, d, j: (l, d, 0, 0)),
            pl.BlockSpec((None,) + mst.shape[1:], per_dir3),
            pl.BlockSpec((None,) + hmsk.shape[1:], lambda b, d, j: (d, 0, 0, 0)),
            pl.BlockSpec((nb, C, 2 * qk_w + MIX_W), tok(0)),
            pl.BlockSpec((nb, C, GATE_LANES), tok_d),
            pl.BlockSpec((None, C, 2 * C), per_dir3),
            pl.BlockSpec((None, 2 * C, C), per_dir3),
            pl.BlockSpec((None, C, C), per_dir3),
        ],
        out_specs=[o_spec, o_spec],
        out_shape=[o_shape, o_shape],
        scratch_shapes=[pltpu.VMEM((nb, N_HEADS, HEAD_DV, HG_DK), F32),
                        pltpu.VMEM((nb, N_HEADS, ML_DQK, 2 * HEAD_DV), F32),
                        pltpu.VMEM((nb, N_HEADS, 8, 128), F32)],
        compiler_params=_cparams(("arbitrary", "arbitrary", "arbitrary")),
        name="mixers",
    )(hg_a, hg_f, lb4, mst, hmsk, ml_a, ml_g, tri_cat, tri_t_cat, mmsk)


def _lru_scan_body(a0_ref, u0_ref, a1_ref, u1_ref, hf_ref, hb_ref, sf_ref, sb_ref,
                   pa0_ref, pu0_ref, pa1_ref, pu1_ref, *, tc):
    @pl.when(pl.program_id(0) == 0)
    def _():
        sf_ref[...] = jnp.zeros_like(sf_ref)
        sb_ref[...] = jnp.zeros_like(sb_ref)

    for src, dst in ((a0_ref, pa0_ref), (u0_ref, pu0_ref), (a1_ref, pa1_ref), (u1_ref, pu1_ref)):
        dst[:, 0:tc, :] = src[...]

    def step(i, carry):
        hf, hb = carry
        hf = pa0_ref[:, i, :] * hf + pu0_ref[:, i, :]
        pu0_ref[:, i, :] = hf
        ib = tc - 1 - i
        hb = pa1_ref[:, ib, :] * hb + pu1_ref[:, ib, :]
        pu1_ref[:, ib, :] = hb
        return hf, hb

    hf, hb = lax.fori_loop(0, tc, step, (sf_ref[...], sb_ref[...]), unroll=8)
    sf_ref[...] = hf
    sb_ref[...] = hb
    hf_ref[...] = pu0_ref[:, 0:tc, :].astype(hf_ref.dtype)
    hb_ref[...] = pu1_ref[:, 0:tc, :].astype(hb_ref.dtype)


def _lru_scan_call(a0, u0, a1, u1, lc):
    bsz, tt, w = a0.shape
    tc = CHUNK
    nch, nctx = tt // tc, lc // tc
    fw = pl.BlockSpec((bsz, tc, w), lambda j: (0, j, 0))
    bw = pl.BlockSpec((bsz, tc, w), lambda j: (0, _chunk_index(1, j, nctx, nch), 0))
    out = jax.ShapeDtypeStruct((bsz, tt, w), BF16)
    return pl.pallas_call(
        functools.partial(_lru_scan_body, tc=tc),
        grid=(nch,),
        in_specs=[fw, fw, bw, bw],
        out_specs=[fw, bw],
        out_shape=[out, out],
        scratch_shapes=[pltpu.VMEM((bsz, w), F32), pltpu.VMEM((bsz, w), F32)]
        + [pltpu.VMEM((bsz, tc + LRU_PITCH_PAD, w), F32)] * 4,
        compiler_params=_cparams(("arbitrary",)),
        name="rglru_scan",
    )(a0, u0, a1, u1)


def _head_rms(o, w):
    parts = []
    for h in range(N_HEADS):
        oh = o[:, h * HEAD_DV:(h + 1) * HEAD_DV]
        parts.append(oh * lax.rsqrt(jnp.mean(oh * oh, axis=-1, keepdims=True) + EPS))
    return jnp.concatenate(parts, axis=1) * w


def _gelu_tanh(x):
    return 0.5 * x * (1.0 + jnp.tanh(0.7978845608028654 * (x + 0.044715 * (x * x * x))))


def _merge_rows(hg0_ref, hg1_ref, ml0_ref, ml1_ref, lf_ref, lb_ref, hgg_ref, mlo_ref, ly_ref, h_ref,
                res_ref, gate_ref, wmg_ref, bmg_ref, hgn_ref, mln_ref, wb_ref, wo_ref, o_ref,
                r0, rows, d_model):
    rs = slice(r0, r0 + rows)
    hgg = hgg_ref[rs, :].astype(F32)
    o_hg = hg0_ref[rs, :].astype(F32) + hg1_ref[rs, :].astype(F32)
    o_ml = ml0_ref[rs, :].astype(F32) + ml1_ref[rs, :].astype(F32)
    a_out = _head_rms(o_hg, hgn_ref[...]) * (hgg * _sigmoid_t(hgg))
    b_out = _head_rms(o_ml, mln_ref[...]) * _sigmoid_t(mlo_ref[rs, :].astype(F32))
    h_lru = lf_ref[rs, :].astype(F32) + lb_ref[rs, :].astype(F32)
    c_out = h_lru * _gelu_tanh(ly_ref[rs, :].astype(F32))
    h = h_ref[rs, :]
    yield
    merged = None
    for n, br in enumerate((a_out, b_out, c_out)):
        cs = slice(n * d_model, (n + 1) * d_model)
        gate = _sigmoid_t((_dot(h, wmg_ref[:, cs]) + bmg_ref[:, cs]).astype(BF16))
        term = gate * _dot(br.astype(BF16), wb_ref[n]).astype(BF16)
        merged = term if merged is None else merged + term
        yield
    o_ref[rs, :] = res_ref[rs, :] + gate_ref[...] * _dot(merged, wo_ref[...])
    yield


def _merge_body(*refs, rows, d_model):
    tm = refs[-1].shape[0]
    gens = [_merge_rows(*refs, r0, rows, d_model) for r0 in range(0, tm, rows)]
    n_phase = 5
    for step in range(n_phase + len(gens) - 1):
        for i, g in enumerate(gens):
            if 0 <= step - i < n_phase:
                next(g)


def _merge_call(l, is_ctx, tm, o_hg, o_ml, h_f, h_b, hg_a, ml_a, lru_y, h1, res, mods4,
                w_mg, b_mg, hg_norm, ml_norm, w_branch, w_out):
    bsz, n, d_model = res.shape
    t0 = 0 if is_ctx else 1
    rows = min(tm, MERGE_ROWS)

    def tok(col):
        return lambda b, t: (b, t + t0, col)

    mix = lambda d: pl.BlockSpec((None, None, tm, MIX_W), lambda b, t: (b, d, t + t0, 0))
    in_specs = [
        mix(0), mix(1), mix(0), mix(1),
        pl.BlockSpec((None, tm, MIX_W), tok(0)),
        pl.BlockSpec((None, tm, MIX_W), tok(0)),
        pl.BlockSpec((None, tm, MIX_W), tok(2)),
        pl.BlockSpec((None, tm, MIX_W), tok(2)),
        pl.BlockSpec((None, tm, MIX_W), tok(0)),
        pl.BlockSpec((None, tm, d_model), tok(0)),
        pl.BlockSpec((None, tm, d_model), lambda b, t: (b, t, 0)),
        pl.BlockSpec((None, None, 1, d_model), lambda b, t: (l, bsz if is_ctx else b, 0, 2)),
        _resident((None, d_model, 3 * d_model), lambda b, t: (l, 0, 0)),
        pl.BlockSpec((None, 1, 3 * d_model), lambda b, t: (l, 0, 0)),
        pl.BlockSpec((None, 1, MIX_W), lambda b, t: (l, 0, 0)),
        pl.BlockSpec((None, 1, MIX_W), lambda b, t: (l, 0, 0)),
        _resident((None, 3, MIX_W, d_model), lambda b, t: (l, 0, 0, 0)),
        _resident((None, d_model, d_model), lambda b, t: (l, 0, 0)),
    ]
    return pl.pallas_call(
        functools.partial(_merge_body, rows=rows, d_model=d_model),
        grid=(bsz, n // tm),
        in_specs=in_specs,
        out_specs=pl.BlockSpec((None, tm, d_model), lambda b, t: (b, t, 0)),
        out_shape=jax.ShapeDtypeStruct(res.shape, F32),
        compiler_params=_cparams(("arbitrary", "arbitrary")),
        name="merge",
    )(o_hg, o_hg, o_ml, o_ml, h_f, h_b, hg_a, ml_a, lru_y, h1, res, mods4,
      w_mg, b_mg, hg_norm, ml_norm, w_branch, w_out)


def _ffn_rows(x_ref, sh_ref, sc_ref, gt_ref, ln_ref, wi_ref, wo_ref, fn_ref, o_ref, r0, rows, *,
              n_inner, final_norm):
    x = x_ref[r0:r0 + rows, :]
    h = _norm_mod(x, ln_ref[...], sh_ref[...], sc_ref[...]).astype(BF16)
    yield
    gu = _dot(h, wi_ref[...])
    yield
    hidden = gu.shape[-1] // 2
    gate = gu[:, :hidden]
    act = (gate * _sigmoid_t(gate) * gu[:, hidden:]).astype(BF16)
    yield
    f = _dot(act, wo_ref[...])
    yield
    y = x + gt_ref[...] * f
    if final_norm:
        y = y * lax.rsqrt(jnp.mean(y * y, axis=-1, keepdims=True) + EPS) * fn_ref[...]
    if n_inner is None:
        o_ref[r0:r0 + rows, :] = y
    else:
        for a in range(rows // n_inner):
            o_ref[:, r0 // n_inner + a, :] = y[a * n_inner:(a + 1) * n_inner, :]
    yield


def _ffn_body(x_ref, sh_ref, sc_ref, gt_ref, ln_ref, wi_ref, wo_ref, fn_ref, o_ref, *, rows, n_inner,
              final_norm):
    gens = [_ffn_rows(x_ref, sh_ref, sc_ref, gt_ref, ln_ref, wi_ref, wo_ref, fn_ref, o_ref, r0, rows,
                      n_inner=n_inner, final_norm=final_norm)
            for r0 in range(0, x_ref.shape[0], rows)]
    skew = 2
    n_phase = 5
    for step in range(n_phase + skew * (len(gens) - 1)):
        for i, g in enumerate(gens):
            if 0 <= step - skew * i < n_phase:
                next(g)


def _ffn_call(l, x, mods4, ctx_row, ln2, wi, wo, fin, *, tm, n_inner, final_norm):
    bsz, n, d_model = x.shape
    hidden = wo.shape[1]
    nt = n // tm
    rows = min(tm, FFN_ROWS)

    def mod_spec(chunk):
        return pl.BlockSpec((None, None, 1, d_model),
                            lambda b, t: (l, bsz if ctx_row else b, 0, chunk))

    if n_inner is None:
        o_spec = pl.BlockSpec((None, tm, d_model), lambda b, t: (b, t, 0))
        o_shape = jax.ShapeDtypeStruct(x.shape, F32)
    else:
        assert rows % n_inner == 0
        n_a = tm // n_inner
        o_spec = pl.BlockSpec((None, n_inner, n_a, d_model), lambda b, t: (b, 0, t, 0))
        o_shape = jax.ShapeDtypeStruct((bsz, n_inner, n // n_inner, d_model), F32)
    out = pl.pallas_call(
        functools.partial(_ffn_body, rows=rows, n_inner=n_inner, final_norm=final_norm),
        grid=(bsz, nt),
        in_specs=[
            pl.BlockSpec((None, tm, d_model), lambda b, t: (b, t, 0)),
            mod_spec(3), mod_spec(4), mod_spec(5),
            pl.BlockSpec((None, 1, d_model), lambda b, t: (l, 0, 0)),
            _resident((None, d_model, 2 * hidden), lambda b, t: (l, 0, 0)),
            _resident((None, hidden, d_model), lambda b, t: (l, 0, 0)),
            pl.BlockSpec((1, d_model), lambda b, t: (0, 0)),
        ],
        out_specs=o_spec,
        out_shape=o_shape,
        compiler_params=_cparams(("arbitrary", "arbitrary")),
        name="ffn",
    )(x, mods4, mods4, mods4, ln2, wi, wo, fin)
    return out.reshape(bsz, n, d_model)


def _prep_in_proj(w_in, b_in):
    depth, d_model, _ = w_in.shape
    hq = N_HEADS * HG_DK
    sizes = (hq, MIX_W, MIX_W, 2 * hq, N_HEADS * ML_DQK, N_HEADS * ML_DQK, MIX_W, MIX_W,
             2 * N_HEADS, 2 * N_HEADS, MIX_W, MIX_W, 3 * d_model)
    offs = np.concatenate([[0], np.cumsum(sizes)])
    assert offs[-1] == w_in.shape[-1]

    def cols(a, lo, hi):
        return a[..., int(offs[lo]):int(offs[hi])]

    def gate_block(a):
        ig = cols(a, 8, 9).reshape(a.shape[:-1] + (2, N_HEADS))
        fg = cols(a, 9, 10).reshape(a.shape[:-1] + (2, N_HEADS))
        pad = jnp.zeros(a.shape[:-1] + (2, GATE_LANES - 2 * N_HEADS), a.dtype)
        return jnp.concatenate([ig, fg, pad], axis=-1).reshape(a.shape[:-1] + (2 * GATE_LANES,))

    groups = [
        (lambda a: cols(a, 0, 3), BF16),
        (lambda a: cols(a, 3, 4), F32),
        (lambda a: cols(a, 4, 8), BF16),
        (gate_block, F32),
        (lambda a: cols(a, 10, 11), F32),
        (lambda a: cols(a, 11, 12), BF16),
    ]
    merge_gates = lambda a: cols(a, 12, 13)
    kinds = ("hgrn2_qig", "hgrn2_forget", None, "mlstm_gates", "lru_x", None)
    b3 = b_in.reshape(depth, 1, -1)
    ws = [f(w_in).astype(BF16) for f, _ in groups]
    bs = [f(b3) for f, _ in groups]
    w_mg = merge_gates(w_in).astype(BF16)
    b_mg = merge_gates(b3)
    return ws, bs, [dt for _, dt in groups], kinds, w_mg, b_mg


def _prep_lru_gates(lru_gate_w, lru_gate_b):
    depth = lru_gate_w.shape[0]
    eye = jnp.eye(LRU_BLOCKS, dtype=lru_gate_w.dtype)
    dense = lru_gate_w[:, :, :, :, :, None, :] * eye[:, None, :, None]
    dense = dense.reshape(depth, 4, MIX_W, MIX_W).astype(BF16)
    return dense, lru_gate_b.reshape(depth, 1, 4 * MIX_W)


def kernel(x, c, ctx, c_ctx, w_ada, b_ada, ln1, w_in, b_in, hg_lb_raw, hg_norm, ml_norm, conv_w, conv_b,
           lru_gate_w, lru_gate_b, lru_lambda, w_branch, w_out, ln2, w_ffn_in, w_ffn_out, final_norm):
    bsz, seq, d_model = x.shape
    lc = ctx.shape[1]
    depth = w_ada.shape[0]
    rows = seq // GRID_W
    assert lc % CHUNK == 0 and seq % lc == 0 and seq % FFN_TM == 0 and bsz < 16

    c_all = jnp.zeros((16, d_model), F32).at[:bsz].set(c).at[bsz].set(c_ctx)
    mods4 = _ada_call(c_all, w_ada, b_ada).reshape(depth, 16, 1, 6 * d_model)
    lb_all = _lb_call(hg_lb_raw)
    lb3 = lb_all.reshape(depth, 1, 2 * MIX_W)
    lb4 = lb_all.reshape(depth, 2, 1, MIX_W)

    ws, bs, out_dtypes, kinds, w_mg, b_mg = _prep_in_proj(w_in, b_in)
    wgate, bgate = _prep_lru_gates(lru_gate_w, lru_gate_b)
    w_ffn_in_b = w_ffn_in.astype(BF16)
    w_ffn_out_b = w_ffn_out.astype(BF16)
    w_branch_b = w_branch.astype(BF16)
    w_out_b = w_out.astype(BF16)
    ln1_3 = ln1.reshape(depth, 1, d_model)
    ln2_3 = ln2.reshape(depth, 1, d_model)
    hgn = hg_norm.reshape(depth, 1, MIX_W)
    mln = ml_norm.reshape(depth, 1, MIX_W)
    conv_b3 = conv_b.reshape(depth, 1, MIX_W)
    fin = final_norm.reshape(1, d_model)

    for l in range(depth):
        last = l == depth - 1
        n_inner = GRID_W if l % 2 == 0 else rows
        hg_a, hg_f, ml_a, ml_g, lru_y, h1, a0, u0, a1, u1 = _in_proj_call(
            l, ctx, x, mods4, ln1_3, lb3, ws, bs, out_dtypes, kinds, conv_w, conv_b3, wgate, bgate, lru_lambda)
        o_hg, o_ml = _mixers_call(l, hg_a, hg_f, lb4, ml_a, ml_g, lc)
        h_f, h_b = _lru_scan_call(a0, u0, a1, u1, lc)
        branches = (o_hg, o_ml, h_f, h_b, hg_a, ml_a, lru_y, h1)
        merge_w = (w_mg, b_mg, hgn, mln, w_branch_b, w_out_b)
        x_m = _merge_call(l, False, lc, *branches, x, mods4, *merge_w)
        x = _ffn_call(l, x_m, mods4, False, ln2_3, w_ffn_in_b, w_ffn_out_b, fin,
                      tm=FFN_TM, n_inner=n_inner, final_norm=last)
        if not last:
            ctx_m = _merge_call(l, True, lc, *branches, ctx, mods4, *merge_w)
            ctx = _ffn_call(l, ctx_m, mods4, True, ln2_3, w_ffn_in_b, w_ffn_out_b, fin,
                            tm=lc, n_inner=None, final_norm=False)
    return x
```

```python
import functools
import itertools

import numpy as np
import jax
import jax.numpy as jnp
from jax import lax
from jax.experimental import pallas as pl
from jax.experimental.pallas import tpu as pltpu

F32 = jnp.float32
BF16 = jnp.bfloat16

GRID_W = 64
MIX_W = 512
N_HEADS = 4
HEAD_DV = 128
HG_DK = 128
ML_DQK = 64
LRU_BLOCKS = 8
LRU_BD = 64
LRU_C = 8.0
EPS = 1e-6
NEG_BIG = -1e30
LB_TINY = 1e-30
LOG2_E = 1.4426950408889634
GATE_LANES = 128
CHUNK = 128
HG_GROUP = 1
MIX_NB = 4
MIX_SKEW = 14
FFN_TM = 512
FFN_ROWS = 256
MERGE_ROWS = 128
IN_PROJ_ROWS = 128
LRU_PITCH_PAD = 8
VMEM_LIMIT = 56 * 1024 * 1024


def _cparams(sem):
    return pltpu.CompilerParams(dimension_semantics=sem, vmem_limit_bytes=VMEM_LIMIT)


def _resident(shape, index_map):
    return pl.BlockSpec(shape, index_map, pipeline_mode=pl.Buffered(1))


def _sigmoid(x):
    return 1.0 / (1.0 + jnp.exp(-x))


def _sigmoid_t(x):
    return 0.5 * jnp.tanh(0.5 * x) + 0.5


def _log_sigmoid(x):
    return jnp.minimum(x, 0.0) - jnp.log(1.0 + jnp.exp(-jnp.abs(x)))


def _dot(a, b):
    return jnp.dot(a, b, preferred_element_type=F32)


def _dot_nt(a, b):
    return lax.dot_general(a, b, (((1,), (1,)), ((), ())), preferred_element_type=F32)


def _dot_tn(a, b):
    return lax.dot_general(a, b, (((0,), (0,)), ((), ())), preferred_element_type=F32)


def _split_hi_lo(x):
    hi = x.astype(BF16)
    lo = (x - hi.astype(F32)).astype(BF16)
    return hi, lo


def _norm_mod(x, ln, shift, scale):
    gain = ln * (1.0 + scale)
    return (x * lax.rsqrt(jnp.mean(x * x, axis=-1, keepdims=True) + EPS)) * gain + shift


def _mirror(m):
    return m[..., ::-1, ::-1].copy()


@functools.lru_cache(maxsize=None)
def _hgrn2_consts(C):
    n_lv = int(np.log2(C))
    mats, masks = [], []
    r = np.arange(C)
    for lv in range(n_lv):
        s = C >> (lv + 1)
        base = (r // (2 * s)) * (2 * s)
        mid = base + s - 1
        odd = r >= base + s
        m = np.zeros((C, C), np.float32)
        for t in range(C):
            if odd[t]:
                m[t, mid[t] + 1:t + 1] = 1.0
            else:
                m[t, t + 1:mid[t] + 1] = 1.0
        mats.append(m)
        same = base[:, None] == base[None, :]
        masks.append((same & odd[:, None] & (~odd)[None, :]).astype(np.float32))
    mats.append(np.tril(np.ones((C, C), np.float32)))
    masks.append(np.eye(C, dtype=np.float32))
    mats = np.stack(mats)
    masks = np.stack(masks)
    mats = np.stack([mats, _mirror(mats)])
    masks = np.stack([masks, _mirror(masks)])
    mst = mats.reshape(2, (n_lv + 1) * C, C)
    mst = np.concatenate([mst, mst], axis=-1)
    return mst, masks, n_lv


@functools.lru_cache(maxsize=None)
def _mlstm_consts(C):
    tri = np.tril(np.ones((C, C), np.float32))
    tri = np.stack([tri, _mirror(tri)])
    tri_cat = np.concatenate([tri, tri], axis=-1)
    tri_t = np.transpose(tri, (0, 2, 1))
    tri_t_cat = np.concatenate([tri_t, tri_t], axis=1)
    return tri_cat, tri_t_cat, tri


def _chunk_index(d, j, nctx, nch):
    bw = jnp.where(j < nctx, nctx - 1 - j, nch - 1 + nctx - j)
    return jnp.where(d == 0, j, bw)


def _ada_body(c_ref, w_ref, b_ref, o_ref):
    cc = c_ref[...]
    s = cc * _sigmoid(cc)
    o_ref[...] = jnp.dot(s, w_ref[...], preferred_element_type=F32,
                         precision=lax.Precision.HIGHEST) + b_ref[...]


def _ada_call(c_all, w_ada, b_ada):
    depth, d_model, n6 = w_ada.shape
    rows = c_all.shape[0]
    tn = 1536
    return pl.pallas_call(
        _ada_body,
        grid=(depth, n6 // tn),
        in_specs=[
            pl.BlockSpec((rows, d_model), lambda l, n: (0, 0)),
            pl.BlockSpec((None, d_model, tn), lambda l, n: (l, 0, n)),
            pl.BlockSpec((None, 1, tn), lambda l, n: (l, 0, n)),
        ],
        out_specs=pl.BlockSpec((None, rows, tn), lambda l, n: (l, 0, n)),
        out_shape=jax.ShapeDtypeStruct((depth, rows, n6), F32),
        compiler_params=_cparams(("arbitrary", "arbitrary")),
        name="ada_mod",
    )(c_all, w_ada, b_ada.reshape(depth, 1, n6))


def _lb_body(raw_ref, o_ref):
    raw = raw_ref[...]
    depth = raw.shape[0]
    e = jnp.exp(raw - jnp.max(raw, axis=0, keepdims=True))
    p = e / jnp.sum(e, axis=0, keepdims=True)
    acc = jnp.zeros_like(p[0:1])
    for l in range(depth):
        acc = acc + p[l:l + 1]
        o_ref[l:l + 1, :] = acc - p[0:1]


def _lb_call(hg_lb_raw):
    depth = hg_lb_raw.shape[0]
    raw = hg_lb_raw.reshape(depth, -1)
    return pl.pallas_call(
        _lb_body,
        out_shape=jax.ShapeDtypeStruct(raw.shape, F32),
        name="hgrn2_lower_bounds",
    )(raw)


def _log2_forget(f_pre, lb):
    half = 0.5 * (1.0 - lb)
    floor = jnp.maximum(lb, LB_TINY)
    return jnp.log2(jnp.maximum((floor + half) + half * jnp.tanh(0.5 * f_pre), floor))


def _lru_decay_input(r_pre, i_pre, lam, xc):
    z = -lam
    softplus = jnp.maximum(z, 0.0) + jnp.log(1.0 + jnp.exp(-jnp.abs(z)))
    log_a = (-LRU_C * softplus) * _sigmoid_t(r_pre)
    a = jnp.exp(log_a)
    one_m_a = -jnp.tanh(0.5 * log_a) * (a + 1.0)
    one_m_a2 = one_m_a * (a + 1.0)
    return one_m_a, jnp.sqrt(jnp.maximum(one_m_a2, 0.0)) * (_sigmoid_t(i_pre) * xc)


def _in_proj_body(ctx_ref, x_ref, xp_ref, xn_ref, sh_ref, sc_ref, ln_ref, lb_ref,
                  cw_ref, cb_ref, wg_ref, bg_ref, lam_ref, *rest, kinds):
    n_grp = len(kinds)
    w_refs = rest[:n_grp]
    b_refs = rest[n_grp:2 * n_grp]
    outs = list(rest[2 * n_grp:])
    ext_ref = outs.pop()
    lru_refs = [outs.pop() for _ in range(4)][::-1]
    h_ref = outs.pop()
    o_refs = iter(outs)
    o_refs = [None if kind == "lru_x" else next(o_refs) for kind in kinds]
    t = pl.program_id(1)
    tm = x_ref.shape[0]
    rows = min(tm, IN_PROJ_ROWS)
    lru_i = kinds.index("lru_x")

    def norm(x):
        return _norm_mod(x, ln_ref[...], sh_ref[...], sc_ref[...]).astype(BF16)

    def finish(kind, y, o_ref, r0):
        rs = slice(r0, r0 + rows)
        if kind == "hgrn2_qig":
            q = y[:, :MIX_W]
            o_ref[rs, :MIX_W] = (q * _sigmoid_t(q)).astype(o_ref.dtype)
            o_ref[rs, MIX_W:] = y[:, MIX_W:].astype(o_ref.dtype)
        elif kind == "hgrn2_forget":
            o_ref[rs, :] = _log2_forget(y, lb_ref[...]).astype(o_ref.dtype)
        elif kind == "mlstm_gates":
            lane = lax.broadcasted_iota(jnp.int32, y.shape, 1) % GATE_LANES
            is_f = jnp.logical_and(lane >= N_HEADS, lane < 2 * N_HEADS)
            o_ref[rs, :] = (jnp.where(is_f, _log_sigmoid(y), y) * LOG2_E).astype(o_ref.dtype)
        elif kind == "lru_x":
            ext_ref[8 + r0:8 + r0 + rows, :] = y
        else:
            o_ref[rs, :] = y.astype(o_ref.dtype)

    def row_group(r0):
        rs = slice(r0, r0 + rows)
        h = norm(jnp.where(t == 0, ctx_ref[rs, :], x_ref[rs, :]))
        h_ref[rs, :] = h
        yield
        rest_order = sorted((i for i in range(n_grp) if i != lru_i), key=lambda i: kinds[i] is None)
        for i in [lru_i] + rest_order:
            finish(kinds[i], _dot(h, w_refs[i][...]) + b_refs[i][...], o_refs[i], r0)
            yield

    def lru_gates(wait):
        prev_ok = t >= 2
        next_ok = jnp.logical_and(t >= 1, t < pl.num_programs(1) - 1)
        w_lx, b_lx = w_refs[lru_i][...], b_refs[lru_i][...]
        ext_ref[0:8, :] = jnp.where(prev_ok, _dot(norm(xp_ref[...]), w_lx) + b_lx, 0.0)
        ext_ref[8 + tm:16 + tm, :] = jnp.where(next_ok, _dot(norm(xn_ref[...]), w_lx) + b_lx, 0.0)
        for _ in range(wait):
            yield
        cw = cw_ref[...]
        xc = (cw[0:1] * ext_ref[6:6 + tm, :] + cw[1:2] * ext_ref[7:7 + tm, :] + cw[2:3] * ext_ref[8:8 + tm, :]
              + cw[3:4] * ext_ref[9:9 + tm, :]) + cb_ref[...]
        yield
        xcb = xc.astype(BF16)
        gates = [_dot(xcb, wg_ref[n]) + bg_ref[:, n * MIX_W:(n + 1) * MIX_W] for n in range(4)]
        yield
        for d in range(2):
            for r0 in range(0, tm, rows):
                rs = slice(r0, r0 + rows)
                one_m_a, u = _lru_decay_input(gates[2 * d][rs, :], gates[2 * d + 1][rs, :],
                                              lam_ref[d:d + 1, :], xc[rs, :])
                lru_refs[2 * d][rs, :] = one_m_a.astype(lru_refs[2 * d].dtype)
                lru_refs[2 * d + 1][rs, :] = u.astype(lru_refs[2 * d + 1].dtype)
                yield

    groups = [row_group(r0) for r0 in range(0, tm, rows)]
    gens = groups + [lru_gates(wait=len(groups) + 1)]
    live = list(gens)
    step = 0
    while live:
        for i, g in enumerate(gens):
            if g in live and (step >= i or g is gens[-1]):
                try:
                    next(g)
                except StopIteration:
                    live.remove(g)
        step += 1


def _in_proj_call(l, ctx, x, mods4, ln1, lb3, ws, bs, out_dtypes, kinds, conv_w, conv_b, wgate, bgate, lam):
    bsz, lc, d_model = ctx.shape
    seq = x.shape[1]
    tm = lc
    nt = 1 + seq // tm
    h8 = tm // 8

    def mod_spec(chunk):
        return pl.BlockSpec((None, None, 1, d_model),
                            lambda b, t: (l, jnp.where(t == 0, bsz, b), 0, chunk))

    def per_layer(shape):
        return pl.BlockSpec((None,) + shape, lambda b, t: (l,) + (0,) * len(shape))

    in_specs = [
        pl.BlockSpec((None, tm, d_model), lambda b, t: (b, 0, 0)),
        pl.BlockSpec((None, tm, d_model), lambda b, t: (b, jnp.maximum(t - 1, 0), 0)),
        pl.BlockSpec((None, 8, d_model), lambda b, t: (b, jnp.maximum((t - 1) * h8 - 1, 0), 0)),
        pl.BlockSpec((None, 8, d_model), lambda b, t: (b, jnp.minimum(jnp.maximum(t, 1) * h8, seq // 8 - 1), 0)),
        mod_spec(0), mod_spec(1),
        per_layer((1, d_model)),
        per_layer((1, lb3.shape[-1])),
        per_layer((4, MIX_W)), per_layer((1, MIX_W)),
        per_layer((4, MIX_W, MIX_W)), per_layer((1, 4 * MIX_W)), per_layer((2, MIX_W)),
    ]
    in_specs += [per_layer((d_model, w.shape[-1])) for w in ws]
    in_specs += [per_layer((1, w.shape[-1])) for w in ws]
    tok = lambda width: pl.BlockSpec((None, tm, width), lambda b, t: (b, t, 0))
    out_specs, out_shape = [], []
    for w, dt, kind in zip(ws, out_dtypes, kinds):
        if kind != "lru_x":
            out_specs.append(tok(w.shape[-1]))
            out_shape.append(jax.ShapeDtypeStruct((bsz, lc + seq, w.shape[-1]), dt))
    out_specs.append(tok(d_model))
    out_shape.append(jax.ShapeDtypeStruct((bsz, lc + seq, d_model), BF16))
    out_specs += [tok(MIX_W)] * 4
    out_shape += [jax.ShapeDtypeStruct((bsz, lc + seq, MIX_W), BF16)] * 4
    return pl.pallas_call(
        functools.partial(_in_proj_body, kinds=kinds),
        grid=(bsz, nt),
        in_specs=in_specs,
        out_specs=out_specs,
        out_shape=out_shape,
        scratch_shapes=[pltpu.VMEM((tm + 16, MIX_W), F32)],
        compiler_params=_cparams(("arbitrary", "arbitrary")),
        name="in_proj",
    )(ctx, x, x, x, mods4, mods4, ln1, lb3, conv_w, conv_b, wgate, bgate, lam, *ws, *bs)


def _hgrn2_body(qv_ref, g_ref, lb_ref, mst_ref, msk_ref, o_ref, st_ref, *, C, n_lv):
    q16 = qv_ref[:, :MIX_W]
    v = qv_ref[:, MIX_W:]
    q = q16.astype(F32)
    g = g_ref[...]
    lb = lb_ref[...]
    kk = (1.0 - jnp.exp2(g)) + (jnp.maximum(lb, LB_TINY) - lb)
    g_hi, g_lo = _split_hi_lo(g)
    ex = _dot(mst_ref[...], jnp.concatenate([g_hi, g_lo], axis=0))
    tot = jnp.sum(g, axis=0, keepdims=True)
    k16 = kk.astype(BF16)

    for h0 in range(0, N_HEADS, HG_GROUP):
        heads = range(h0, h0 + HG_GROUP)
        sls = {h: slice(h * HG_DK, (h + 1) * HG_DK) for h in heads}
        p = {h: msk_ref[n_lv] * _dot_nt(q16[:, sls[h]], k16[:, sls[h]]) for h in heads}
        for lv in range(n_lv):
            for h in heads:
                w = jnp.exp2(ex[lv * C:(lv + 1) * C, sls[h]]).astype(BF16)
                p[h] = p[h] + msk_ref[lv] * _dot_nt(q16[:, sls[h]] * w, k16[:, sls[h]] * w)
            if lv % 2 == 1:
                yield
        for h in heads:
            sl = sls[h]
            st = st_ref[h]
            b_in = ex[n_lv * C:(n_lv + 1) * C, sl]
            qb = (q[:, sl] * jnp.exp2(b_in)).astype(BF16)
            vh = v[:, sl]
            o_ref[:, sl] = (_dot(p[h].astype(BF16), vh) + _dot_nt(qb, st.astype(BF16))).astype(o_ref.dtype)
            tot_h = tot[:, sl]
            kb = (kk[:, sl] * jnp.exp2(tot_h - b_in)).astype(BF16)
            st_ref[h] = st * jnp.exp2(tot_h) + _dot_tn(vh, kb)
            yield


def _mlstm_body(qkv_ref, g_ref, tri_ref, trit_ref, msk_ref, o_ref, c_ref, m_ref, *, C):
    qk_w = N_HEADS * ML_DQK
    gates = g_ref[...]
    gates_t = gates.T
    lf_hi, lf_lo = _split_hi_lo(gates)
    b_cols = _dot(tri_ref[...], jnp.concatenate([lf_hi, lf_lo], axis=0))
    lft_hi, lft_lo = _split_hi_lo(gates_t)
    b_rows = _dot(jnp.concatenate([lft_hi, lft_lo], axis=1), trit_ref[...])
    tot = jnp.sum(gates, axis=0, keepdims=True)
    allowed = msk_ref[...] > 0.0
    ones = jnp.ones((C, HEAD_DV), BF16)
    log2_scale = 0.5 * np.log2(ML_DQK)

    for h in range(N_HEADS):
        b_col = b_cols[:, N_HEADS + h:N_HEADS + h + 1]
        c_col = gates[:, h:h + 1] - b_col
        c_row = gates_t[h:h + 1, :] - b_rows[N_HEADS + h:N_HEADS + h + 1, :]
        m_prev = m_ref[h][0:1, 0:1]
        cm = jnp.where(allowed, c_row, NEG_BIG)
        m_t = jnp.maximum(jnp.max(cm, axis=-1, keepdims=True), m_prev)
        yield
        qh = qkv_ref[:, h * ML_DQK:(h + 1) * ML_DQK]
        kh = qkv_ref[:, qk_w + h * ML_DQK:qk_w + (h + 1) * ML_DQK]
        vh = qkv_ref[:, 2 * qk_w + h * HEAD_DV:2 * qk_w + (h + 1) * HEAD_DV]
        vaug = jnp.concatenate([vh, ones], axis=1)
        s = _dot_nt(qh, kh) * jnp.exp2(cm - (m_t + log2_scale))
        qs = (qh.astype(F32) * jnp.exp2(m_prev - m_t - log2_scale)).astype(BF16)
        yield
        c_aug = c_ref[h]
        r = _dot(jnp.concatenate([s.astype(BF16), qs], axis=1),
                 jnp.concatenate([vaug, c_aug.astype(BF16)], axis=0))
        num = r[:, :HEAD_DV]
        den = r[:, HEAD_DV:]
        hout = num / jnp.maximum(jnp.abs(den), jnp.exp2(-(b_col + m_t)))
        o_ref[:, h * HEAD_DV:(h + 1) * HEAD_DV] = hout.astype(o_ref.dtype)
        yield
        b_last = tot[:, N_HEADS + h:N_HEADS + h + 1]
        log_w = b_last + c_col
        m_new = jnp.maximum(b_last + m_prev, jnp.max(log_w, axis=0, keepdims=True))
        kw = (kh.astype(F32) * jnp.exp2(log_w - m_new)).astype(BF16)
        c_ref[h] = jnp.exp2(b_last + m_prev - m_new) * c_aug + _dot_tn(kw, vaug)
        m_ref[h] = jnp.broadcast_to(m_new, m_ref.shape[1:])
        yield


def _mixers_body(hqv_ref, hg_ref, lb_ref, mst_ref, hmsk_ref, mqkv_ref, mg_ref,
                 tri_ref, trit_ref, mmsk_ref, o_hg_ref, o_ml_ref, st_ref, c_ref, m_ref, *, C, n_lv):
    @pl.when(pl.program_id(2) == 0)
    def _():
        st_ref[...] = jnp.zeros_like(st_ref)
        c_ref[...] = jnp.zeros_like(c_ref)
        m_ref[...] = jnp.zeros_like(m_ref)

    def element(bb):
        gens = [_hgrn2_body(hqv_ref.at[bb], hg_ref.at[bb], lb_ref, mst_ref, hmsk_ref,
                            o_hg_ref.at[bb], st_ref.at[bb], C=C, n_lv=n_lv),
                _mlstm_body(mqkv_ref.at[bb], mg_ref.at[bb], tri_ref, trit_ref,
                            mmsk_ref, o_ml_ref.at[bb], c_ref.at[bb], m_ref.at[bb], C=C)]
        for _ in itertools.zip_longest(*gens):
            yield

    streams = [element(bb) for bb in range(hqv_ref.shape[0])]
    live = []
    while streams or live:
        if streams and (not live or live[-1][1] >= MIX_SKEW):
            live.append([streams.pop(0), 0])
        for item in list(live):
            try:
                next(item[0])
                item[1] += 1
            except StopIteration:
                live.remove(item)


def _mixers_call(l, hg_a, hg_f, lb4, ml_a, ml_g, lc):
    bsz, tt, _ = hg_a.shape
    C = CHUNK
    nch, nctx = tt // C, lc // C
    mst, hmsk, n_lv = _hgrn2_consts(C)
    mst = jnp.asarray(mst, BF16)
    hmsk = jnp.asarray(hmsk, F32)
    tri_cat, tri_t_cat, tri = _mlstm_consts(C)
    tri_cat = jnp.asarray(tri_cat, BF16)
    tri_t_cat = jnp.asarray(tri_t_cat, BF16)
    mmsk = jnp.asarray(tri, F32)
    cidx = functools.partial(_chunk_index, nctx=nctx, nch=nch)
    qk_w = N_HEADS * ML_DQK
    tok = lambda col: (lambda b, d, j: (b, cidx(d, j), col))
    tok_d = lambda b, d, j: (b, cidx(d, j), d)
    per_dir3 = lambda b, d, j: (d, 0, 0)
    nb = max(n for n in range(1, MIX_NB + 1) if bsz % n == 0)
    o_spec = pl.BlockSpec((nb, None, C, MIX_W), lambda b, d, j: (b, d, cidx(d, j), 0))
    o_shape = jax.ShapeDtypeStruct((bsz, 2, tt, MIX_W), BF16)
    return pl.pallas_call(
        functools.partial(_mixers_body, C=C, n_lv=n_lv),
        grid=(bsz // nb, 2, nch),
        in_specs=[
            pl.BlockSpec((nb, C, 2 * MIX_W), tok(0)),
            pl.BlockSpec((nb, C, MIX_W), tok_d),
            pl.BlockSpec((None, None, 1, MIX_W), lambda b, d, j: (l, d, 0, 0)),
            pl.BlockSpec((None,) + mst.shape[1:], per_dir3),
            pl.BlockSpec((None,) + hmsk.shape[1:], lambda b, d, j: (d, 0, 0, 0)),
            pl.BlockSpec((nb, C, 2 * qk_w + MIX_W), tok(0)),
            pl.BlockSpec((nb, C, GATE_LANES), tok_d),
            pl.BlockSpec((None, C, 2 * C), per_dir3),
            pl.BlockSpec((None, 2 * C, C), per_dir3),
            pl.BlockSpec((None, C, C), per_dir3),
        ],
        out_specs=[o_spec, o_spec],
        out_shape=[o_shape, o_shape],
        scratch_shapes=[pltpu.VMEM((nb, N_HEADS, HEAD_DV, HG_DK), F32),
                        pltpu.VMEM((nb, N_HEADS, ML_DQK, 2 * HEAD_DV), F32),
                        pltpu.VMEM((nb, N_HEADS, 8, 128), F32)],
        compiler_params=_cparams(("arbitrary", "arbitrary", "arbitrary")),
        name="mixers",
    )(hg_a, hg_f, lb4, mst, hmsk, ml_a, ml_g, tri_cat, tri_t_cat, mmsk)


def _lru_scan_body(a0_ref, u0_ref, a1_ref, u1_ref, hf_ref, hb_ref, sf_ref, sb_ref,
                   pa0_ref, pu0_ref, pa1_ref, pu1_ref, *, tc):
    @pl.when(pl.program_id(0) == 0)
    def _():
        sf_ref[...] = jnp.zeros_like(sf_ref)
        sb_ref[...] = jnp.zeros_like(sb_ref)

    for src, dst in ((u0_ref, pu0_ref), (u1_ref, pu1_ref)):
        dst[:, 0:tc, :] = src[...].astype(F32)
    for src, dst in ((a0_ref, pa0_ref), (a1_ref, pa1_ref)):
        dst[:, 0:tc, :] = 1.0 - src[...].astype(F32)

    def step(i, carry):
        hf, hb = carry
        hf = pa0_ref[:, i, :] * hf + pu0_ref[:, i, :]
        pu0_ref[:, i, :] = hf
        ib = tc - 1 - i
        hb = pa1_ref[:, ib, :] * hb + pu1_ref[:, ib, :]
        pu1_ref[:, ib, :] = hb
        return hf, hb

    hf, hb = lax.fori_loop(0, tc, step, (sf_ref[...], sb_ref[...]), unroll=8)
    sf_ref[...] = hf
    sb_ref[...] = hb
    hf_ref[...] = pu0_ref[:, 0:tc, :].astype(hf_ref.dtype)
    hb_ref[...] = pu1_ref[:, 0:tc, :].astype(hb_ref.dtype)


def _lru_scan_call(a0, u0, a1, u1, lc):
    bsz, tt, w = a0.shape
    tc = CHUNK
    nch, nctx = tt // tc, lc // tc
    fw = pl.BlockSpec((bsz, tc, w), lambda j: (0, j, 0))
    bw = pl.BlockSpec((bsz, tc, w), lambda j: (0, _chunk_index(1, j, nctx, nch), 0))
    out = jax.ShapeDtypeStruct((bsz, tt, w), BF16)
    return pl.pallas_call(
        functools.partial(_lru_scan_body, tc=tc),
        grid=(nch,),
        in_specs=[fw, fw, bw, bw],
        out_specs=[fw, bw],
        out_shape=[out, out],
        scratch_shapes=[pltpu.VMEM((bsz, w), F32), pltpu.VMEM((bsz, w), F32)]
        + [pltpu.VMEM((bsz, tc + LRU_PITCH_PAD, w), F32)] * 4,
        compiler_params=_cparams(("arbitrary",)),
        name="rglru_scan",
    )(a0, u0, a1, u1)


def _head_rms(o, w):
    parts = []
    for h in range(N_HEADS):
        oh = o[:, h * HEAD_DV:(h + 1) * HEAD_DV]
        parts.append(oh * lax.rsqrt(jnp.mean(oh * oh, axis=-1, keepdims=True) + EPS))
    return jnp.concatenate(parts, axis=1) * w


def _gelu_tanh(x):
    return 0.5 * x * (1.0 + jnp.tanh(0.7978845608028654 * (x + 0.044715 * (x * x * x))))


def _merge_rows(hg0_ref, hg1_ref, ml0_ref, ml1_ref, lf_ref, lb_ref, hgg_ref, mlo_ref, ly_ref, h_ref,
                res_ref, gate_ref, wmg_ref, bmg_ref, hgn_ref, mln_ref, wb_ref, wo_ref, o_ref,
                r0, rows, d_model):
    rs = slice(r0, r0 + rows)
    hgg = hgg_ref[rs, :].astype(F32)
    o_hg = hg0_ref[rs, :].astype(F32) + hg1_ref[rs, :].astype(F32)
    o_ml = ml0_ref[rs, :].astype(F32) + ml1_ref[rs, :].astype(F32)
    a_out = _head_rms(o_hg, hgn_ref[...]) * (hgg * _sigmoid_t(hgg))
    b_out = _head_rms(o_ml, mln_ref[...]) * _sigmoid_t(mlo_ref[rs, :].astype(F32))
    h_lru = lf_ref[rs, :].astype(F32) + lb_ref[rs, :].astype(F32)
    c_out = h_lru * _gelu_tanh(ly_ref[rs, :].astype(F32))
    h = h_ref[rs, :]
    yield
    merged = None
    for n, br in enumerate((a_out, b_out, c_out)):
        cs = slice(n * d_model, (n + 1) * d_model)
        gate = _sigmoid_t((_dot(h, wmg_ref[:, cs]) + bmg_ref[:, cs]).astype(BF16))
        term = gate * _dot(br.astype(BF16), wb_ref[n]).astype(BF16)
        merged = term if merged is None else merged + term
        yield
    o_ref[rs, :] = res_ref[rs, :] + gate_ref[...] * _dot(merged, wo_ref[...])
    yield


def _merge_body(*refs, rows, d_model):
    tm = refs[-1].shape[0]
    gens = [_merge_rows(*refs, r0, rows, d_model) for r0 in range(0, tm, rows)]
    n_phase = 5
    for step in range(n_phase + len(gens) - 1):
        for i, g in enumerate(gens):
            if 0 <= step - i < n_phase:
                next(g)


def _merge_call(l, is_ctx, tm, o_hg, o_ml, h_f, h_b, hg_a, ml_a, lru_y, h1, res, mods4,
                w_mg, b_mg, hg_norm, ml_norm, w_branch, w_out):
    bsz, n, d_model = res.shape
    t0 = 0 if is_ctx else 1
    rows = min(tm, MERGE_ROWS)

    def tok(col):
        return lambda b, t: (b, t + t0, col)

    mix = lambda d: pl.BlockSpec((None, None, tm, MIX_W), lambda b, t: (b, d, t + t0, 0))
    in_specs = [
        mix(0), mix(1), mix(0), mix(1),
        pl.BlockSpec((None, tm, MIX_W), tok(0)),
        pl.BlockSpec((None, tm, MIX_W), tok(0)),
        pl.BlockSpec((None, tm, MIX_W), tok(2)),
        pl.BlockSpec((None, tm, MIX_W), tok(2)),
        pl.BlockSpec((None, tm, MIX_W), tok(0)),
        pl.BlockSpec((None, tm, d_model), tok(0)),
        pl.BlockSpec((None, tm, d_model), lambda b, t: (b, t, 0)),
        pl.BlockSpec((None, None, 1, d_model), lambda b, t: (l, bsz if is_ctx else b, 0, 2)),
        _resident((None, d_model, 3 * d_model), lambda b, t: (l, 0, 0)),
        pl.BlockSpec((None, 1, 3 * d_model), lambda b, t: (l, 0, 0)),
        pl.BlockSpec((None, 1, MIX_W), lambda b, t: (l, 0, 0)),
        pl.BlockSpec((None, 1, MIX_W), lambda b, t: (l, 0, 0)),
        _resident((None, 3, MIX_W, d_model), lambda b, t: (l, 0, 0, 0)),
        _resident((None, d_model, d_model), lambda b, t: (l, 0, 0)),
    ]
    return pl.pallas_call(
        functools.partial(_merge_body, rows=rows, d_model=d_model),
        grid=(bsz, n // tm),
        in_specs=in_specs,
        out_specs=pl.BlockSpec((None, tm, d_model), lambda b, t: (b, t, 0)),
        out_shape=jax.ShapeDtypeStruct(res.shape, F32),
        compiler_params=_cparams(("arbitrary", "arbitrary")),
        name="merge",
    )(o_hg, o_hg, o_ml, o_ml, h_f, h_b, hg_a, ml_a, lru_y, h1, res, mods4,
      w_mg, b_mg, hg_norm, ml_norm, w_branch, w_out)


def _ffn_rows(x_ref, sh_ref, sc_ref, gt_ref, ln_ref, wi_ref, wo_ref, fn_ref, o_ref, r0, rows, *,
              n_inner, final_norm):
    x = x_ref[r0:r0 + rows, :]
    h = _norm_mod(x, ln_ref[...], sh_ref[...], sc_ref[...]).astype(BF16)
    yield
    gu = _dot(h, wi_ref[...])
    yield
    hidden = gu.shape[-1] // 2
    gate = gu[:, :hidden]
    act = (gate * _sigmoid_t(gate) * gu[:, hidden:]).astype(BF16)
    yield
    f = _dot(act, wo_ref[...])
    yield
    y = x + gt_ref[...] * f
    if final_norm:
        y = y * lax.rsqrt(jnp.mean(y * y, axis=-1, keepdims=True) + EPS) * fn_ref[...]
    if n_inner is None:
        o_ref[r0:r0 + rows, :] = y
    else:
        for a in range(rows // n_inner):
            o_ref[:, r0 // n_inner + a, :] = y[a * n_inner:(a + 1) * n_inner, :]
    yield


def _ffn_body(x_ref, sh_ref, sc_ref, gt_ref, ln_ref, wi_ref, wo_ref, fn_ref, o_ref, *, rows, n_inner,
              final_norm):
    gens = [_ffn_rows(x_ref, sh_ref, sc_ref, gt_ref, ln_ref, wi_ref, wo_ref, fn_ref, o_ref, r0, rows,
                      n_inner=n_inner, final_norm=final_norm)
            for r0 in range(0, x_ref.shape[0], rows)]
    skew = 2
    n_phase = 5
    for step in range(n_phase + skew * (len(gens) - 1)):
        for i, g in enumerate(gens):
            if 0 <= step - skew * i < n_phase:
                next(g)


def _ffn_call(l, x, mods4, ctx_row, ln2, wi, wo, fin, *, tm, n_inner, final_norm):
    bsz, n, d_model = x.shape
    hidden = wo.shape[1]
    nt = n // tm
    rows = min(tm, FFN_ROWS)

    def mod_spec(chunk):
        return pl.BlockSpec((None, None, 1, d_model),
                            lambda b, t: (l, bsz if ctx_row else b, 0, chunk))

    if n_inner is None:
        o_spec = pl.BlockSpec((None, tm, d_model), lambda b, t: (b, t, 0))
        o_shape = jax.ShapeDtypeStruct(x.shape, F32)
    else:
        assert rows % n_inner == 0
        n_a = tm // n_inner
        o_spec = pl.BlockSpec((None, n_inner, n_a, d_model), lambda b, t: (b, 0, t, 0))
        o_shape = jax.ShapeDtypeStruct((bsz, n_inner, n // n_inner, d_model), F32)
    out = pl.pallas_call(
        functools.partial(_ffn_body, rows=rows, n_inner=n_inner, final_norm=final_norm),
        grid=(bsz, nt),
        in_specs=[
            pl.BlockSpec((None, tm, d_model), lambda b, t: (b, t, 0)),
            mod_spec(3), mod_spec(4), mod_spec(5),
            pl.BlockSpec((None, 1, d_model), lambda b, t: (l, 0, 0)),
            _resident((None, d_model, 2 * hidden), lambda b, t: (l, 0, 0)),
            _resident((None, hidden, d_model), lambda b, t: (l, 0, 0)),
            pl.BlockSpec((1, d_model), lambda b, t: (0, 0)),
        ],
        out_specs=o_spec,
        out_shape=o_shape,
        compiler_params=_cparams(("arbitrary", "arbitrary")),
        name="ffn",
    )(x, mods4, mods4, mods4, ln2, wi, wo, fin)
    return out.reshape(bsz, n, d_model)


def _prep_in_proj(w_in, b_in):
    depth, d_model, _ = w_in.shape
    hq = N_HEADS * HG_DK
    sizes = (hq, MIX_W, MIX_W, 2 * hq, N_HEADS * ML_DQK, N_HEADS * ML_DQK, MIX_W, MIX_W,
             2 * N_HEADS, 2 * N_HEADS, MIX_W, MIX_W, 3 * d_model)
    offs = np.concatenate([[0], np.cumsum(sizes)])
    assert offs[-1] == w_in.shape[-1]

    def cols(a, lo, hi):
        return a[..., int(offs[lo]):int(offs[hi])]

    def gate_block(a):
        ig = cols(a, 8, 9).reshape(a.shape[:-1] + (2, N_HEADS))
        fg = cols(a, 9, 10).reshape(a.shape[:-1] + (2, N_HEADS))
        pad = jnp.zeros(a.shape[:-1] + (2, GATE_LANES - 2 * N_HEADS), a.dtype)
        return jnp.concatenate([ig, fg, pad], axis=-1).reshape(a.shape[:-1] + (2 * GATE_LANES,))

    groups = [
        (lambda a: cols(a, 0, 3), BF16),
        (lambda a: cols(a, 3, 4), F32),
        (lambda a: cols(a, 4, 8), BF16),
        (gate_block, F32),
        (lambda a: cols(a, 10, 11), F32),
        (lambda a: cols(a, 11, 12), BF16),
    ]
    merge_gates = lambda a: cols(a, 12, 13)
    kinds = ("hgrn2_qig", "hgrn2_forget", None, "mlstm_gates", "lru_x", None)
    b3 = b_in.reshape(depth, 1, -1)
    ws = [f(w_in).astype(BF16) for f, _ in groups]
    bs = [f(b3) for f, _ in groups]
    w_mg = merge_gates(w_in).astype(BF16)
    b_mg = merge_gates(b3)
    return ws, bs, [dt for _, dt in groups], kinds, w_mg, b_mg


def _prep_lru_gates(lru_gate_w, lru_gate_b):
    depth = lru_gate_w.shape[0]
    eye = jnp.eye(LRU_BLOCKS, dtype=lru_gate_w.dtype)
    dense = lru_gate_w[:, :, :, :, :, None, :] * eye[:, None, :, None]
    dense = dense.reshape(depth, 4, MIX_W, MIX_W).astype(BF16)
    return dense, lru_gate_b.reshape(depth, 1, 4 * MIX_W)


def kernel(x, c, ctx, c_ctx, w_ada, b_ada, ln1, w_in, b_in, hg_lb_raw, hg_norm, ml_norm, conv_w, conv_b,
           lru_gate_w, lru_gate_b, lru_lambda, w_branch, w_out, ln2, w_ffn_in, w_ffn_out, final_norm):
    bsz, seq, d_model = x.shape
    lc = ctx.shape[1]
    depth = w_ada.shape[0]
    rows = seq // GRID_W
    assert lc % CHUNK == 0 and seq % lc == 0 and seq % FFN_TM == 0 and bsz < 16

    c_all = jnp.zeros((16, d_model), F32).at[:bsz].set(c).at[bsz].set(c_ctx)
    mods4 = _ada_call(c_all, w_ada, b_ada).reshape(depth, 16, 1, 6 * d_model)
    lb_all = _lb_call(hg_lb_raw)
    lb3 = lb_all.reshape(depth, 1, 2 * MIX_W)
    lb4 = lb_all.reshape(depth, 2, 1, MIX_W)

    ws, bs, out_dtypes, kinds, w_mg, b_mg = _prep_in_proj(w_in, b_in)
    wgate, bgate = _prep_lru_gates(lru_gate_w, lru_gate_b)
    w_ffn_in_b = w_ffn_in.astype(BF16)
    w_ffn_out_b = w_ffn_out.astype(BF16)
    w_branch_b = w_branch.astype(BF16)
    w_out_b = w_out.astype(BF16)
    ln1_3 = ln1.reshape(depth, 1, d_model)
    ln2_3 = ln2.reshape(depth, 1, d_model)
    hgn = hg_norm.reshape(depth, 1, MIX_W)
    mln = ml_norm.reshape(depth, 1, MIX_W)
    conv_b3 = conv_b.reshape(depth, 1, MIX_W)
    fin = final_norm.reshape(1, d_model)

    for l in range(depth):
        last = l == depth - 1
        n_inner = GRID_W if l % 2 == 0 else rows
        hg_a, hg_f, ml_a, ml_g, lru_y, h1, a0, u0, a1, u1 = _in_proj_call(
            l, ctx, x, mods4, ln1_3, lb3, ws, bs, out_dtypes, kinds, conv_w, conv_b3, wgate, bgate, lru_lambda)
        o_hg, o_ml = _mixers_call(l, hg_a, hg_f, lb4, ml_a, ml_g, lc)
        h_f, h_b = _lru_scan_call(a0, u0, a1, u1, lc)
        branches = (o_hg, o_ml, h_f, h_b, hg_a, ml_a, lru_y, h1)
        merge_w = (w_mg, b_mg, hgn, mln, w_branch_b, w_out_b)
        x_m = _merge_call(l, False, lc, *branches, x, mods4, *merge_w)
        x = _ffn_call(l, x_m, mods4, False, ln2_3, w_ffn_in_b, w_ffn_out_b, fin,
                      tm=FFN_TM, n_inner=n_inner, final_norm=last)
        if not last:
            ctx_m = _merge_call(l, True, lc, *branches, ctx, mods4, *merge_w)
            ctx = _ffn_call(l, ctx_m, mods4, True, ln2_3, w_ffn_in_b, w_ffn_out_b, fin,
                            tm=lc, n_inner=None, final_norm=False)
    return x
```

```python
import functools
import itertools

import numpy as np
import jax
import jax.numpy as jnp
from jax import lax
from jax.experimental import pallas as pl
from jax.experimental.pallas import tpu as pltpu

F32 = jnp.float32
BF16 = jnp.bfloat16

GRID_W = 64
MIX_W = 512
N_HEADS = 4
HEAD_DV = 128
HG_DK = 128
ML_DQK = 64
LRU_BLOCKS = 8
LRU_BD = 64
LRU_C = 8.0
EPS = 1e-6
NEG_BIG = -1e30
LB_TINY = 1e-30
LOG2_E = 1.4426950408889634
GATE_LANES = 128
CHUNK = 128
HG_GROUP = 1
MIX_LATE = 0
MIX_LAG = 2
MIX_NB = 8
MIX_SKEW = 15
FFN_TM = 512
FFN_ROWS = 256
MERGE_ROWS = 128
IN_PROJ_ROWS = 256
LRU_PITCH_PAD = 8
VMEM_LIMIT = 56 * 1024 * 1024


def _cparams(sem):
    return pltpu.CompilerParams(dimension_semantics=sem, vmem_limit_bytes=VMEM_LIMIT)


def _resident(shape, index_map):
    return pl.BlockSpec(shape, index_map, pipeline_mode=pl.Buffered(1))


def _sigmoid(x):
    return 1.0 / (1.0 + jnp.exp(-x))


def _sigmoid_t(x):
    return 0.5 * jnp.tanh(0.5 * x) + 0.5


def _log_sigmoid(x):
    return jnp.minimum(x, 0.0) - jnp.log(1.0 + jnp.exp(-jnp.abs(x)))


def _dot(a, b):
    return jnp.dot(a, b, preferred_element_type=F32)


def _dot_nt(a, b):
    return lax.dot_general(a, b, (((1,), (1,)), ((), ())), preferred_element_type=F32)


def _dot_tn(a, b):
    return lax.dot_general(a, b, (((0,), (0,)), ((), ())), preferred_element_type=F32)


def _split_hi_lo(x):
    hi = x.astype(BF16)
    lo = (x - hi.astype(F32)).astype(BF16)
    return hi, lo


def _norm_mod(x, ln, shift, scale):
    gain = ln * (1.0 + scale)
    return (x * lax.rsqrt(jnp.mean(x * x, axis=-1, keepdims=True) + EPS)) * gain + shift


def _mirror(m):
    return m[..., ::-1, ::-1].copy()


@functools.lru_cache(maxsize=None)
def _hgrn2_consts(C):
    n_lv = int(np.log2(C))
    mats, masks = [], []
    r = np.arange(C)
    for lv in range(n_lv):
        s = C >> (lv + 1)
        base = (r // (2 * s)) * (2 * s)
        mid = base + s - 1
        odd = r >= base + s
        m = np.zeros((C, C), np.float32)
        for t in range(C):
            if odd[t]:
                m[t, mid[t] + 1:t + 1] = 1.0
            else:
                m[t, t + 1:mid[t] + 1] = 1.0
        mats.append(m)
        same = base[:, None] == base[None, :]
        masks.append((same & odd[:, None] & (~odd)[None, :]).astype(np.float32))
    mats.append(np.tril(np.ones((C, C), np.float32)))
    masks.append(np.eye(C, dtype=np.float32))
    mats = np.stack(mats)
    masks = np.stack(masks)
    mats = np.stack([mats, _mirror(mats)])
    masks = np.stack([masks, _mirror(masks)])
    mst = mats.reshape(2, (n_lv + 1) * C, C)
    mst = np.concatenate([mst, mst], axis=-1)
    return mst, masks, n_lv


@functools.lru_cache(maxsize=None)
def _mlstm_consts(C):
    tri = np.tril(np.ones((C, C), np.float32))
    tri = np.stack([tri, _mirror(tri)])
    tri_cat = np.concatenate([tri, tri], axis=-1)
    tri_t = np.transpose(tri, (0, 2, 1))
    tri_t_cat = np.concatenate([tri_t, tri_t], axis=1)
    return tri_cat, tri_t_cat, tri


def _chunk_index(d, j, nctx, nch):
    bw = jnp.where(j < nctx, nctx - 1 - j, nch - 1 + nctx - j)
    return jnp.where(d == 0, j, bw)


def _ada_body(c_ref, w_ref, b_ref, o_ref):
    cc = c_ref[...]
    s = cc * _sigmoid(cc)
    o_ref[...] = jnp.dot(s, w_ref[...], preferred_element_type=F32,
                         precision=lax.Precision.HIGHEST) + b_ref[...]


def _ada_call(c_all, w_ada, b_ada):
    depth, d_model, n6 = w_ada.shape
    rows = c_all.shape[0]
    tn = 1536
    return pl.pallas_call(
        _ada_body,
        grid=(depth, n6 // tn),
        in_specs=[
            pl.BlockSpec((rows, d_model), lambda l, n: (0, 0)),
            pl.BlockSpec((None, d_model, tn), lambda l, n: (l, 0, n)),
            pl.BlockSpec((None, 1, tn), lambda l, n: (l, 0, n)),
        ],
        out_specs=pl.BlockSpec((None, rows, tn), lambda l, n: (l, 0, n)),
        out_shape=jax.ShapeDtypeStruct((depth, rows, n6), F32),
        compiler_params=_cparams(("arbitrary", "arbitrary")),
        name="ada_mod",
    )(c_all, w_ada, b_ada.reshape(depth, 1, n6))


def _lb_body(raw_ref, o_ref):
    raw = raw_ref[...]
    depth = raw.shape[0]
    e = jnp.exp(raw - jnp.max(raw, axis=0, keepdims=True))
    p = e / jnp.sum(e, axis=0, keepdims=True)
    acc = jnp.zeros_like(p[0:1])
    for l in range(depth):
        acc = acc + p[l:l + 1]
        o_ref[l:l + 1, :] = acc - p[0:1]


def _lb_call(hg_lb_raw):
    depth = hg_lb_raw.shape[0]
    raw = hg_lb_raw.reshape(depth, -1)
    return pl.pallas_call(
        _lb_body,
        out_shape=jax.ShapeDtypeStruct(raw.shape, F32),
        name="hgrn2_lower_bounds",
    )(raw)


def _log2_forget(f_pre, lb):
    half = 0.5 * (1.0 - lb)
    floor = jnp.maximum(lb, LB_TINY)
    return jnp.log2(jnp.maximum((floor + half) + half * jnp.tanh(0.5 * f_pre), floor))


def _lru_decay_input(r_pre, i_pre, lam, xc):
    z = -lam
    softplus = jnp.maximum(z, 0.0) + jnp.log(1.0 + jnp.exp(-jnp.abs(z)))
    log_a = (-LRU_C * softplus) * _sigmoid_t(r_pre)
    a = jnp.exp(log_a)
    one_m_a = -jnp.tanh(0.5 * log_a) * (a + 1.0)
    one_m_a2 = one_m_a * (a + 1.0)
    return one_m_a, jnp.sqrt(jnp.maximum(one_m_a2, 0.0)) * (_sigmoid_t(i_pre) * xc)


def _in_proj_body(ctx_ref, x_ref, xp_ref, xn_ref, sh_ref, sc_ref, ln_ref, lb_ref,
                  cw_ref, cb_ref, wg_ref, bg_ref, lam_ref, *rest, kinds):
    n_grp = len(kinds)
    w_refs = rest[:n_grp]
    b_refs = rest[n_grp:2 * n_grp]
    outs = list(rest[2 * n_grp:])
    ext_ref = outs.pop()
    lru_refs = [outs.pop() for _ in range(4)][::-1]
    h_ref = outs.pop()
    o_refs = iter(outs)
    o_refs = [None if kind == "lru_x" else next(o_refs) for kind in kinds]
    t = pl.program_id(1)
    tm = x_ref.shape[0]
    rows = min(tm, IN_PROJ_ROWS)
    lru_i = kinds.index("lru_x")

    def norm(x):
        return _norm_mod(x, ln_ref[...], sh_ref[...], sc_ref[...]).astype(BF16)

    def finish(kind, y, o_ref, r0):
        rs = slice(r0, r0 + rows)
        if kind == "hgrn2_qig":
            q = y[:, :MIX_W]
            o_ref[rs, :MIX_W] = (q * _sigmoid_t(q)).astype(o_ref.dtype)
            o_ref[rs, MIX_W:] = y[:, MIX_W:].astype(o_ref.dtype)
        elif kind == "hgrn2_forget":
            o_ref[rs, :] = _log2_forget(y, lb_ref[...]).astype(o_ref.dtype)
        elif kind == "mlstm_gates":
            lane = lax.broadcasted_iota(jnp.int32, y.shape, 1) % GATE_LANES
            is_f = jnp.logical_and(lane >= N_HEADS, lane < 2 * N_HEADS)
            o_ref[rs, :] = (jnp.where(is_f, _log_sigmoid(y), y) * LOG2_E).astype(o_ref.dtype)
        elif kind == "lru_x":
            ext_ref[8 + r0:8 + r0 + rows, :] = y
        else:
            o_ref[rs, :] = y.astype(o_ref.dtype)

    def row_group(r0):
        rs = slice(r0, r0 + rows)
        h = norm(jnp.where(t == 0, ctx_ref[rs, :], x_ref[rs, :]))
        h_ref[rs, :] = h
        yield
        rest_order = sorted((i for i in range(n_grp) if i != lru_i), key=lambda i: kinds[i] is None)
        for i in [lru_i] + rest_order:
            finish(kinds[i], _dot(h, w_refs[i][...]) + b_refs[i][...], o_refs[i], r0)
            yield

    def lru_gates(wait):
        prev_ok = t >= 2
        next_ok = jnp.logical_and(t >= 1, t < pl.num_programs(1) - 1)
        w_lx, b_lx = w_refs[lru_i][...], b_refs[lru_i][...]
        ext_ref[0:8, :] = jnp.where(prev_ok, _dot(norm(xp_ref[...]), w_lx) + b_lx, 0.0)
        ext_ref[8 + tm:16 + tm, :] = jnp.where(next_ok, _dot(norm(xn_ref[...]), w_lx) + b_lx, 0.0)
        for _ in range(wait):
            yield
        cw = cw_ref[...]
        xc = (cw[0:1] * ext_ref[6:6 + tm, :] + cw[1:2] * ext_ref[7:7 + tm, :] + cw[2:3] * ext_ref[8:8 + tm, :]
              + cw[3:4] * ext_ref[9:9 + tm, :]) + cb_ref[...]
        yield
        xcb = xc.astype(BF16)
        gates = [_dot(xcb, wg_ref[n]) + bg_ref[:, n * MIX_W:(n + 1) * MIX_W] for n in range(4)]
        yield
        for d in range(2):
            for r0 in range(0, tm, rows):
                rs = slice(r0, r0 + rows)
                one_m_a, u = _lru_decay_input(gates[2 * d][rs, :], gates[2 * d + 1][rs, :],
                                              lam_ref[d:d + 1, :], xc[rs, :])
                lru_refs[2 * d][rs, :] = one_m_a.astype(lru_refs[2 * d].dtype)
                lru_refs[2 * d + 1][rs, :] = u.astype(lru_refs[2 * d + 1].dtype)
                yield

    groups = [row_group(r0) for r0 in range(0, tm, rows)]
    gens = groups + [lru_gates(wait=len(groups) + 1)]
    live = list(gens)
    step = 0
    while live:
        for i, g in enumerate(gens):
            if g in live and (step >= i or g is gens[-1]):
                try:
                    next(g)
                except StopIteration:
                    live.remove(g)
        step += 1


def _in_proj_call(l, ctx, x, mods4, ln1, lb3, ws, bs, out_dtypes, kinds, conv_w, conv_b, wgate, bgate, lam):
    bsz, lc, d_model = ctx.shape
    seq = x.shape[1]
    tm = lc
    nt = 1 + seq // tm
    h8 = tm // 8

    def mod_spec(chunk):
        return pl.BlockSpec((None, None, 1, d_model),
                            lambda b, t: (l, jnp.where(t == 0, bsz, b), 0, chunk))

    def per_layer(shape):
        return pl.BlockSpec((None,) + shape, lambda b, t: (l,) + (0,) * len(shape))

    in_specs = [
        pl.BlockSpec((None, tm, d_model), lambda b, t: (b, 0, 0)),
        pl.BlockSpec((None, tm, d_model), lambda b, t: (b, jnp.maximum(t - 1, 0), 0)),
        pl.BlockSpec((None, 8, d_model), lambda b, t: (b, jnp.maximum((t - 1) * h8 - 1, 0), 0)),
        pl.BlockSpec((None, 8, d_model), lambda b, t: (b, jnp.minimum(jnp.maximum(t, 1) * h8, seq // 8 - 1), 0)),
        mod_spec(0), mod_spec(1),
        per_layer((1, d_model)),
        per_layer((1, lb3.shape[-1])),
        per_layer((4, MIX_W)), per_layer((1, MIX_W)),
        per_layer((4, MIX_W, MIX_W)), per_layer((1, 4 * MIX_W)), per_layer((2, MIX_W)),
    ]
    in_specs += [per_layer((d_model, w.shape[-1])) for w in ws]
    in_specs += [per_layer((1, w.shape[-1])) for w in ws]
    tok = lambda width: pl.BlockSpec((None, tm, width), lambda b, t: (b, t, 0))
    out_specs, out_shape = [], []
    for w, dt, kind in zip(ws, out_dtypes, kinds):
        if kind != "lru_x":
            out_specs.append(tok(w.shape[-1]))
            out_shape.append(jax.ShapeDtypeStruct((bsz, lc + seq, w.shape[-1]), dt))
    out_specs.append(tok(d_model))
    out_shape.append(jax.ShapeDtypeStruct((bsz, lc + seq, d_model), BF16))
    out_specs += [tok(MIX_W)] * 4
    out_shape += [jax.ShapeDtypeStruct((bsz, lc + seq, MIX_W), BF16)] * 4
    return pl.pallas_call(
        functools.partial(_in_proj_body, kinds=kinds),
        grid=(bsz, nt),
        in_specs=in_specs,
        out_specs=out_specs,
        out_shape=out_shape,
        scratch_shapes=[pltpu.VMEM((tm + 16, MIX_W), F32)],
        compiler_params=_cparams(("arbitrary", "arbitrary")),
        name="in_proj",
    )(ctx, x, x, x, mods4, mods4, ln1, lb3, conv_w, conv_b, wgate, bgate, lam, *ws, *bs)


def _hgrn2_body(qv_ref, g_ref, lb_ref, mst_ref, msk_ref, o_ref, st_ref, *, C, n_lv):
    q16 = qv_ref[:, :MIX_W]
    v = qv_ref[:, MIX_W:]
    q = q16.astype(F32)
    g = g_ref[...]
    lb = lb_ref[...]
    kk = (1.0 - jnp.exp2(g)) + (jnp.maximum(lb, LB_TINY) - lb)
    g_hi, g_lo = _split_hi_lo(g)
    ex = _dot(mst_ref[...], jnp.concatenate([g_hi, g_lo], axis=0))
    tot = jnp.sum(g, axis=0, keepdims=True)
    k16 = kk.astype(BF16)

    for h0 in range(0, N_HEADS, HG_GROUP):
        heads = range(h0, h0 + HG_GROUP)
        sls = {h: slice(h * HG_DK, (h + 1) * HG_DK) for h in heads}
        p = {h: msk_ref[n_lv] * _dot_nt(q16[:, sls[h]], k16[:, sls[h]]) for h in heads}
        for lv in range(n_lv):
            for h in heads:
                w = jnp.exp2(ex[lv * C:(lv + 1) * C, sls[h]]).astype(BF16)
                p[h] = p[h] + msk_ref[lv] * _dot_nt(q16[:, sls[h]] * w, k16[:, sls[h]] * w)
            if lv % 2 == 1:
                yield
        for h in heads:
            sl = sls[h]
            st = st_ref[h]
            b_in = ex[n_lv * C:(n_lv + 1) * C, sl]
            qb = (q[:, sl] * jnp.exp2(b_in)).astype(BF16)
            vh = v[:, sl]
            o_ref[:, sl] = (_dot(p[h].astype(BF16), vh) + _dot_nt(qb, st.astype(BF16))).astype(o_ref.dtype)
            tot_h = tot[:, sl]
            kb = (kk[:, sl] * jnp.exp2(tot_h - b_in)).astype(BF16)
            st_ref[h] = st * jnp.exp2(tot_h) + _dot_tn(vh, kb)
            yield


def _mlstm_body(qkv_ref, g_ref, tri_ref, trit_ref, msk_ref, o_ref, c_ref, m_ref, *, C):
    qk_w = N_HEADS * ML_DQK
    gates = g_ref[...]
    gates_t = gates.T
    lf_hi, lf_lo = _split_hi_lo(gates)
    b_cols = _dot(tri_ref[...], jnp.concatenate([lf_hi, lf_lo], axis=0))
    lft_hi, lft_lo = _split_hi_lo(gates_t)
    b_rows = _dot(jnp.concatenate([lft_hi, lft_lo], axis=1), trit_ref[...])
    tot = jnp.sum(gates, axis=0, keepdims=True)
    allowed = msk_ref[...] > 0.0
    ones = jnp.ones((C, HEAD_DV), BF16)
    log2_scale = 0.5 * np.log2(ML_DQK)

    for h in range(N_HEADS):
        b_col = b_cols[:, N_HEADS + h:N_HEADS + h + 1]
        c_col = gates[:, h:h + 1] - b_col
        c_row = gates_t[h:h + 1, :] - b_rows[N_HEADS + h:N_HEADS + h + 1, :]
        m_prev = m_ref[h][0:1, 0:1]
        cm = jnp.where(allowed, c_row, NEG_BIG)
        m_t = jnp.maximum(jnp.max(cm, axis=-1, keepdims=True), m_prev)
        yield
        qh = qkv_ref[:, h * ML_DQK:(h + 1) * ML_DQK]
        kh = qkv_ref[:, qk_w + h * ML_DQK:qk_w + (h + 1) * ML_DQK]
        vh = qkv_ref[:, 2 * qk_w + h * HEAD_DV:2 * qk_w + (h + 1) * HEAD_DV]
        vaug = jnp.concatenate([vh, ones], axis=1)
        s = _dot_nt(qh, kh) * jnp.exp2(cm - (m_t + log2_scale))
        qs = (qh.astype(F32) * jnp.exp2(m_prev - m_t - log2_scale)).astype(BF16)
        yield
        c_aug = c_ref[h]
        r = _dot(jnp.concatenate([s.astype(BF16), qs], axis=1),
                 jnp.concatenate([vaug, c_aug.astype(BF16)], axis=0))
        num = r[:, :HEAD_DV]
        den = r[:, HEAD_DV:]
        hout = num / jnp.maximum(jnp.abs(den), jnp.exp2(-(b_col + m_t)))
        o_ref[:, h * HEAD_DV:(h + 1) * HEAD_DV] = hout.astype(o_ref.dtype)
        yield
        b_last = tot[:, N_HEADS + h:N_HEADS + h + 1]
        log_w = b_last + c_col
        m_new = jnp.maximum(b_last + m_prev, jnp.max(log_w, axis=0, keepdims=True))
        kw = (kh.astype(F32) * jnp.exp2(log_w - m_new)).astype(BF16)
        c_ref[h] = jnp.exp2(b_last + m_prev - m_new) * c_aug + _dot_tn(kw, vaug)
        m_ref[h] = jnp.broadcast_to(m_new, m_ref.shape[1:])
        yield


def _mixers_body(hqv_ref, hg_ref, lb_ref, mst_ref, hmsk_ref, mqkv_ref, mg_ref,
                 tri_ref, trit_ref, mmsk_ref, o_hg_ref, o_ml_ref, st_ref, c_ref, m_ref, *, C, n_lv):
    @pl.when(pl.program_id(2) == 0)
    def _():
        st_ref[...] = jnp.zeros_like(st_ref)
        c_ref[...] = jnp.zeros_like(c_ref)
        m_ref[...] = jnp.zeros_like(m_ref)

    def element(bb):
        gens = [_hgrn2_body(hqv_ref.at[bb], hg_ref.at[bb], lb_ref, mst_ref, hmsk_ref,
                            o_hg_ref.at[bb], st_ref.at[bb], C=C, n_lv=n_lv),
                _mlstm_body(mqkv_ref.at[bb], mg_ref.at[bb], tri_ref, trit_ref,
                            mmsk_ref, o_ml_ref.at[bb], c_ref.at[bb], m_ref.at[bb], C=C)]
        gens[MIX_LATE] = itertools.chain([None] * MIX_LAG, gens[MIX_LATE])
        for _ in itertools.zip_longest(*gens):
            yield

    streams = [element(bb) for bb in range(hqv_ref.shape[0])]
    live = []
    while streams or live:
        if streams and (not live or live[-1][1] >= MIX_SKEW):
            live.append([streams.pop(0), 0])
        for item in list(live):
            try:
                next(item[0])
                item[1] += 1
            except StopIteration:
                live.remove(item)


def _mixers_call(l, hg_a, hg_f, lb4, ml_a, ml_g, lc):
    bsz, tt, _ = hg_a.shape
    C = CHUNK
    nch, nctx = tt // C, lc // C
    mst, hmsk, n_lv = _hgrn2_consts(C)
    mst = jnp.asarray(mst, BF16)
    hmsk = jnp.asarray(hmsk, F32)
    tri_cat, tri_t_cat, tri = _mlstm_consts(C)
    tri_cat = jnp.asarray(tri_cat, BF16)
    tri_t_cat = jnp.asarray(tri_t_cat, BF16)
    mmsk = jnp.asarray(tri, F32)
    cidx = functools.partial(_chunk_index, nctx=nctx, nch=nch)
    qk_w = N_HEADS * ML_DQK
    tok = lambda col: (lambda b, d, j: (b, cidx(d, j), col))
    tok_d = lambda b, d, j: (b, cidx(d, j), d)
    per_dir3 = lambda b, d, j: (d, 0, 0)
    nb = max(n for n in range(1, MIX_NB + 1) if bsz % n == 0)
    o_spec = pl.BlockSpec((nb, None, C, MIX_W), lambda b, d, j: (b, d, cidx(d, j), 0))
    o_shape = jax.ShapeDtypeStruct((bsz, 2, tt, MIX_W), BF16)
    return pl.pallas_call(
        functools.partial(_mixers_body, C=C, n_lv=n_lv),
        grid=(bsz // nb, 2, nch),
        in_specs=[
            pl.BlockSpec((nb, C, 2 * MIX_W), tok(0)),
            pl.BlockSpec((nb, C, MIX_W), tok_d),
            pl.BlockSpec((None, None, 1, MIX_W), lambda b, d, j: (l, d, 0, 0)),
            pl.BlockSpec((None,) + mst.shape[1:], per_dir3),
            pl.BlockSpec((None,) + hmsk.shape[1:], lambda b, d, j: (d, 0, 0, 0)),
            pl.BlockSpec((nb, C, 2 * qk_w + MIX_W), tok(0)),
            pl.BlockSpec((nb, C, GATE_LANES), tok_d),
            pl.BlockSpec((None, C, 2 * C), per_dir3),
            pl.BlockSpec((None, 2 * C, C), per_dir3),
            pl.BlockSpec((None, C, C), per_dir3),
        ],
        out_specs=[o_spec, o_spec],
        out_shape=[o_shape, o_shape],
        scratch_shapes=[pltpu.VMEM((nb, N_HEADS, HEAD_DV, HG_DK), F32),
                        pltpu.VMEM((nb, N_HEADS, ML_DQK, 2 * HEAD_DV), F32),
                        pltpu.VMEM((nb, N_HEADS, 8, 128), F32)],
        compiler_params=_cparams(("arbitrary", "arbitrary", "arbitrary")),
        name="mixers",
    )(hg_a, hg_f, lb4, mst, hmsk, ml_a, ml_g, tri_cat, tri_t_cat, mmsk)


def _lru_scan_body(a0_ref, u0_ref, a1_ref, u1_ref, hf_ref, hb_ref, sf_ref, sb_ref,
                   pa0_ref, pu0_ref, pa1_ref, pu1_ref, *, tc):
    @pl.when(pl.program_id(0) == 0)
    def _():
        sf_ref[...] = jnp.zeros_like(sf_ref)
        sb_ref[...] = jnp.zeros_like(sb_ref)

    for src, dst in ((u0_ref, pu0_ref), (u1_ref, pu1_ref)):
        dst[:, 0:tc, :] = src[...].astype(F32)
    for src, dst in ((a0_ref, pa0_ref), (a1_ref, pa1_ref)):
        dst[:, 0:tc, :] = 1.0 - src[...].astype(F32)

    def step(i, carry):
        hf, hb = carry
        hf = pa0_ref[:, i, :] * hf + pu0_ref[:, i, :]
        pu0_ref[:, i, :] = hf
        ib = tc - 1 - i
        hb = pa1_ref[:, ib, :] * hb + pu1_ref[:, ib, :]
        pu1_ref[:, ib, :] = hb
        return hf, hb

    hf, hb = lax.fori_loop(0, tc, step, (sf_ref[...], sb_ref[...]), unroll=8)
    sf_ref[...] = hf
    sb_ref[...] = hb
    hf_ref[...] = pu0_ref[:, 0:tc, :].astype(hf_ref.dtype)
    hb_ref[...] = pu1_ref[:, 0:tc, :].astype(hb_ref.dtype)


def _lru_scan_call(a0, u0, a1, u1, lc):
    bsz, tt, w = a0.shape
    tc = CHUNK
    nch, nctx = tt // tc, lc // tc
    fw = pl.BlockSpec((bsz, tc, w), lambda j: (0, j, 0))
    bw = pl.BlockSpec((bsz, tc, w), lambda j: (0, _chunk_index(1, j, nctx, nch), 0))
    out = jax.ShapeDtypeStruct((bsz, tt, w), BF16)
    return pl.pallas_call(
        functools.partial(_lru_scan_body, tc=tc),
        grid=(nch,),
        in_specs=[fw, fw, bw, bw],
        out_specs=[fw, bw],
        out_shape=[out, out],
        scratch_shapes=[pltpu.VMEM((bsz, w), F32), pltpu.VMEM((bsz, w), F32)]
        + [pltpu.VMEM((bsz, tc + LRU_PITCH_PAD, w), F32)] * 4,
        compiler_params=_cparams(("arbitrary",)),
        name="rglru_scan",
    )(a0, u0, a1, u1)


def _head_rms(o, w):
    parts = []
    for h in range(N_HEADS):
        oh = o[:, h * HEAD_DV:(h + 1) * HEAD_DV]
        parts.append(oh * lax.rsqrt(jnp.mean(oh * oh, axis=-1, keepdims=True) + EPS))
    return jnp.concatenate(parts, axis=1) * w


def _gelu_tanh(x):
    return 0.5 * x * (1.0 + jnp.tanh(0.7978845608028654 * (x + 0.044715 * (x * x * x))))


def _merge_rows(hg0_ref, hg1_ref, ml0_ref, ml1_ref, lf_ref, lb_ref, hgg_ref, mlo_ref, ly_ref, h_ref,
                res_ref, gate_ref, wmg_ref, bmg_ref, hgn_ref, mln_ref, wb_ref, wo_ref, o_ref,
                r0, rows, d_model):
    rs = slice(r0, r0 + rows)
    hgg = hgg_ref[rs, :].astype(F32)
    o_hg = hg0_ref[rs, :].astype(F32) + hg1_ref[rs, :].astype(F32)
    o_ml = ml0_ref[rs, :].astype(F32) + ml1_ref[rs, :].astype(F32)
    a_out = _head_rms(o_hg, hgn_ref[...]) * (hgg * _sigmoid_t(hgg))
    b_out = _head_rms(o_ml, mln_ref[...]) * _sigmoid_t(mlo_ref[rs, :].astype(F32))
    h_lru = lf_ref[rs, :].astype(F32) + lb_ref[rs, :].astype(F32)
    c_out = h_lru * _gelu_tanh(ly_ref[rs, :].astype(F32))
    h = h_ref[rs, :]
    yield
    merged = None
    for n, br in enumerate((a_out, b_out, c_out)):
        cs = slice(n * d_model, (n + 1) * d_model)
        gate = _sigmoid_t((_dot(h, wmg_ref[:, cs]) + bmg_ref[:, cs]).astype(BF16))
        term = gate * _dot(br.astype(BF16), wb_ref[n]).astype(BF16)
        merged = term if merged is None else merged + term
        yield
    o_ref[rs, :] = res_ref[rs, :] + gate_ref[...] * _dot(merged, wo_ref[...])
    yield


def _merge_body(*refs, rows, d_model):
    tm = refs[-1].shape[0]
    gens = [_merge_rows(*refs, r0, rows, d_model) for r0 in range(0, tm, rows)]
    n_phase = 5
    for step in range(n_phase + len(gens) - 1):
        for i, g in enumerate(gens):
            if 0 <= step - i < n_phase:
                next(g)


def _merge_call(l, is_ctx, tm, o_hg, o_ml, h_f, h_b, hg_a, ml_a, lru_y, h1, res, mods4,
                w_mg, b_mg, hg_norm, ml_norm, w_branch, w_out):
    bsz, n, d_model = res.shape
    t0 = 0 if is_ctx else 1
    rows = min(tm, MERGE_ROWS)

    def tok(col):
        return lambda b, t: (b, t + t0, col)

    mix = lambda d: pl.BlockSpec((None, None, tm, MIX_W), lambda b, t: (b, d, t + t0, 0))
    in_specs = [
        mix(0), mix(1), mix(0), mix(1),
        pl.BlockSpec((None, tm, MIX_W), tok(0)),
        pl.BlockSpec((None, tm, MIX_W), tok(0)),
        pl.BlockSpec((None, tm, MIX_W), tok(2)),
        pl.BlockSpec((None, tm, MIX_W), tok(2)),
        pl.BlockSpec((None, tm, MIX_W), tok(0)),
        pl.BlockSpec((None, tm, d_model), tok(0)),
        pl.BlockSpec((None, tm, d_model), lambda b, t: (b, t, 0)),
        pl.BlockSpec((None, None, 1, d_model), lambda b, t: (l, bsz if is_ctx else b, 0, 2)),
        _resident((None, d_model, 3 * d_model), lambda b, t: (l, 0, 0)),
        pl.BlockSpec((None, 1, 3 * d_model), lambda b, t: (l, 0, 0)),
        pl.BlockSpec((None, 1, MIX_W), lambda b, t: (l, 0, 0)),
        pl.BlockSpec((None, 1, MIX_W), lambda b, t: (l, 0, 0)),
        _resident((None, 3, MIX_W, d_model), lambda b, t: (l, 0, 0, 0)),
        _resident((None, d_model, d_model), lambda b, t: (l, 0, 0)),
    ]
    return pl.pallas_call(
        functools.partial(_merge_body, rows=rows, d_model=d_model),
        grid=(bsz, n // tm),
        in_specs=in_specs,
        out_specs=pl.BlockSpec((None, tm, d_model), lambda b, t: (b, t, 0)),
        out_shape=jax.ShapeDtypeStruct(res.shape, F32),
        compiler_params=_cparams(("arbitrary", "arbitrary")),
        name="merge",
    )(o_hg, o_hg, o_ml, o_ml, h_f, h_b, hg_a, ml_a, lru_y, h1, res, mods4,
      w_mg, b_mg, hg_norm, ml_norm, w_branch, w_out)


def _ffn_rows(x_ref, sh_ref, sc_ref, gt_ref, ln_ref, wi_ref, wo_ref, fn_ref, o_ref, r0, rows, *,
              n_inner, final_norm):
    x = x_ref[r0:r0 + rows, :]
    h = _norm_mod(x, ln_ref[...], sh_ref[...], sc_ref[...]).astype(BF16)
    yield
    gu = _dot(h, wi_ref[...])
    yield
    hidden = gu.shape[-1] // 2
    gate = gu[:, :hidden]
    act = (gate * _sigmoid_t(gate) * gu[:, hidden:]).astype(BF16)
    yield
    f = _dot(act, wo_ref[...])
    yield
    y = x + gt_ref[...] * f
    if final_norm:
        y = y * lax.rsqrt(jnp.mean(y * y, axis=-1, keepdims=True) + EPS) * fn_ref[...]
    if n_inner is None:
        o_ref[r0:r0 + rows, :] = y
    else:
        for a in range(rows // n_inner):
            o_ref[:, r0 // n_inner + a, :] = y[a * n_inner:(a + 1) * n_inner, :]
    yield


def _ffn_body(x_ref, sh_ref, sc_ref, gt_ref, ln_ref, wi_ref, wo_ref, fn_ref, o_ref, *, rows, n_inner,
              final_norm):
    gens = [_ffn_rows(x_ref, sh_ref, sc_ref, gt_ref, ln_ref, wi_ref, wo_ref, fn_ref, o_ref, r0, rows,
                      n_inner=n_inner, final_norm=final_norm)
            for r0 in range(0, x_ref.shape[0], rows)]
    skew = 2
    n_phase = 5
    for step in range(n_phase + skew * (len(gens) - 1)):
        for i, g in enumerate(gens):
            if 0 <= step - skew * i < n_phase:
                next(g)


def _ffn_call(l, x, mods4, ctx_row, ln2, wi, wo, fin, *, tm, n_inner, final_norm):
    bsz, n, d_model = x.shape
    hidden = wo.shape[1]
    nt = n // tm
    rows = min(tm, FFN_ROWS)

    def mod_spec(chunk):
        return pl.BlockSpec((None, None, 1, d_model),
                            lambda b, t: (l, bsz if ctx_row else b, 0, chunk))

    if n_inner is None:
        o_spec = pl.BlockSpec((None, tm, d_model), lambda b, t: (b, t, 0))
        o_shape = jax.ShapeDtypeStruct(x.shape, F32)
    else:
        assert rows % n_inner == 0
        n_a = tm // n_inner
        o_spec = pl.BlockSpec((None, n_inner, n_a, d_model), lambda b, t: (b, 0, t, 0))
        o_shape = jax.ShapeDtypeStruct((bsz, n_inner, n // n_inner, d_model), F32)
    out = pl.pallas_call(
        functools.partial(_ffn_body, rows=rows, n_inner=n_inner, final_norm=final_norm),
        grid=(bsz, nt),
        in_specs=[
            pl.BlockSpec((None, tm, d_model), lambda b, t: (b, t, 0)),
            mod_spec(3), mod_spec(4), mod_spec(5),
            pl.BlockSpec((None, 1, d_model), lambda b, t: (l, 0, 0)),
            _resident((None, d_model, 2 * hidden), lambda b, t: (l, 0, 0)),
            _resident((None, hidden, d_model), lambda b, t: (l, 0, 0)),
            pl.BlockSpec((1, d_model), lambda b, t: (0, 0)),
        ],
        out_specs=o_spec,
        out_shape=o_shape,
        compiler_params=_cparams(("arbitrary", "arbitrary")),
        name="ffn",
    )(x, mods4, mods4, mods4, ln2, wi, wo, fin)
    return out.reshape(bsz, n, d_model)


def _prep_in_proj(w_in, b_in):
    depth, d_model, _ = w_in.shape
    hq = N_HEADS * HG_DK
    sizes = (hq, MIX_W, MIX_W, 2 * hq, N_HEADS * ML_DQK, N_HEADS * ML_DQK, MIX_W, MIX_W,
             2 * N_HEADS, 2 * N_HEADS, MIX_W, MIX_W, 3 * d_model)
    offs = np.concatenate([[0], np.cumsum(sizes)])
    assert offs[-1] == w_in.shape[-1]

    def cols(a, lo, hi):
        return a[..., int(offs[lo]):int(offs[hi])]

    def gate_block(a):
        ig = cols(a, 8, 9).reshape(a.shape[:-1] + (2, N_HEADS))
        fg = cols(a, 9, 10).reshape(a.shape[:-1] + (2, N_HEADS))
        pad = jnp.zeros(a.shape[:-1] + (2, GATE_LANES - 2 * N_HEADS), a.dtype)
        return jnp.concatenate([ig, fg, pad], axis=-1).reshape(a.shape[:-1] + (2 * GATE_LANES,))

    groups = [
        (lambda a: cols(a, 0, 3), BF16),
        (lambda a: cols(a, 3, 4), F32),
        (lambda a: cols(a, 4, 8), BF16),
        (gate_block, F32),
        (lambda a: cols(a, 10, 11), F32),
        (lambda a: cols(a, 11, 12), BF16),
    ]
    merge_gates = lambda a: cols(a, 12, 13)
    kinds = ("hgrn2_qig", "hgrn2_forget", None, "mlstm_gates", "lru_x", None)
    b3 = b_in.reshape(depth, 1, -1)
    ws = [f(w_in).astype(BF16) for f, _ in groups]
    bs = [f(b3) for f, _ in groups]
    w_mg = merge_gates(w_in).astype(BF16)
    b_mg = merge_gates(b3)
    return ws, bs, [dt for _, dt in groups], kinds, w_mg, b_mg


def _prep_lru_gates(lru_gate_w, lru_gate_b):
    depth = lru_gate_w.shape[0]
    eye = jnp.eye(LRU_BLOCKS, dtype=lru_gate_w.dtype)
    dense = lru_gate_w[:, :, :, :, :, None, :] * eye[:, None, :, None]
    dense = dense.reshape(depth, 4, MIX_W, MIX_W).astype(BF16)
    return dense, lru_gate_b.reshape(depth, 1, 4 * MIX_W)


def kernel(x, c, ctx, c_ctx, w_ada, b_ada, ln1, w_in, b_in, hg_lb_raw, hg_norm, ml_norm, conv_w, conv_b,
           lru_gate_w, lru_gate_b, lru_lambda, w_branch, w_out, ln2, w_ffn_in, w_ffn_out, final_norm):
    bsz, seq, d_model = x.shape
    lc = ctx.shape[1]
    depth = w_ada.shape[0]
    rows = seq // GRID_W
    assert lc % CHUNK == 0 and seq % lc == 0 and seq % FFN_TM == 0 and bsz < 16

    c_all = jnp.zeros((16, d_model), F32).at[:bsz].set(c).at[bsz].set(c_ctx)
    mods4 = _ada_call(c_all, w_ada, b_ada).reshape(depth, 16, 1, 6 * d_model)
    lb_all = _lb_call(hg_lb_raw)
    lb3 = lb_all.reshape(depth, 1, 2 * MIX_W)
    lb4 = lb_all.reshape(depth, 2, 1, MIX_W)

    ws, bs, out_dtypes, kinds, w_mg, b_mg = _prep_in_proj(w_in, b_in)
    wgate, bgate = _prep_lru_gates(lru_gate_w, lru_gate_b)
    w_ffn_in_b = w_ffn_in.astype(BF16)
    w_ffn_out_b = w_ffn_out.astype(BF16)
    w_branch_b = w_branch.astype(BF16)
    w_out_b = w_out.astype(BF16)
    ln1_3 = ln1.reshape(depth, 1, d_model)
    ln2_3 = ln2.reshape(depth, 1, d_model)
    hgn = hg_norm.reshape(depth, 1, MIX_W)
    mln = ml_norm.reshape(depth, 1, MIX_W)
    conv_b3 = conv_b.reshape(depth, 1, MIX_W)
    fin = final_norm.reshape(1, d_model)

    for l in range(depth):
        last = l == depth - 1
        n_inner = GRID_W if l % 2 == 0 else rows
        hg_a, hg_f, ml_a, ml_g, lru_y, h1, a0, u0, a1, u1 = _in_proj_call(
            l, ctx, x, mods4, ln1_3, lb3, ws, bs, out_dtypes, kinds, conv_w, conv_b3, wgate, bgate, lru_lambda)
        o_hg, o_ml = _mixers_call(l, hg_a, hg_f, lb4, ml_a, ml_g, lc)
        h_f, h_b = _lru_scan_call(a0, u0, a1, u1, lc)
        branches = (o_hg, o_ml, h_f, h_b, hg_a, ml_a, lru_y, h1)
        merge_w = (w_mg, b_mg, hgn, mln, w_branch_b, w_out_b)
        x_m = _merge_call(l, False, lc, *branches, x, mods4, *merge_w)
        x = _ffn_call(l, x_m, mods4, False, ln2_3, w_ffn_in_b, w_ffn_out_b, fin,
                      tm=FFN_TM, n_inner=n_inner, final_norm=last)
        if not last:
            ctx_m = _merge_call(l, True, lc, *branches, ctx, mods4, *merge_w)
            ctx = _ffn_call(l, ctx_m, mods4, True, ln2_3, w_ffn_in_b, w_ffn_out_b, fin,
                            tm=lc, n_inner=None, final_norm=False)
    return x
```

```python
import functools
import itertools

import numpy as np
import jax
import jax.numpy as jnp
from jax import lax
from jax.experimental import pallas as pl
from jax.experimental.pallas import tpu as pltpu

F32 = jnp.float32
BF16 = jnp.bfloat16

GRID_W = 64
MIX_W = 512
N_HEADS = 4
HEAD_DV = 128
HG_DK = 128
ML_DQK = 64
LRU_BLOCKS = 8
LRU_BD = 64
LRU_C = 8.0
EPS = 1e-6
NEG_BIG = -1e30
LB_TINY = 1e-30
LOG2_E = 1.4426950408889634
GATE_LANES = 128
CHUNK = 128
HG_GROUP = 1
HG_YIELD = 3
MIX_LATE = 0
MIX_LAG = 1
MIX_NB = 8
MIX_SKEW = 15
FFN_TM = 512
FFN_ROWS = 256
MERGE_ROWS = 128
IN_PROJ_ROWS = 256
LRU_PIECE = 256
LRU_PITCH_PAD = 8
VMEM_LIMIT = 56 * 1024 * 1024


def _cparams(sem):
    return pltpu.CompilerParams(dimension_semantics=sem, vmem_limit_bytes=VMEM_LIMIT)


def _resident(shape, index_map):
    return pl.BlockSpec(shape, index_map, pipeline_mode=pl.Buffered(1))


def _sigmoid(x):
    return 1.0 / (1.0 + jnp.exp(-x))


def _sigmoid_t(x):
    return 0.5 * jnp.tanh(0.5 * x) + 0.5


def _log_sigmoid(x):
    return jnp.minimum(x, 0.0) - jnp.log(1.0 + jnp.exp(-jnp.abs(x)))


def _dot(a, b):
    return jnp.dot(a, b, preferred_element_type=F32)


def _dot_nt(a, b):
    return lax.dot_general(a, b, (((1,), (1,)), ((), ())), preferred_element_type=F32)


def _dot_tn(a, b):
    return lax.dot_general(a, b, (((0,), (0,)), ((), ())), preferred_element_type=F32)


def _split_hi_lo(x):
    hi = x.astype(BF16)
    lo = (x - hi.astype(F32)).astype(BF16)
    return hi, lo


def _norm_mod(x, ln, shift, scale):
    gain = ln * (1.0 + scale)
    return (x * lax.rsqrt(jnp.mean(x * x, axis=-1, keepdims=True) + EPS)) * gain + shift


def _mirror(m):
    return m[..., ::-1, ::-1].copy()


@functools.lru_cache(maxsize=None)
def _hgrn2_consts(C):
    n_lv = int(np.log2(C))
    mats, masks = [], []
    r = np.arange(C)
    for lv in range(n_lv):
        s = C >> (lv + 1)
        base = (r // (2 * s)) * (2 * s)
        mid = base + s - 1
        odd = r >= base + s
        m = np.zeros((C, C), np.float32)
        for t in range(C):
            if odd[t]:
                m[t, mid[t] + 1:t + 1] = 1.0
            else:
                m[t, t + 1:mid[t] + 1] = 1.0
        mats.append(m)
        same = base[:, None] == base[None, :]
        masks.append((same & odd[:, None] & (~odd)[None, :]).astype(np.float32))
    mats.append(np.tril(np.ones((C, C), np.float32)))
    masks.append(np.eye(C, dtype=np.float32))
    mats = np.stack(mats)
    masks = np.stack(masks)
    mats = np.stack([mats, _mirror(mats)])
    masks = np.stack([masks, _mirror(masks)])
    mst = mats.reshape(2, (n_lv + 1) * C, C)
    mst = np.concatenate([mst, mst], axis=-1)
    return mst, masks, n_lv


@functools.lru_cache(maxsize=None)
def _mlstm_consts(C):
    tri = np.tril(np.ones((C, C), np.float32))
    tri = np.stack([tri, _mirror(tri)])
    tri_cat = np.concatenate([tri, tri], axis=-1)
    tri_t = np.transpose(tri, (0, 2, 1))
    tri_t_cat = np.concatenate([tri_t, tri_t], axis=1)
    return tri_cat, tri_t_cat, tri


def _chunk_index(d, j, nctx, nch):
    bw = jnp.where(j < nctx, nctx - 1 - j, nch - 1 + nctx - j)
    return jnp.where(d == 0, j, bw)


def _ada_body(c_ref, w_ref, b_ref, o_ref):
    cc = c_ref[...]
    s = cc * _sigmoid(cc)
    o_ref[...] = jnp.dot(s, w_ref[...], preferred_element_type=F32,
                         precision=lax.Precision.HIGHEST) + b_ref[...]


def _ada_call(c_all, w_ada, b_ada):
    depth, d_model, n6 = w_ada.shape
    rows = c_all.shape[0]
    tn = 1536
    return pl.pallas_call(
        _ada_body,
        grid=(depth, n6 // tn),
        in_specs=[
            pl.BlockSpec((rows, d_model), lambda l, n: (0, 0)),
            pl.BlockSpec((None, d_model, tn), lambda l, n: (l, 0, n)),
            pl.BlockSpec((None, 1, tn), lambda l, n: (l, 0, n)),
        ],
        out_specs=pl.BlockSpec((None, rows, tn), lambda l, n: (l, 0, n)),
        out_shape=jax.ShapeDtypeStruct((depth, rows, n6), F32),
        compiler_params=_cparams(("arbitrary", "arbitrary")),
        name="ada_mod",
    )(c_all, w_ada, b_ada.reshape(depth, 1, n6))


def _lb_body(raw_ref, o_ref):
    raw = raw_ref[...]
    depth = raw.shape[0]
    e = jnp.exp(raw - jnp.max(raw, axis=0, keepdims=True))
    p = e / jnp.sum(e, axis=0, keepdims=True)
    acc = jnp.zeros_like(p[0:1])
    for l in range(depth):
        acc = acc + p[l:l + 1]
        o_ref[l:l + 1, :] = acc - p[0:1]


def _lb_call(hg_lb_raw):
    depth = hg_lb_raw.shape[0]
    raw = hg_lb_raw.reshape(depth, -1)
    return pl.pallas_call(
        _lb_body,
        out_shape=jax.ShapeDtypeStruct(raw.shape, F32),
        name="hgrn2_lower_bounds",
    )(raw)


def _log2_forget(f_pre, lb):
    half = 0.5 * (1.0 - lb)
    floor = jnp.maximum(lb, LB_TINY)
    return jnp.log2(jnp.maximum((floor + half) + half * jnp.tanh(0.5 * f_pre), floor))


def _lru_decay_input(r_pre, i_pre, lam, xc):
    z = -lam
    softplus = jnp.maximum(z, 0.0) + jnp.log(1.0 + jnp.exp(-jnp.abs(z)))
    log_a = (-LRU_C * softplus) * _sigmoid_t(r_pre)
    a = jnp.exp(log_a)
    one_m_a = -jnp.tanh(0.5 * log_a) * (a + 1.0)
    one_m_a2 = one_m_a * (a + 1.0)
    return one_m_a, jnp.sqrt(jnp.maximum(one_m_a2, 0.0)) * (_sigmoid_t(i_pre) * xc)


def _in_proj_body(ctx_ref, x_ref, xp_ref, xn_ref, sh_ref, sc_ref, ln_ref, lb_ref,
                  cw_ref, cb_ref, wg_ref, bg_ref, lam_ref, *rest, kinds):
    n_grp = len(kinds)
    w_refs = rest[:n_grp]
    b_refs = rest[n_grp:2 * n_grp]
    outs = list(rest[2 * n_grp:])
    ext_ref = outs.pop()
    lru_refs = [outs.pop() for _ in range(4)][::-1]
    h_ref = outs.pop()
    o_refs = iter(outs)
    o_refs = [None if kind == "lru_x" else next(o_refs) for kind in kinds]
    t = pl.program_id(1)
    tm = x_ref.shape[0]
    rows = min(tm, IN_PROJ_ROWS)
    lru_i = kinds.index("lru_x")

    def norm(x):
        return _norm_mod(x, ln_ref[...], sh_ref[...], sc_ref[...]).astype(BF16)

    def finish(kind, y, o_ref, r0):
        rs = slice(r0, r0 + rows)
        if kind == "hgrn2_qig":
            q = y[:, :MIX_W]
            o_ref[rs, :MIX_W] = (q * _sigmoid_t(q)).astype(o_ref.dtype)
            o_ref[rs, MIX_W:] = y[:, MIX_W:].astype(o_ref.dtype)
        elif kind == "hgrn2_forget":
            o_ref[rs, :] = _log2_forget(y, lb_ref[...]).astype(o_ref.dtype)
        elif kind == "mlstm_gates":
            lane = lax.broadcasted_iota(jnp.int32, y.shape, 1) % GATE_LANES
            is_f = jnp.logical_and(lane >= N_HEADS, lane < 2 * N_HEADS)
            o_ref[rs, :] = (jnp.where(is_f, _log_sigmoid(y), y) * LOG2_E).astype(o_ref.dtype)
        elif kind == "lru_x":
            ext_ref[8 + r0:8 + r0 + rows, :] = y
        else:
            o_ref[rs, :] = y.astype(o_ref.dtype)

    def row_group(r0):
        rs = slice(r0, r0 + rows)
        h = norm(jnp.where(t == 0, ctx_ref[rs, :], x_ref[rs, :]))
        h_ref[rs, :] = h
        yield
        rest_order = sorted((i for i in range(n_grp) if i != lru_i), key=lambda i: kinds[i] is None)
        for i in [lru_i] + rest_order:
            finish(kinds[i], _dot(h, w_refs[i][...]) + b_refs[i][...], o_refs[i], r0)
            yield

    def lru_gates(wait):
        prev_ok = t >= 2
        next_ok = jnp.logical_and(t >= 1, t < pl.num_programs(1) - 1)
        w_lx, b_lx = w_refs[lru_i][...], b_refs[lru_i][...]
        ext_ref[0:8, :] = jnp.where(prev_ok, _dot(norm(xp_ref[...]), w_lx) + b_lx, 0.0)
        ext_ref[8 + tm:16 + tm, :] = jnp.where(next_ok, _dot(norm(xn_ref[...]), w_lx) + b_lx, 0.0)
        for _ in range(wait):
            yield
        cw = cw_ref[...]
        xc = (cw[0:1] * ext_ref[6:6 + tm, :] + cw[1:2] * ext_ref[7:7 + tm, :] + cw[2:3] * ext_ref[8:8 + tm, :]
              + cw[3:4] * ext_ref[9:9 + tm, :]) + cb_ref[...]
        yield
        xcb = xc.astype(BF16)
        gates = [_dot(xcb, wg_ref[n]) + bg_ref[:, n * MIX_W:(n + 1) * MIX_W] for n in range(4)]
        yield
        for d in range(2):
            for r0 in range(0, tm, LRU_PIECE):
                rs = slice(r0, r0 + LRU_PIECE)
                one_m_a, u = _lru_decay_input(gates[2 * d][rs, :], gates[2 * d + 1][rs, :],
                                              lam_ref[d:d + 1, :], xc[rs, :])
                lru_refs[2 * d][rs, :] = one_m_a.astype(lru_refs[2 * d].dtype)
                lru_refs[2 * d + 1][rs, :] = u.astype(lru_refs[2 * d + 1].dtype)
                yield

    groups = [row_group(r0) for r0 in range(0, tm, rows)]
    gens = groups + [lru_gates(wait=len(groups) + 1)]
    live = list(gens)
    step = 0
    while live:
        for i, g in enumerate(gens):
            if g in live and (step >= i or g is gens[-1]):
                try:
                    next(g)
                except StopIteration:
                    live.remove(g)
        step += 1


def _in_proj_call(l, ctx, x, mods4, ln1, lb3, ws, bs, out_dtypes, kinds, conv_w, conv_b, wgate, bgate, lam):
    bsz, lc, d_model = ctx.shape
    seq = x.shape[1]
    tm = lc
    nt = 1 + seq // tm
    h8 = tm // 8

    def mod_spec(chunk):
        return pl.BlockSpec((None, None, 1, d_model),
                            lambda b, t: (l, jnp.where(t == 0, bsz, b), 0, chunk))

    def per_layer(shape):
        return pl.BlockSpec((None,) + shape, lambda b, t: (l,) + (0,) * len(shape))

    in_specs = [
        pl.BlockSpec((None, tm, d_model), lambda b, t: (b, 0, 0)),
        pl.BlockSpec((None, tm, d_model), lambda b, t: (b, jnp.maximum(t - 1, 0), 0)),
        pl.BlockSpec((None, 8, d_model), lambda b, t: (b, jnp.maximum((t - 1) * h8 - 1, 0), 0)),
        pl.BlockSpec((None, 8, d_model), lambda b, t: (b, jnp.minimum(jnp.maximum(t, 1) * h8, seq // 8 - 1), 0)),
        mod_spec(0), mod_spec(1),
        per_layer((1, d_model)),
        per_layer((1, lb3.shape[-1])),
        per_layer((4, MIX_W)), per_layer((1, MIX_W)),
        per_layer((4, MIX_W, MIX_W)), per_layer((1, 4 * MIX_W)), per_layer((2, MIX_W)),
    ]
    in_specs += [per_layer((d_model, w.shape[-1])) for w in ws]
    in_specs += [per_layer((1, w.shape[-1])) for w in ws]
    tok = lambda width: pl.BlockSpec((None, tm, width), lambda b, t: (b, t, 0))
    out_specs, out_shape = [], []
    for w, dt, kind in zip(ws, out_dtypes, kinds):
        if kind != "lru_x":
            out_specs.append(tok(w.shape[-1]))
            out_shape.append(jax.ShapeDtypeStruct((bsz, lc + seq, w.shape[-1]), dt))
    out_specs.append(tok(d_model))
    out_shape.append(jax.ShapeDtypeStruct((bsz, lc + seq, d_model), BF16))
    out_specs += [tok(MIX_W)] * 4
    out_shape += [jax.ShapeDtypeStruct((bsz, lc + seq, MIX_W), BF16)] * 4
    return pl.pallas_call(
        functools.partial(_in_proj_body, kinds=kinds),
        grid=(bsz, nt),
        in_specs=in_specs,
        out_specs=out_specs,
        out_shape=out_shape,
        scratch_shapes=[pltpu.VMEM((tm + 16, MIX_W), F32)],
        compiler_params=_cparams(("arbitrary", "arbitrary")),
        name="in_proj",
    )(ctx, x, x, x, mods4, mods4, ln1, lb3, conv_w, conv_b, wgate, bgate, lam, *ws, *bs)


def _hgrn2_body(qv_ref, g_ref, lb_ref, mst_ref, msk_ref, o_ref, st_ref, *, C, n_lv):
    q16 = qv_ref[:, :MIX_W]
    v = qv_ref[:, MIX_W:]
    q = q16.astype(F32)
    g = g_ref[...]
    lb = lb_ref[...]
    kk = (1.0 - jnp.exp2(g)) + (jnp.maximum(lb, LB_TINY) - lb)
    g_hi, g_lo = _split_hi_lo(g)
    ex = _dot(mst_ref[...], jnp.concatenate([g_hi, g_lo], axis=0))
    tot = jnp.sum(g, axis=0, keepdims=True)
    k16 = kk.astype(BF16)

    for h0 in range(0, N_HEADS, HG_GROUP):
        heads = range(h0, h0 + HG_GROUP)
        sls = {h: slice(h * HG_DK, (h + 1) * HG_DK) for h in heads}
        p = {h: msk_ref[n_lv] * _dot_nt(q16[:, sls[h]], k16[:, sls[h]]) for h in heads}
        for lv in range(n_lv):
            for h in heads:
                w = jnp.exp2(ex[lv * C:(lv + 1) * C, sls[h]]).astype(BF16)
                p[h] = p[h] + msk_ref[lv] * _dot_nt(q16[:, sls[h]] * w, k16[:, sls[h]] * w)
            if lv % HG_YIELD == HG_YIELD - 1:
                yield
        for h in heads:
            sl = sls[h]
            st = st_ref[h]
            b_in = ex[n_lv * C:(n_lv + 1) * C, sl]
            qb = (q[:, sl] * jnp.exp2(b_in)).astype(BF16)
            vh = v[:, sl]
            o_ref[:, sl] = (_dot(p[h].astype(BF16), vh) + _dot_nt(qb, st.astype(BF16))).astype(o_ref.dtype)
            tot_h = tot[:, sl]
            kb = (kk[:, sl] * jnp.exp2(tot_h - b_in)).astype(BF16)
            st_ref[h] = st * jnp.exp2(tot_h) + _dot_tn(vh, kb)
            yield


def _mlstm_body(qkv_ref, g_ref, tri_ref, trit_ref, msk_ref, o_ref, c_ref, m_ref, *, C):
    qk_w = N_HEADS * ML_DQK
    gates = g_ref[...]
    gates_t = gates.T
    lf_hi, lf_lo = _split_hi_lo(gates)
    b_cols = _dot(tri_ref[...], jnp.concatenate([lf_hi, lf_lo], axis=0))
    lft_hi, lft_lo = _split_hi_lo(gates_t)
    b_rows = _dot(jnp.concatenate([lft_hi, lft_lo], axis=1), trit_ref[...])
    tot = jnp.sum(gates, axis=0, keepdims=True)
    allowed = msk_ref[...] > 0.0
    ones = jnp.ones((C, HEAD_DV), BF16)
    log2_scale = 0.5 * np.log2(ML_DQK)

    for h in range(N_HEADS):
        b_col = b_cols[:, N_HEADS + h:N_HEADS + h + 1]
        c_col = gates[:, h:h + 1] - b_col
        c_row = gates_t[h:h + 1, :] - b_rows[N_HEADS + h:N_HEADS + h + 1, :]
        m_prev = m_ref[h][0:1, 0:1]
        cm = jnp.where(allowed, c_row, NEG_BIG)
        m_t = jnp.maximum(jnp.max(cm, axis=-1, keepdims=True), m_prev)
        yield
        qh = qkv_ref[:, h * ML_DQK:(h + 1) * ML_DQK]
        kh = qkv_ref[:, qk_w + h * ML_DQK:qk_w + (h + 1) * ML_DQK]
        vh = qkv_ref[:, 2 * qk_w + h * HEAD_DV:2 * qk_w + (h + 1) * HEAD_DV]
        vaug = jnp.concatenate([vh, ones], axis=1)
        s = _dot_nt(qh, kh) * jnp.exp2(cm - (m_t + log2_scale))
        qs = (qh.astype(F32) * jnp.exp2(m_prev - m_t - log2_scale)).astype(BF16)
        yield
        c_aug = c_ref[h]
        r = _dot(jnp.concatenate([s.astype(BF16), qs], axis=1),
                 jnp.concatenate([vaug, c_aug.astype(BF16)], axis=0))
        num = r[:, :HEAD_DV]
        den = r[:, HEAD_DV:]
        hout = num / jnp.maximum(jnp.abs(den), jnp.exp2(-(b_col + m_t)))
        o_ref[:, h * HEAD_DV:(h + 1) * HEAD_DV] = hout.astype(o_ref.dtype)
        yield
        b_last = tot[:, N_HEADS + h:N_HEADS + h + 1]
        log_w = b_last + c_col
        m_new = jnp.maximum(b_last + m_prev, jnp.max(log_w, axis=0, keepdims=True))
        kw = (kh.astype(F32) * jnp.exp2(log_w - m_new)).astype(BF16)
        c_ref[h] = jnp.exp2(b_last + m_prev - m_new) * c_aug + _dot_tn(kw, vaug)
        m_ref[h] = jnp.broadcast_to(m_new, m_ref.shape[1:])
        yield


def _mixers_body(hqv_ref, hg_ref, lb_ref, mst_ref, hmsk_ref, mqkv_ref, mg_ref,
                 tri_ref, trit_ref, mmsk_ref, o_hg_ref, o_ml_ref, st_ref, c_ref, m_ref, *, C, n_lv):
    @pl.when(pl.program_id(2) == 0)
    def _():
        st_ref[...] = jnp.zeros_like(st_ref)
        c_ref[...] = jnp.zeros_like(c_ref)
        m_ref[...] = jnp.zeros_like(m_ref)

    def element(bb):
        gens = [_hgrn2_body(hqv_ref.at[bb], hg_ref.at[bb], lb_ref, mst_ref, hmsk_ref,
                            o_hg_ref.at[bb], st_ref.at[bb], C=C, n_lv=n_lv),
                _mlstm_body(mqkv_ref.at[bb], mg_ref.at[bb], tri_ref, trit_ref,
                            mmsk_ref, o_ml_ref.at[bb], c_ref.at[bb], m_ref.at[bb], C=C)]
        gens[MIX_LATE] = itertools.chain([None] * MIX_LAG, gens[MIX_LATE])
        for _ in itertools.zip_longest(*gens):
            yield

    streams = [element(bb) for bb in range(hqv_ref.shape[0])]
    live = []
    while streams or live:
        if streams and (not live or live[-1][1] >= MIX_SKEW):
            live.append([streams.pop(0), 0])
        for item in list(live):
            try:
                next(item[0])
                item[1] += 1
            except StopIteration:
                live.remove(item)


def _mixers_call(l, hg_a, hg_f, lb4, ml_a, ml_g, lc):
    bsz, tt, _ = hg_a.shape
    C = CHUNK
    nch, nctx = tt // C, lc // C
    mst, hmsk, n_lv = _hgrn2_consts(C)
    mst = jnp.asarray(mst, BF16)
    hmsk = jnp.asarray(hmsk, F32)
    tri_cat, tri_t_cat, tri = _mlstm_consts(C)
    tri_cat = jnp.asarray(tri_cat, BF16)
    tri_t_cat = jnp.asarray(tri_t_cat, BF16)
    mmsk = jnp.asarray(tri, F32)
    cidx = functools.partial(_chunk_index, nctx=nctx, nch=nch)
    qk_w = N_HEADS * ML_DQK
    tok = lambda col: (lambda b, d, j: (b, cidx(d, j), col))
    tok_d = lambda b, d, j: (b, cidx(d, j), d)
    per_dir3 = lambda b, d, j: (d, 0, 0)
    nb = max(n for n in range(1, MIX_NB + 1) if bsz % n == 0)
    o_spec = pl.BlockSpec((nb, None, C, MIX_W), lambda b, d, j: (b, d, cidx(d, j), 0))
    o_shape = jax.ShapeDtypeStruct((bsz, 2, tt, MIX_W), BF16)
    return pl.pallas_call(
        functools.partial(_mixers_body, C=C, n_lv=n_lv),
        grid=(bsz // nb, 2, nch),
        in_specs=[
            pl.BlockSpec((nb, C, 2 * MIX_W), tok(0)),
            pl.BlockSpec((nb, C, MIX_W), tok_d),
            pl.BlockSpec((None, None, 1, MIX_W), lambda b, d, j: (l, d, 0, 0)),
            pl.BlockSpec((None,) + mst.shape[1:], per_dir3),
            pl.BlockSpec((None,) + hmsk.shape[1:], lambda b, d, j: (d, 0, 0, 0)),
            pl.BlockSpec((nb, C, 2 * qk_w + MIX_W), tok(0)),
            pl.BlockSpec((nb, C, GATE_LANES), tok_d),
            pl.BlockSpec((None, C, 2 * C), per_dir3),
            pl.BlockSpec((None, 2 * C, C), per_dir3),
            pl.BlockSpec((None, C, C), per_dir3),
        ],
        out_specs=[o_spec, o_spec],
        out_shape=[o_shape, o_shape],
        scratch_shapes=[pltpu.VMEM((nb, N_HEADS, HEAD_DV, HG_DK), F32),
                        pltpu.VMEM((nb, N_HEADS, ML_DQK, 2 * HEAD_DV), F32),
                        pltpu.VMEM((nb, N_HEADS, 8, 128), F32)],
        compiler_params=_cparams(("arbitrary", "arbitrary", "arbitrary")),
        name="mixers",
    )(hg_a, hg_f, lb4, mst, hmsk, ml_a, ml_g, tri_cat, tri_t_cat, mmsk)


def _lru_scan_body(a0_ref, u0_ref, a1_ref, u1_ref, hf_ref, hb_ref, sf_ref, sb_ref,
                   pa0_ref, pu0_ref, pa1_ref, pu1_ref, *, tc):
    @pl.when(pl.program_id(0) == 0)
    def _():
        sf_ref[...] = jnp.zeros_like(sf_ref)
        sb_ref[...] = jnp.zeros_like(sb_ref)

    for src, dst in ((u0_ref, pu0_ref), (u1_ref, pu1_ref)):
        dst[:, 0:tc, :] = src[...].astype(F32)
    for src, dst in ((a0_ref, pa0_ref), (a1_ref, pa1_ref)):
        dst[:, 0:tc, :] = 1.0 - src[...].astype(F32)

    def step(i, carry):
        hf, hb = carry
        hf = pa0_ref[:, i, :] * hf + pu0_ref[:, i, :]
        pu0_ref[:, i, :] = hf
        ib = tc - 1 - i
        hb = pa1_ref[:, ib, :] * hb + pu1_ref[:, ib, :]
        pu1_ref[:, ib, :] = hb
        return hf, hb

    hf, hb = lax.fori_loop(0, tc, step, (sf_ref[...], sb_ref[...]), unroll=8)
    sf_ref[...] = hf
    sb_ref[...] = hb
    hf_ref[...] = pu0_ref[:, 0:tc, :].astype(hf_ref.dtype)
    hb_ref[...] = pu1_ref[:, 0:tc, :].astype(hb_ref.dtype)


def _lru_scan_call(a0, u0, a1, u1, lc):
    bsz, tt, w = a0.shape
    tc = CHUNK
    nch, nctx = tt // tc, lc // tc
    fw = pl.BlockSpec((bsz, tc, w), lambda j: (0, j, 0))
    bw = pl.BlockSpec((bsz, tc, w), lambda j: (0, _chunk_index(1, j, nctx, nch), 0))
    out = jax.ShapeDtypeStruct((bsz, tt, w), BF16)
    return pl.pallas_call(
        functools.partial(_lru_scan_body, tc=tc),
        grid=(nch,),
        in_specs=[fw, fw, bw, bw],
        out_specs=[fw, bw],
        out_shape=[out, out],
        scratch_shapes=[pltpu.VMEM((bsz, w), F32), pltpu.VMEM((bsz, w), F32)]
        + [pltpu.VMEM((bsz, tc + LRU_PITCH_PAD, w), F32)] * 4,
        compiler_params=_cparams(("arbitrary",)),
        name="rglru_scan",
    )(a0, u0, a1, u1)


def _head_rms(o, w):
    parts = []
    for h in range(N_HEADS):
        oh = o[:, h * HEAD_DV:(h + 1) * HEAD_DV]
        parts.append(oh * lax.rsqrt(jnp.mean(oh * oh, axis=-1, keepdims=True) + EPS))
    return jnp.concatenate(parts, axis=1) * w


def _gelu_tanh(x):
    return 0.5 * x * (1.0 + jnp.tanh(0.7978845608028654 * (x + 0.044715 * (x * x * x))))


def _merge_rows(hg0_ref, hg1_ref, ml0_ref, ml1_ref, lf_ref, lb_ref, hgg_ref, mlo_ref, ly_ref, h_ref,
                res_ref, gate_ref, wmg_ref, bmg_ref, hgn_ref, mln_ref, wb_ref, wo_ref, o_ref,
                r0, rows, d_model):
    rs = slice(r0, r0 + rows)
    hgg = hgg_ref[rs, :].astype(F32)
    o_hg = hg0_ref[rs, :].astype(F32) + hg1_ref[rs, :].astype(F32)
    o_ml = ml0_ref[rs, :].astype(F32) + ml1_ref[rs, :].astype(F32)
    a_out = _head_rms(o_hg, hgn_ref[...]) * (hgg * _sigmoid_t(hgg))
    b_out = _head_rms(o_ml, mln_ref[...]) * _sigmoid_t(mlo_ref[rs, :].astype(F32))
    h_lru = lf_ref[rs, :].astype(F32) + lb_ref[rs, :].astype(F32)
    c_out = h_lru * _gelu_tanh(ly_ref[rs, :].astype(F32))
    h = h_ref[rs, :]
    yield
    merged = None
    for n, br in enumerate((a_out, b_out, c_out)):
        cs = slice(n * d_model, (n + 1) * d_model)
        gate = _sigmoid_t((_dot(h, wmg_ref[:, cs]) + bmg_ref[:, cs]).astype(BF16))
        term = gate * _dot(br.astype(BF16), wb_ref[n]).astype(BF16)
        merged = term if merged is None else merged + term
        yield
    o_ref[rs, :] = res_ref[rs, :] + gate_ref[...] * _dot(merged, wo_ref[...])
    yield


def _merge_body(*refs, rows, d_model):
    tm = refs[-1].shape[0]
    gens = [_merge_rows(*refs, r0, rows, d_model) for r0 in range(0, tm, rows)]
    n_phase = 5
    for step in range(n_phase + len(gens) - 1):
        for i, g in enumerate(gens):
            if 0 <= step - i < n_phase:
                next(g)


def _merge_call(l, is_ctx, tm, o_hg, o_ml, h_f, h_b, hg_a, ml_a, lru_y, h1, res, mods4,
                w_mg, b_mg, hg_norm, ml_norm, w_branch, w_out):
    bsz, n, d_model = res.shape
    t0 = 0 if is_ctx else 1
    rows = min(tm, MERGE_ROWS)

    def tok(col):
        return lambda b, t: (b, t + t0, col)

    mix = lambda d: pl.BlockSpec((None, None, tm, MIX_W), lambda b, t: (b, d, t + t0, 0))
    in_specs = [
        mix(0), mix(1), mix(0), mix(1),
        pl.BlockSpec((None, tm, MIX_W), tok(0)),
        pl.BlockSpec((None, tm, MIX_W), tok(0)),
        pl.BlockSpec((None, tm, MIX_W), tok(2)),
        pl.BlockSpec((None, tm, MIX_W), tok(2)),
        pl.BlockSpec((None, tm, MIX_W), tok(0)),
        pl.BlockSpec((None, tm, d_model), tok(0)),
        pl.BlockSpec((None, tm, d_model), lambda b, t: (b, t, 0)),
        pl.BlockSpec((None, None, 1, d_model), lambda b, t: (l, bsz if is_ctx else b, 0, 2)),
        _resident((None, d_model, 3 * d_model), lambda b, t: (l, 0, 0)),
        pl.BlockSpec((None, 1, 3 * d_model), lambda b, t: (l, 0, 0)),
        pl.BlockSpec((None, 1, MIX_W), lambda b, t: (l, 0, 0)),
        pl.BlockSpec((None, 1, MIX_W), lambda b, t: (l, 0, 0)),
        _resident((None, 3, MIX_W, d_model), lambda b, t: (l, 0, 0, 0)),
        _resident((None, d_model, d_model), lambda b, t: (l, 0, 0)),
    ]
    return pl.pallas_call(
        functools.partial(_merge_body, rows=rows, d_model=d_model),
        grid=(bsz, n // tm),
        in_specs=in_specs,
        out_specs=pl.BlockSpec((None, tm, d_model), lambda b, t: (b, t, 0)),
        out_shape=jax.ShapeDtypeStruct(res.shape, F32),
        compiler_params=_cparams(("arbitrary", "arbitrary")),
        name="merge",
    )(o_hg, o_hg, o_ml, o_ml, h_f, h_b, hg_a, ml_a, lru_y, h1, res, mods4,
      w_mg, b_mg, hg_norm, ml_norm, w_branch, w_out)


def _ffn_rows(x_ref, sh_ref, sc_ref, gt_ref, ln_ref, wi_ref, wo_ref, fn_ref, o_ref, r0, rows, *,
              n_inner, final_norm):
    x = x_ref[r0:r0 + rows, :]
    h = _norm_mod(x, ln_ref[...], sh_ref[...], sc_ref[...]).astype(BF16)
    yield
    gu = _dot(h, wi_ref[...])
    yield
    hidden = gu.shape[-1] // 2
    gate = gu[:, :hidden]
    act = (gate * _sigmoid_t(gate) * gu[:, hidden:]).astype(BF16)
    yield
    f = _dot(act, wo_ref[...])
    yield
    y = x + gt_ref[...] * f
    if final_norm:
        y = y * lax.rsqrt(jnp.mean(y * y, axis=-1, keepdims=True) + EPS) * fn_ref[...]
    if n_inner is None:
        o_ref[r0:r0 + rows, :] = y
    else:
        for a in range(rows // n_inner):
            o_ref[:, r0 // n_inner + a, :] = y[a * n_inner:(a + 1) * n_inner, :]
    yield


def _ffn_body(x_ref, sh_ref, sc_ref, gt_ref, ln_ref, wi_ref, wo_ref, fn_ref, o_ref, *, rows, n_inner,
              final_norm):
    gens = [_ffn_rows(x_ref, sh_ref, sc_ref, gt_ref, ln_ref, wi_ref, wo_ref, fn_ref, o_ref, r0, rows,
                      n_inner=n_inner, final_norm=final_norm)
            for r0 in range(0, x_ref.shape[0], rows)]
    skew = 3
    n_phase = 5
    for step in range(n_phase + skew * (len(gens) - 1)):
        for i, g in enumerate(gens):
            if 0 <= step - skew * i < n_phase:
                next(g)


def _ffn_call(l, x, mods4, ctx_row, ln2, wi, wo, fin, *, tm, n_inner, final_norm):
    bsz, n, d_model = x.shape
    hidden = wo.shape[1]
    nt = n // tm
    rows = min(tm, FFN_ROWS)

    def mod_spec(chunk):
        return pl.BlockSpec((None, None, 1, d_model),
                            lambda b, t: (l, bsz if ctx_row else b, 0, chunk))

    if n_inner is None:
        o_spec = pl.BlockSpec((None, tm, d_model), lambda b, t: (b, t, 0))
        o_shape = jax.ShapeDtypeStruct(x.shape, F32)
    else:
        assert rows % n_inner == 0
        n_a = tm // n_inner
        o_spec = pl.BlockSpec((None, n_inner, n_a, d_model), lambda b, t: (b, 0, t, 0))
        o_shape = jax.ShapeDtypeStruct((bsz, n_inner, n // n_inner, d_model), F32)
    out = pl.pallas_call(
        functools.partial(_ffn_body, rows=rows, n_inner=n_inner, final_norm=final_norm),
        grid=(bsz, nt),
        in_specs=[
            pl.BlockSpec((None, tm, d_model), lambda b, t: (b, t, 0)),
            mod_spec(3), mod_spec(4), mod_spec(5),
            pl.BlockSpec((None, 1, d_model), lambda b, t: (l, 0, 0)),
            _resident((None, d_model, 2 * hidden), lambda b, t: (l, 0, 0)),
            _resident((None, hidden, d_model), lambda b, t: (l, 0, 0)),
            pl.BlockSpec((1, d_model), lambda b, t: (0, 0)),
        ],
        out_specs=o_spec,
        out_shape=o_shape,
        compiler_params=_cparams(("arbitrary", "arbitrary")),
        name="ffn",
    )(x, mods4, mods4, mods4, ln2, wi, wo, fin)
    return out.reshape(bsz, n, d_model)


def _prep_in_proj(w_in, b_in):
    depth, d_model, _ = w_in.shape
    hq = N_HEADS * HG_DK
    sizes = (hq, MIX_W, MIX_W, 2 * hq, N_HEADS * ML_DQK, N_HEADS * ML_DQK, MIX_W, MIX_W,
             2 * N_HEADS, 2 * N_HEADS, MIX_W, MIX_W, 3 * d_model)
    offs = np.concatenate([[0], np.cumsum(sizes)])
    assert offs[-1] == w_in.shape[-1]

    def cols(a, lo, hi):
        return a[..., int(offs[lo]):int(offs[hi])]

    def gate_block(a):
        ig = cols(a, 8, 9).reshape(a.shape[:-1] + (2, N_HEADS))
        fg = cols(a, 9, 10).reshape(a.shape[:-1] + (2, N_HEADS))
        pad = jnp.zeros(a.shape[:-1] + (2, GATE_LANES - 2 * N_HEADS), a.dtype)
        return jnp.concatenate([ig, fg, pad], axis=-1).reshape(a.shape[:-1] + (2 * GATE_LANES,))

    groups = [
        (lambda a: cols(a, 0, 3), BF16),
        (lambda a: cols(a, 3, 4), F32),
        (lambda a: cols(a, 4, 8), BF16),
        (gate_block, F32),
        (lambda a: cols(a, 10, 11), F32),
        (lambda a: cols(a, 11, 12), BF16),
    ]
    merge_gates = lambda a: cols(a, 12, 13)
    kinds = ("hgrn2_qig", "hgrn2_forget", None, "mlstm_gates", "lru_x", None)
    b3 = b_in.reshape(depth, 1, -1)
    ws = [f(w_in).astype(BF16) for f, _ in groups]
    bs = [f(b3) for f, _ in groups]
    w_mg = merge_gates(w_in).astype(BF16)
    b_mg = merge_gates(b3)
    return ws, bs, [dt for _, dt in groups], kinds, w_mg, b_mg


def _prep_lru_gates(lru_gate_w, lru_gate_b):
    depth = lru_gate_w.shape[0]
    eye = jnp.eye(LRU_BLOCKS, dtype=lru_gate_w.dtype)
    dense = lru_gate_w[:, :, :, :, :, None, :] * eye[:, None, :, None]
    dense = dense.reshape(depth, 4, MIX_W, MIX_W).astype(BF16)
    return dense, lru_gate_b.reshape(depth, 1, 4 * MIX_W)


def kernel(x, c, ctx, c_ctx, w_ada, b_ada, ln1, w_in, b_in, hg_lb_raw, hg_norm, ml_norm, conv_w, conv_b,
           lru_gate_w, lru_gate_b, lru_lambda, w_branch, w_out, ln2, w_ffn_in, w_ffn_out, final_norm):
    bsz, seq, d_model = x.shape
    lc = ctx.shape[1]
    depth = w_ada.shape[0]
    rows = seq // GRID_W
    assert lc % CHUNK == 0 and seq % lc == 0 and seq % FFN_TM == 0 and bsz < 16

    c_all = jnp.zeros((16, d_model), F32).at[:bsz].set(c).at[bsz].set(c_ctx)
    mods4 = _ada_call(c_all, w_ada, b_ada).reshape(depth, 16, 1, 6 * d_model)
    lb_all = _lb_call(hg_lb_raw)
    lb3 = lb_all.reshape(depth, 1, 2 * MIX_W)
    lb4 = lb_all.reshape(depth, 2, 1, MIX_W)

    ws, bs, out_dtypes, kinds, w_mg, b_mg = _prep_in_proj(w_in, b_in)
    wgate, bgate = _prep_lru_gates(lru_gate_w, lru_gate_b)
    w_ffn_in_b = w_ffn_in.astype(BF16)
    w_ffn_out_b = w_ffn_out.astype(BF16)
    w_branch_b = w_branch.astype(BF16)
    w_out_b = w_out.astype(BF16)
    ln1_3 = ln1.reshape(depth, 1, d_model)
    ln2_3 = ln2.reshape(depth, 1, d_model)
    hgn = hg_norm.reshape(depth, 1, MIX_W)
    mln = ml_norm.reshape(depth, 1, MIX_W)
    conv_b3 = conv_b.reshape(depth, 1, MIX_W)
    fin = final_norm.reshape(1, d_model)

    for l in range(depth):
        last = l == depth - 1
        n_inner = GRID_W if l % 2 == 0 else rows
        hg_a, hg_f, ml_a, ml_g, lru_y, h1, a0, u0, a1, u1 = _in_proj_call(
            l, ctx, x, mods4, ln1_3, lb3, ws, bs, out_dtypes, kinds, conv_w, conv_b3, wgate, bgate, lru_lambda)
        o_hg, o_ml = _mixers_call(l, hg_a, hg_f, lb4, ml_a, ml_g, lc)
        h_f, h_b = _lru_scan_call(a0, u0, a1, u1, lc)
        branches = (o_hg, o_ml, h_f, h_b, hg_a, ml_a, lru_y, h1)
        merge_w = (w_mg, b_mg, hgn, mln, w_branch_b, w_out_b)
        x_m = _merge_call(l, False, lc, *branches, x, mods4, *merge_w)
        x = _ffn_call(l, x_m, mods4, False, ln2_3, w_ffn_in_b, w_ffn_out_b, fin,
                      tm=FFN_TM, n_inner=n_inner, final_norm=last)
        if not last:
            ctx_m = _merge_call(l, True, lc, *branches, ctx, mods4, *merge_w)
            ctx = _ffn_call(l, ctx_m, mods4, True, ln2_3, w_ffn_in_b, w_ffn_out_b, fin,
                            tm=lc, n_inner=None, final_norm=False)
    return x
```
